```python
import math
import jax
import jax.numpy as jnp
from jax import lax
import numpy as np

D_MODEL = 1024
BATCH = 32
SEQ = 2048
DEPTH = 1
DEC_BATCH = 32
DEC_SEQ = 64
PAST_LEN = 4096

CHUNK = 64
Q_BLOCK = 128
FOX_HD = 64
FOX_HEADS = (D_MODEL // 2) // FOX_HD
FOX_W = FOX_HEADS * FOX_HD
DIFF_HD = 64
DIFF_HEADS = (D_MODEL // 2) // (2 * DIFF_HD)
DIFF_W = DIFF_HEADS * 2 * DIFF_HD
MIX_WIDTH = FOX_W + DIFF_W
OFF_FQ = 0
OFF_FK = OFF_FQ + FOX_W
OFF_FV = OFF_FK + FOX_W
OFF_FF = OFF_FV + FOX_W
OFF_DQ = OFF_FF + FOX_HEADS
OFF_DK = OFF_DQ + DIFF_W
OFF_DV = OFF_DK + DIFF_W
IN_COLS = OFF_DV + DIFF_W
NUM_BUCKETS = 32
MAX_DISTANCE = 128
N_EXPERTS = 32
TOP_K = 4
D_FF = D_MODEL
SWIGLU_LIMIT = 7.0
SWIGLU_ALPHA = 1.702
PLE_DIM = 256
NORM_EPS = 1e-6
SUBLN_EPS = 1e-5
NEG_INF = -1e30
FORGET_BIAS_INIT = 2.0

kernel_name = "fox_diff_hybrid_moe_stream_step"


def _rmsnorm(x, g, eps=NORM_EPS):
    xf = x.astype(jnp.float32)
    y = xf * lax.rsqrt(jnp.mean(xf * xf, axis=-1, keepdims=True) + eps)
    return (y * g.astype(jnp.float32)).astype(x.dtype)


def _t5_bucket(rel):
    nb = NUM_BUCKETS // 2
    max_exact = nb // 2
    ret = jnp.where(rel > 0, nb, 0)
    n = jnp.abs(rel)
    nf = jnp.maximum(n, max_exact).astype(jnp.float32)
    large = max_exact + (jnp.log(nf / max_exact) / math.log(MAX_DISTANCE / max_exact)
                         * (nb - max_exact)).astype(jnp.int32)
    large = jnp.minimum(large, nb - 1)
    return ret + jnp.where(n < max_exact, n, large)


def _sweep_queries(block_fn, tq):
    qb = min(Q_BLOCK, tq)
    starts = jnp.arange(tq // qb, dtype=jnp.int32) * qb
    out = lax.map(lambda s: block_fn(s, qb), starts)
    out = jnp.moveaxis(out, 0, 1)
    b, nb = out.shape[0], out.shape[1]
    return out.reshape((b, nb * qb) + out.shape[3:])


def _fox_attention(q, k, v, c_q, c_k, q_pos, k_pos):
    scale = 1.0 / math.sqrt(FOX_HD)
    c_k_t = jnp.transpose(c_k, (0, 2, 1))[:, :, None, :]

    def block(start, qb):
        qblk = lax.dynamic_slice_in_dim(q, start, qb, axis=1)
        cq = lax.dynamic_slice_in_dim(c_q, start, qb, axis=1)
        qp = lax.dynamic_slice_in_dim(q_pos, start, qb)
        s = (jnp.einsum('bqhd,bkhd->bhqk', qblk, k).astype(jnp.float32) * scale
             + jnp.transpose(cq, (0, 2, 1))[..., None] - c_k_t)
        s = jnp.where(k_pos[None, :] <= qp[:, None], s, NEG_INF)
        p = jax.nn.softmax(s, axis=-1)
        return jnp.einsum('bhqk,bkhd->bqhd', p.astype(v.dtype), v)

    return _sweep_queries(block, q.shape[1])


def _diff_attention(q, k, v, q_pos, k_pos, rel_table, lam):
    scale = 1.0 / math.sqrt(DIFF_HD)

    def block(start, qb):
        qblk = lax.dynamic_slice_in_dim(q, start, qb, axis=1)
        qp = lax.dynamic_slice_in_dim(q_pos, start, qb)
        s = jnp.einsum('bqhmd,bkhmd->bhmqk', qblk, k).astype(jnp.float32) * scale
        bias = rel_table[_t5_bucket(k_pos[None, :] - qp[:, None])]
        s = s + jnp.transpose(bias, (2, 0, 1))[None, :, None].astype(jnp.float32)
        visible = (k_pos[None, :] // CHUNK) <= (qp[:, None] // CHUNK)
        s = jnp.where(visible, s, NEG_INF)
        a = jax.nn.softmax(s, axis=-1)
        w = a[:, :, 0] - lam * a[:, :, 1]
        return jnp.einsum('bhqk,bkhe->bqhe', w.astype(v.dtype), v)

    return _sweep_queries(block, q.shape[1])


def _moe(h, w_r, b_r, w_gate, b_gate, w_up, b_up, w_down, b_down):
    b, t, d = h.shape
    hf = h.reshape(b * t, d)
    logits = (hf @ w_r + b_r).astype(jnp.float32)
    top_v, top_i = lax.top_k(logits, TOP_K)
    top_w = jax.nn.softmax(top_v, axis=-1)
    combine = jnp.sum(jax.nn.one_hot(top_i, N_EXPERTS, dtype=jnp.float32) * top_w[..., None], axis=1)

    def expert(acc, xs):
        wg, bg, wu, bu, wd, bd, c = xs
        g = jnp.minimum(hf @ wg + bg, SWIGLU_LIMIT)
        u = jnp.clip(hf @ wu + bu, -SWIGLU_LIMIT, SWIGLU_LIMIT)
        y = ((u + 1.0) * (g * jax.nn.sigmoid(SWIGLU_ALPHA * g))) @ wd + bd
        return acc + c[:, None].astype(y.dtype) * y, None

    y, _ = lax.scan(expert, jnp.zeros_like(hf),
                    (w_gate, b_gate, w_up, b_up, w_down, b_down, combine.T))
    return y.reshape(b, t, d)


def _layer(x, p_i, past, lw, lambda_init):
    (g_mix, w_in, b_forget, diff_lambda, diff_subln_g, rel_table, w_o, g_ffn,
     w_r, b_r, w_gate, b_gate, w_up, b_up, w_down, b_down, g_ple, w_ple_gate, w_ple_proj) = lw
    pk_f, pv_f, plogf, pk_d, pv_d = past
    b, t, _ = x.shape
    past_len = pk_f.shape[1]
    h = _rmsnorm(x, g_mix)
    u = h @ w_in
    fq = u[..., OFF_FQ:OFF_FK].reshape(b, t, FOX_HEADS, FOX_HD)
    fk = u[..., OFF_FK:OFF_FV].reshape(b, t, FOX_HEADS, FOX_HD)
    fv = u[..., OFF_FV:OFF_FF].reshape(b, t, FOX_HEADS, FOX_HD)
    logf = jax.nn.log_sigmoid((u[..., OFF_FF:OFF_DQ] + b_forget).astype(jnp.float32))
    dq = u[..., OFF_DQ:OFF_DK].reshape(b, t, DIFF_HEADS, 2, DIFF_HD)
    dk = u[..., OFF_DK:OFF_DV].reshape(b, t, DIFF_HEADS, 2, DIFF_HD)
    dv = u[..., OFF_DV:IN_COLS].reshape(b, t, DIFF_HEADS, 2 * DIFF_HD)

    k_pos = jnp.arange(past_len + t, dtype=jnp.int32)
    q_pos = past_len + jnp.arange(t, dtype=jnp.int32)

    kf_all = jnp.concatenate([pk_f, fk], axis=1)
    vf_all = jnp.concatenate([pv_f, fv], axis=1)
    c_all = jnp.cumsum(jnp.concatenate([plogf.astype(jnp.float32), logf], axis=1), axis=1)
    o_f = _fox_attention(fq, kf_all, vf_all, c_all[:, past_len:], c_all, q_pos, k_pos)

    dl = diff_lambda.astype(jnp.float32)
    lam = jnp.exp(jnp.sum(dl[0] * dl[1])) - jnp.exp(jnp.sum(dl[2] * dl[3])) + lambda_init
    kd_all = jnp.concatenate([pk_d, dk], axis=1)
    vd_all = jnp.concatenate([pv_d, dv], axis=1)
    o_d = _diff_attention(dq, kd_all, vd_all, q_pos, k_pos, rel_table, lam)
    o_d = _rmsnorm(o_d, diff_subln_g, SUBLN_EPS) * (1.0 - lambda_init)

    mixed = jnp.concatenate([o_f.reshape(b, t, FOX_W), o_d.reshape(b, t, DIFF_W)], axis=-1)
    x = x + mixed @ w_o
    x = x + _moe(_rmsnorm(x, g_ffn), w_r, b_r, w_gate, b_gate, w_up, b_up, w_down, b_down)
    x = x + (p_i @ w_ple_proj) * jax.nn.sigmoid(_rmsnorm(x, g_ple) @ w_ple_gate)
    return x, (fk, fv, logf.astype(x.dtype), dk, dv)


def setup_inputs(seed: int = 0) -> dict:
    key = jax.random.key(seed)
    ks = iter(jax.random.split(key, 40))
    f32 = jnp.float32

    def nrm(shape, scale=1.0):
        return jax.random.normal(next(ks), shape, f32) * scale

    inp = {}
    inp["x_prompt"] = nrm((BATCH, SEQ, D_MODEL))
    inp["x_sample"] = nrm((DEC_BATCH, DEC_SEQ, D_MODEL))
    inp["p_prompt"] = nrm((DEPTH, BATCH, SEQ, PLE_DIM))
    inp["p_sample"] = nrm((DEPTH, DEC_BATCH, DEC_SEQ, PLE_DIM))
    inp["cache_fox_k"] = nrm((DEPTH, DEC_BATCH, PAST_LEN, FOX_HEADS, FOX_HD))
    inp["cache_fox_v"] = nrm((DEPTH, DEC_BATCH, PAST_LEN, FOX_HEADS, FOX_HD))
    inp["cache_fox_logf"] = jax.nn.log_sigmoid(FORGET_BIAS_INIT + nrm((DEPTH, DEC_BATCH, PAST_LEN, FOX_HEADS)))
    inp["cache_diff_k"] = nrm((DEPTH, DEC_BATCH, PAST_LEN, DIFF_HEADS, 2, DIFF_HD))
    inp["cache_diff_v"] = nrm((DEPTH, DEC_BATCH, PAST_LEN, DIFF_HEADS, 2 * DIFF_HD))
    inp["g_mix"] = 1.0 + nrm((DEPTH, D_MODEL), 0.02)
    inp["w_in"] = nrm((DEPTH, D_MODEL, IN_COLS), D_MODEL ** -0.5)
    inp["b_forget"] = FORGET_BIAS_INIT + nrm((DEPTH, FOX_HEADS), 0.1)
    inp["diff_lambda"] = nrm((DEPTH, 4, DIFF_HD), 0.1)
    inp["diff_subln_g"] = 1.0 + nrm((DEPTH, 2 * DIFF_HD), 0.02)
    inp["rel_bias_table"] = nrm((NUM_BUCKETS, DIFF_HEADS), 0.1)
    inp["w_o"] = nrm((DEPTH, MIX_WIDTH, D_MODEL), MIX_WIDTH ** -0.5)
    inp["g_ffn"] = 1.0 + nrm((DEPTH, D_MODEL), 0.02)
    inp["w_router"] = nrm((DEPTH, D_MODEL, N_EXPERTS), D_MODEL ** -0.5)
    inp["b_router"] = nrm((DEPTH, N_EXPERTS), 0.01)
    inp["w_gate"] = nrm((DEPTH, N_EXPERTS, D_MODEL, D_FF), D_MODEL ** -0.5)
    inp["b_gate"] = nrm((DEPTH, N_EXPERTS, D_FF), 0.01)
    inp["w_up"] = nrm((DEPTH, N_EXPERTS, D_MODEL, D_FF), D_MODEL ** -0.5)
    inp["b_up"] = nrm((DEPTH, N_EXPERTS, D_FF), 0.01)
    inp["w_down"] = nrm((DEPTH, N_EXPERTS, D_FF, D_MODEL), D_FF ** -0.5)
    inp["b_down"] = nrm((DEPTH, N_EXPERTS, D_MODEL), 0.01)
    inp["g_ple"] = 1.0 + nrm((DEPTH, D_MODEL), 0.02)
    inp["w_ple_gate"] = nrm((DEPTH, D_MODEL, D_MODEL), D_MODEL ** -0.5)
    inp["w_ple_proj"] = nrm((DEPTH, PLE_DIM, D_MODEL), PLE_DIM ** -0.5)
    inp["g_final"] = 1.0 + nrm((D_MODEL,), 0.02)
    return inp


def reference(x_prompt, x_sample, p_prompt, p_sample,
              cache_fox_k, cache_fox_v, cache_fox_logf, cache_diff_k, cache_diff_v,
              g_mix, w_in, b_forget, diff_lambda, diff_subln_g, rel_bias_table, w_o,
              g_ffn, w_router, b_router, w_gate, b_gate, w_up, b_up, w_down, b_down,
              g_ple, w_ple_gate, w_ple_proj, g_final):
    bp = x_prompt.shape[0]
    dt = x_prompt.dtype
    empty_past = (jnp.zeros((bp, 0, FOX_HEADS, FOX_HD), dt),
                  jnp.zeros((bp, 0, FOX_HEADS, FOX_HD), dt),
                  jnp.zeros((bp, 0, FOX_HEADS), dt),
                  jnp.zeros((bp, 0, DIFF_HEADS, 2, DIFF_HD), dt),
                  jnp.zeros((bp, 0, DIFF_HEADS, 2 * DIFF_HD), dt))
    xp, xs = x_prompt, x_sample
    rows_p = [[] for _ in range(5)]
    rows_s = [[] for _ in range(5)]
    for i in range(DEPTH):
        lambda_init = 0.8 - 0.6 * math.exp(-0.3 * i)
        lw = (g_mix[i], w_in[i], b_forget[i], diff_lambda[i], diff_subln_g[i], rel_bias_table,
              w_o[i], g_ffn[i], w_router[i], b_router[i], w_gate[i], b_gate[i], w_up[i], b_up[i],
              w_down[i], b_down[i], g_ple[i], w_ple_gate[i], w_ple_proj[i])
        xp, new_p = _layer(xp, p_prompt[i], empty_past, lw, lambda_init)
        past_s = (cache_fox_k[i], cache_fox_v[i], cache_fox_logf[i], cache_diff_k[i], cache_diff_v[i])
        xs, new_s = _layer(xs, p_sample[i], past_s, lw, lambda_init)
        for j in range(5):
            rows_p[j].append(new_p[j])
            rows_s[j].append(new_s[j])
    y_prompt = _rmsnorm(xp, g_final)
    y_sample = _rmsnorm(xs, g_final)
    return (y_prompt, y_sample,
            jnp.stack(rows_p[0]), jnp.stack(rows_p[1]), jnp.stack(rows_p[2]),
            jnp.stack(rows_p[3]), jnp.stack(rows_p[4]),
            jnp.stack(rows_s[0]), jnp.stack(rows_s[1]), jnp.stack(rows_s[2]),
            jnp.stack(rows_s[3]), jnp.stack(rows_s[4]))
```

```python
import functools
import math

import jax
import jax.numpy as jnp
import numpy as np
from jax import lax
from jax.experimental import pallas as pl
from jax.experimental.pallas import tpu as pltpu

HEAD_DIM = 64
CHUNK = 64
NUM_BUCKETS = 32
MAX_DISTANCE = 128
TOP_K = 4
SWIGLU_LIMIT = 7.0
SWIGLU_ALPHA = 1.702
NORM_EPS = 1e-6
SUBLN_EPS = 1e-5
NEG_INF = -1e30

LANES = 128
VMEM_LIMIT_BYTES = 56 * 1024 * 1024

F32 = jnp.float32
BF16 = jnp.bfloat16
I32 = jnp.int32
U32 = jnp.uint32


def _cparams(n_axes):
    return pltpu.CompilerParams(
        dimension_semantics=("arbitrary",) * n_axes, vmem_limit_bytes=VMEM_LIMIT_BYTES)


def _rms_scale(x, eps):
    return lax.rsqrt(jnp.mean(x * x, axis=-1, keepdims=True) + eps)


def _pack_bf16_pairs(lo, hi):
    lo_bits = pltpu.bitcast(lo.astype(BF16).astype(F32), U32)
    hi_bits = pltpu.bitcast(hi.astype(BF16).astype(F32), U32)
    return lo_bits | (hi_bits >> 16)


def _unpack_bf16_pairs(packed):
    lo = pltpu.bitcast(packed & jnp.uint32(0xFFFF0000), F32)
    hi = pltpu.bitcast(packed << 16, F32)
    return lo, hi


def _inproj_kernel(x_ref, g_ref, w_ref, wf_ref, bf_ref,
                   qf_ref, qd_ref, kf_ref, vf_ref, kd_ref, vd_ref,
                   kf16_ref, vf16_ref, kd16_ref, vd16_ref, logf_ref, *, width, n_fox):
    x = x_ref[...]
    h = (x * _rms_scale(x, NORM_EPS) * g_ref[...]).astype(BF16)
    q_scale = 1.0 / math.sqrt(HEAD_DIM)

    def proj(c):
        return jnp.dot(h, w_ref[:, c * width:(c + 1) * width], preferred_element_type=F32)

    qf_ref[...] = (proj(0) * q_scale).astype(BF16)
    qd_ref[...] = (proj(1) * q_scale).astype(BF16)
    for c, (o32, o16) in enumerate(((kf_ref, kf16_ref), (vf_ref, vf16_ref),
                                    (kd_ref, kd16_ref), (vd_ref, vd16_ref))):
        u = proj(2 + c)
        o32[...] = u
        o16[...] = u.astype(BF16)
    f = jnp.dot(h, wf_ref[...], preferred_element_type=F32) + bf_ref[...]
    logf_ref[...] = jax.nn.log_sigmoid(f)[:, :n_fox]


def _inproj(x2d, g_mix, w_main, w_f, b_f, *, tm, n_fox):
    n, d = x2d.shape
    width = w_main.shape[1] // 6
    row = lambda i: (i, 0)
    const = lambda i: (0, 0)
    out_shape = ([jax.ShapeDtypeStruct((n, width), BF16)] * 2
                 + [jax.ShapeDtypeStruct((n, width), F32)] * 4
                 + [jax.ShapeDtypeStruct((n, width), BF16)] * 4
                 + [jax.ShapeDtypeStruct((n, n_fox), F32)])
    out_specs = [pl.BlockSpec((tm, width), row)] * 10 + [pl.BlockSpec((tm, n_fox), row)]
    return pl.pallas_call(
        functools.partial(_inproj_kernel, width=width, n_fox=n_fox),
        out_shape=out_shape,
        grid=(n // tm,),
        in_specs=[pl.BlockSpec((tm, d), row), pl.BlockSpec((1, d), const),
                  pl.BlockSpec(w_main.shape, const), pl.BlockSpec(w_f.shape, const),
                  pl.BlockSpec(b_f.shape, const)],
        out_specs=out_specs,
        compiler_params=_cparams(1),
        name="inproj",
    )(x2d, g_mix, w_main, w_f, b_f)


def _cumsum_kernel(x_ref, o_ref):
    x = x_ref[...]
    length = x.shape[1]
    lane = lax.broadcasted_iota(I32, x.shape, 1)
    shift = 1
    while shift < length:
        x = x + jnp.where(lane >= shift, pltpu.roll(x, shift, axis=1), 0.0)
        shift *= 2
    o_ref[...] = x


def _cumsum_lanes(x):
    b, h, length = x.shape
    spec = pl.BlockSpec((None, h, length), lambda i: (i, 0, 0))
    return pl.pallas_call(
        _cumsum_kernel, out_shape=jax.ShapeDtypeStruct(x.shape, F32), grid=(b,),
        in_specs=[spec], out_specs=spec, compiler_params=_cparams(1), name="cumsum",
    )(x)


def _nt_dot(a, b):
    return lax.dot_general(a, b, (((1,), (1,)), ((), ())), preferred_element_type=F32)


def _fox_kernel(*refs, tq, tkp, n_past):
    if n_past:
        q_ref, kn_ref, vn_ref, cn_ref, kp_ref, vp_ref, cp_ref, o_ref, acc_a, acc_b, m_a, m_b = refs
    else:
        q_ref, kn_ref, vn_ref, cn_ref, o_ref, acc_a, acc_b, m_a, m_b = refs
        kp_ref = vp_ref = cp_ref = None
    i = pl.program_id(2)
    lo = lax.broadcasted_iota(I32, (1, LANES), 1) < HEAD_DIM

    q = q_ref[...]
    zero = jnp.zeros_like(q)
    q_heads = (jnp.where(lo, q, zero), jnp.where(lo, zero, q))
    for acc, m in ((acc_a, m_a), (acc_b, m_b)):
        acc[...] = jnp.zeros_like(acc)
        m[...] = jnp.full_like(m, NEG_INF)

    def chunk(k, v, c_rows, mask):
        k = k.astype(BF16)
        v = v.astype(BF16)
        one = jnp.ones_like(v)
        v_heads = (jnp.where(lo, v, one), jnp.where(lo, one, v))
        for hd, (acc, m) in enumerate(((acc_a, m_a), (acc_b, m_b))):
            s = _nt_dot(q_heads[hd], k) - c_rows[hd:hd + 1, :]
            if mask is not None:
                s = jnp.where(mask, s, NEG_INF)
            m_old = m[...]
            m_new = jnp.maximum(m_old, jnp.max(s, axis=1, keepdims=True))
            p = jnp.exp(s - m_new).astype(BF16)
            acc[...] = jnp.exp(m_old - m_new) * acc[...] + jnp.dot(
                p, v_heads[hd], preferred_element_type=F32)
            m[...] = m_new

    if n_past:
        def past_body(j, carry):
            off = pl.multiple_of(j * tkp, tkp)
            chunk(kp_ref[pl.ds(off, tkp), :], vp_ref[pl.ds(off, tkp), :],
                  cp_ref[:, pl.ds(off, tkp)], None)
            return carry
        lax.fori_loop(0, n_past, past_body, 0)
        off = 0
    else:
        def new_body(j, carry):
            off = pl.multiple_of(j * tq, tq)
            chunk(kn_ref[pl.ds(off, tq), :], vn_ref[pl.ds(off, tq), :],
                  cn_ref[:, pl.ds(off, tq)], None)
            return carry
        lax.fori_loop(0, i, new_body, 0)
        off = pl.multiple_of(i * tq, tq)

    causal = (lax.broadcasted_iota(I32, (tq, tq), 1) <= lax.broadcasted_iota(I32, (tq, tq), 0))
    chunk(kn_ref[pl.ds(off, tq), :], vn_ref[pl.ds(off, tq), :],
          cn_ref[:, pl.ds(off, tq)], causal)

    a = acc_a[...]
    b = acc_b[...]
    out = jnp.where(lo, a / pltpu.roll(a, HEAD_DIM, axis=1), b / pltpu.roll(b, HEAD_DIM, axis=1))
    o_ref[...] = out.astype(o_ref.dtype)


def _fox_attention(q, k_new, v_new, c_new, past, *, tq, tkp):
    b, t, w = q.shape
    n_pairs = w // LANES
    nq = t // tq
    pair_rows = lambda c: c.reshape(b, n_pairs, 2, c.shape[-1])
    qspec = pl.BlockSpec((None, tq, LANES), lambda bi, p, i: (bi, i, p))
    kvspec = pl.BlockSpec((None, t, LANES), lambda bi, p, i: (bi, 0, p))
    cspec = pl.BlockSpec((None, None, 2, t), lambda bi, p, i: (bi, p, 0, 0))
    in_specs = [qspec, kvspec, kvspec, cspec]
    args = [q, k_new, v_new, pair_rows(c_new)]
    n_past = 0
    if past is not None:
        assert nq == 1, "a cached prefix is supported for a single query tile only"
        kp, vp, cp = past
        plen = kp.shape[1]
        n_past = plen // tkp
        assert n_past * tkp == plen
        pspec = pl.BlockSpec((None, plen, LANES), lambda bi, p, i: (bi, 0, p))
        in_specs += [pspec, pspec,
                     pl.BlockSpec((None, None, 2, plen), lambda bi, p, i: (bi, p, 0, 0))]
        args += [kp, vp, pair_rows(cp)]
    return pl.pallas_call(
        functools.partial(_fox_kernel, tq=tq, tkp=tkp, n_past=n_past),
        out_shape=jax.ShapeDtypeStruct((b, t, w), BF16),
        grid=(b, n_pairs, nq),
        in_specs=in_specs,
        out_specs=qspec,
        scratch_shapes=[pltpu.VMEM((tq, LANES), F32), pltpu.VMEM((tq, LANES), F32),
                        pltpu.VMEM((tq, 1), F32), pltpu.VMEM((tq, 1), F32)],
        compiler_params=_cparams(3),
        name="fox_attn",
    )(*args)


def _diff_kernel(*refs, tq, tkp, n_past, lambda_init):
    if n_past:
        (q_ref, kn_ref, vn_ref, bprev_ref, bdiag_ref, lam_ref, g_ref, kp_ref, vp_ref,
         o_ref, acc0, acc1, m0, m1, l0, l1) = refs
    else:
        (q_ref, kn_ref, vn_ref, bprev_ref, bdiag_ref, lam_ref, g_ref,
         o_ref, acc0, acc1, m0, m1, l0, l1) = refs
        kp_ref = vp_ref = None
    i = pl.program_id(2)
    lo = lax.broadcasted_iota(I32, (1, LANES), 1) < HEAD_DIM

    q = q_ref[...]
    zero = jnp.zeros_like(q)
    q_maps = (jnp.where(lo, q, zero), jnp.where(lo, zero, q))
    state = ((acc0, m0, l0), (acc1, m1, l1))
    for acc, m, l in state:
        acc[...] = jnp.zeros_like(acc)
        m[...] = jnp.full_like(m, NEG_INF)
        l[...] = jnp.zeros_like(l)

    def chunk(k, v, bias, mask):
        k = k.astype(BF16)
        v = v.astype(BF16)
        for mp, (acc, m, l) in enumerate(state):
            s = _nt_dot(q_maps[mp], k)
            if bias is not None:
                s = s + bias
            if mask is not None:
                s = jnp.where(mask, s, NEG_INF)
            m_old = m[...]
            m_new = jnp.maximum(m_old, jnp.max(s, axis=1, keepdims=True))
            p = jnp.exp(s - m_new)
            alpha = jnp.exp(m_old - m_new)
            l[...] = alpha * l[...] + jnp.sum(p, axis=1, keepdims=True)
            acc[...] = alpha * acc[...] + jnp.dot(p.astype(BF16), v, preferred_element_type=F32)
            m[...] = m_new

    if n_past:
        def past_body(j, carry):
            off = pl.multiple_of(j * tkp, tkp)
            chunk(kp_ref[pl.ds(off, tkp), :], vp_ref[pl.ds(off, tkp), :], None, None)
            return carry
        lax.fori_loop(0, n_past - 1, past_body, 0)
        off = (n_past - 1) * tkp
        chunk(kp_ref[pl.ds(off, tkp), :], vp_ref[pl.ds(off, tkp), :], bprev_ref[...], None)
        off = 0
    else:
        def new_body(j, carry):
            off = pl.multiple_of(j * tq, tq)
            chunk(kn_ref[pl.ds(off, tq), :], vn_ref[pl.ds(off, tq), :], None, None)
            return carry
        lax.fori_loop(0, jnp.maximum(i - 1, 0), new_body, 0)

        @pl.when(i >= 1)
        def _():
            off = pl.multiple_of((i - 1) * tq, tq)
            chunk(kn_ref[pl.ds(off, tq), :], vn_ref[pl.ds(off, tq), :], bprev_ref[...], None)

        off = pl.multiple_of(i * tq, tq)

    row_chunk = lax.broadcasted_iota(I32, (tq, tq), 0) // CHUNK
    col_chunk = lax.broadcasted_iota(I32, (tq, tq), 1) // CHUNK
    chunk(kn_ref[pl.ds(off, tq), :], vn_ref[pl.ds(off, tq), :], bdiag_ref[...],
          col_chunk <= row_chunk)

    dl = lam_ref[...]
    lam = (jnp.exp(jnp.sum(dl[0:1] * dl[1:2], axis=1, keepdims=True))
           - jnp.exp(jnp.sum(dl[2:3] * dl[3:4], axis=1, keepdims=True)) + lambda_init)
    o = acc0[...] / l0[...] - lam * (acc1[...] / l1[...])
    o = o * _rms_scale(o, SUBLN_EPS) * g_ref[...] * (1.0 - lambda_init)
    o_ref[...] = o.astype(o_ref.dtype)


def _diff_attention(q, k_new, v_new, bias_prev, bias_diag, diff_lambda, subln_g, past, *,
                    tq, tkp, lambda_init):
    b, t, w = q.shape
    n_heads = w // LANES
    nq = t // tq
    qspec = pl.BlockSpec((None, tq, LANES), lambda bi, h, i: (bi, i, h))
    kvspec = pl.BlockSpec((None, t, LANES), lambda bi, h, i: (bi, 0, h))
    head_tile = lambda arr: pl.BlockSpec((None,) + arr.shape[1:], lambda bi, h, i: (h, 0, 0))
    const2 = lambda arr: pl.BlockSpec(arr.shape, lambda bi, h, i: (0, 0))
    in_specs = [qspec, kvspec, kvspec, head_tile(bias_prev), head_tile(bias_diag),
                const2(diff_lambda), const2(subln_g)]
    args = [q, k_new, v_new, bias_prev, bias_diag, diff_lambda, subln_g]
    n_past = 0
    if past is not None:
        assert nq == 1, "a cached prefix is supported for a single query tile only"
        kp, vp = past
        plen = kp.shape[1]
        n_past = plen // tkp
        assert n_past * tkp == plen and tkp >= MAX_DISTANCE
        pspec = pl.BlockSpec((None, plen, LANES), lambda bi, h, i: (bi, 0, h))
        in_specs += [pspec, pspec]
        args += [kp, vp]
    else:
        assert tq >= MAX_DISTANCE
    return pl.pallas_call(
        functools.partial(_diff_kernel, tq=tq, tkp=tkp, n_past=n_past, lambda_init=lambda_init),
        out_shape=jax.ShapeDtypeStruct((b, t, w), BF16),
        grid=(b, n_heads, nq),
        in_specs=in_specs,
        out_specs=qspec,
        scratch_shapes=[pltpu.VMEM((tq, LANES), F32), pltpu.VMEM((tq, LANES), F32)]
        + [pltpu.VMEM((tq, 1), F32)] * 4,
        compiler_params=_cparams(3),
        name="diff_attn",
    )(*args)


def _t5_bucket(rel):
    nb = NUM_BUCKETS // 2
    max_exact = nb // 2
    ret = jnp.where(rel > 0, nb, 0)
    n = jnp.abs(rel)
    nf = jnp.maximum(n, max_exact).astype(F32)
    large = max_exact + (jnp.log(nf / max_exact) / math.log(MAX_DISTANCE / max_exact)
                         * (nb - max_exact)).astype(I32)
    large = jnp.minimum(large, nb - 1)
    return ret + jnp.where(n < max_exact, n, large)


def _bias_tiles(rel_table, tq, tk_prev):
    r = jnp.arange(tq, dtype=I32)[:, None]
    far = rel_table[_t5_bucket(jnp.full((), -4 * MAX_DISTANCE, I32))]

    def tile(rel):
        return jnp.transpose(rel_table[_t5_bucket(rel)] - far, (2, 0, 1)).astype(F32)

    prev = tile(jnp.arange(tk_prev, dtype=I32)[None, :] - tk_prev - r)
    diag = tile(jnp.arange(tq, dtype=I32)[None, :] - r)
    return prev, diag


def _router_kernel(x_ref, of_ref, od_ref, wof_ref, wod_ref, g_ref, wr_ref, br_ref,
                   x1_ref, hn_ref, te_ref, tw_ref, rank_ref, cnt_ref, *, tm):
    step = pl.program_id(0)

    @pl.when(step == 0)
    def _():
        cnt_ref[...] = jnp.zeros_like(cnt_ref)

    x1 = (x_ref[...] + jnp.dot(of_ref[...], wof_ref[...], preferred_element_type=F32)
          + jnp.dot(od_ref[...], wod_ref[...], preferred_element_type=F32))
    x1_ref[...] = x1
    hn = x1 * _rms_scale(x1, NORM_EPS) * g_ref[...]
    half = hn.shape[1] // 2
    hn_ref[...] = _pack_bf16_pairs(hn[:, :half], hn[:, half:])

    logits = jnp.dot(hn, wr_ref[...], preferred_element_type=F32,
                     precision=lax.Precision.HIGHEST) + br_ref[...]
    lane = lax.broadcasted_iota(I32, logits.shape, 1)
    lane_f = lane.astype(F32)
    sel = jnp.zeros(logits.shape, F32)
    vals, idxs = [], []
    for _ in range(TOP_K):
        mx = jnp.max(logits, axis=1, keepdims=True)
        idx = jnp.min(jnp.where(logits == mx, lane_f, float(LANES)), axis=1, keepdims=True)
        hit = lane_f == idx
        sel = jnp.where(hit, 1.0, sel)
        logits = jnp.where(hit, -jnp.inf, logits)
        vals.append(mx)
        idxs.append(idx)
    exps = [jnp.exp(v - vals[0]) for v in vals]
    denom = exps[0] + exps[1] + exps[2] + exps[3]

    earlier = (lax.broadcasted_iota(I32, (tm, tm), 1)
               < lax.broadcasted_iota(I32, (tm, tm), 0)).astype(BF16)
    before = cnt_ref[...] + jnp.dot(earlier, sel.astype(BF16), preferred_element_type=F32)
    for k in range(TOP_K):
        te_ref[:, k:k + 1] = idxs[k].astype(I32)
        tw_ref[:, k:k + 1] = exps[k] / denom
        rank_ref[:, k:k + 1] = jnp.sum(
            jnp.where(lane_f == idxs[k], before, 0.0), axis=1, keepdims=True).astype(I32)
    cnt_ref[...] += jnp.sum(sel, axis=0, keepdims=True)


def _router(x2d, o_fox, o_diff, wo_f, wo_d, g_ffn, w_r, b_r, *, tm):
    n, d = x2d.shape
    row = lambda i: (i, 0)
    const = lambda i: (0, 0)
    full = lambda arr: pl.BlockSpec(arr.shape, const)
    return pl.pallas_call(
        functools.partial(_router_kernel, tm=tm),
        out_shape=[jax.ShapeDtypeStruct((n, d), F32), jax.ShapeDtypeStruct((n, d // 2), U32),
                   jax.ShapeDtypeStruct((n, TOP_K), I32), jax.ShapeDtypeStruct((n, TOP_K), F32),
                   jax.ShapeDtypeStruct((n, TOP_K), I32), jax.ShapeDtypeStruct((1, LANES), F32)],
        grid=(n // tm,),
        in_specs=[pl.BlockSpec((tm, d), row), pl.BlockSpec((tm, o_fox.shape[1]), row),
                  pl.BlockSpec((tm, o_diff.shape[1]), row), full(wo_f), full(wo_d),
                  full(g_ffn), full(w_r), full(b_r)],
        out_specs=[pl.BlockSpec((tm, d), row), pl.BlockSpec((tm, d // 2), row),
                   pl.BlockSpec((tm, TOP_K), row), pl.BlockSpec((tm, TOP_K), row),
                   pl.BlockSpec((tm, TOP_K), row), pl.BlockSpec((1, LANES), const)],
        compiler_params=_cparams(1),
        name="router",
    )(x2d, o_fox, o_diff, wo_f, wo_d, g_ffn, w_r, b_r)


def _row_copy(src_hbm, src_row, dst, dst_row, sem):
    return pltpu.make_async_copy(src_hbm.at[pl.ds(src_row, 1), :], dst.at[pl.ds(dst_row, 1), :], sem)


def _dispatch_kernel(tail_start_ref, tail_valid_ref, pos_ref, hn_hbm, xs_hbm, zeros_ref, sem,
                     *, tm, tme, n_experts):
    step = pl.program_id(0)

    def tail_copy(e):
        return pltpu.make_async_copy(
            zeros_ref, xs_hbm.at[pl.ds(pl.multiple_of(tail_start_ref[e], tme), tme), :], sem.at[0])

    @pl.when(step == 0)
    def _():
        zeros_ref[...] = jnp.zeros_like(zeros_ref)
        for e in range(n_experts):
            @pl.when(tail_valid_ref[e] > 0)
            def _():
                tail_copy(e).start()
        for e in range(n_experts):
            @pl.when(tail_valid_ref[e] > 0)
            def _():
                tail_copy(e).wait()

    base = step * tm

    def start_row(r, carry):
        for k in range(TOP_K):
            _row_copy(hn_hbm, base + r, xs_hbm, pos_ref[0, r * TOP_K + k], sem.at[1]).start()
        return carry

    def wait_row(r, carry):
        for k in range(TOP_K):
            _row_copy(hn_hbm, base + r, xs_hbm, pos_ref[0, r * TOP_K + k], sem.at[1]).wait()
        return carry

    lax.fori_loop(0, tm, start_row, 0, unroll=8)
    lax.fori_loop(0, tm, wait_row, 0, unroll=8)


def _dispatch(hn_packed, pos, tail_start, tail_valid, *, m_pad, tm, tme):
    n, half = hn_packed.shape
    n_experts = tail_start.shape[0]
    pos3 = pos.reshape(n // tm, 1, tm * TOP_K)
    grid_spec = pltpu.PrefetchScalarGridSpec(
        num_scalar_prefetch=2,
        grid=(n // tm,),
        in_specs=[pl.BlockSpec((None, 1, tm * TOP_K), lambda i, *_: (i, 0, 0),
                               memory_space=pltpu.SMEM),
                  pl.BlockSpec(memory_space=pl.ANY)],
        out_specs=pl.BlockSpec(memory_space=pl.ANY),
        scratch_shapes=[pltpu.VMEM((tme, half), U32), pltpu.SemaphoreType.DMA((2,))],
    )
    return pl.pallas_call(
        functools.partial(_dispatch_kernel, tm=tm, tme=tme, n_experts=n_experts),
        out_shape=jax.ShapeDtypeStruct((m_pad, half), U32),
        grid_spec=grid_spec,
        compiler_params=pltpu.CompilerParams(
            dimension_semantics=("arbitrary",), vmem_limit_bytes=VMEM_LIMIT_BYTES,
            has_side_effects=True),
        name="dispatch",
    )(tail_start, tail_valid, pos3, hn_packed)


def _experts_kernel(te_ref, nact_ref, xs_ref, wg_ref, bg_ref, wu_ref, bu_ref, wd_ref, bd_ref,
                    ys_ref):
    @pl.when(pl.program_id(0) < nact_ref[0])
    def _():
        half = xs_ref.shape[1]
        x_lo, x_hi = _unpack_bf16_pairs(xs_ref[...])
        x_lo = x_lo.astype(BF16)
        x_hi = x_hi.astype(BF16)

        def proj(w_ref, b_ref):
            return (jnp.dot(x_lo, w_ref[:half, :], preferred_element_type=F32)
                    + jnp.dot(x_hi, w_ref[half:, :], preferred_element_type=F32) + b_ref[...])

        g = jnp.minimum(proj(wg_ref, bg_ref), SWIGLU_LIMIT)
        u = jnp.clip(proj(wu_ref, bu_ref), -SWIGLU_LIMIT, SWIGLU_LIMIT)
        a = ((u + 1.0) * (g * jax.nn.sigmoid(SWIGLU_ALPHA * g))).astype(BF16)
        y = jnp.dot(a, wd_ref[...], preferred_element_type=F32) + bd_ref[...]
        ys_ref[...] = _pack_bf16_pairs(y[:, :half], y[:, half:])


def _experts(xs, tile_expert, n_active, w_gate, b_gate, w_up, b_up, w_down, b_down, *, tme):
    m_pad, half = xs.shape
    n_tiles = m_pad // tme
    d, d_ff = w_gate.shape[1:]
    row = lambda t, te, na: (jnp.minimum(t, na[0] - 1), 0)
    wspec = lambda shape: pl.BlockSpec((None,) + shape, lambda t, te, na: (te[t], 0, 0))
    grid_spec = pltpu.PrefetchScalarGridSpec(
        num_scalar_prefetch=2,
        grid=(n_tiles,),
        in_specs=[pl.BlockSpec((tme, half), row),
                  wspec((d, d_ff)), wspec((1, d_ff)), wspec((d, d_ff)), wspec((1, d_ff)),
                  wspec((d_ff, d)), wspec((1, d))],
        out_specs=pl.BlockSpec((tme, half), row),
    )
    return pl.pallas_call(
        _experts_kernel,
        out_shape=jax.ShapeDtypeStruct((m_pad, half), U32),
        grid_spec=grid_spec,
        compiler_params=_cparams(1),
        name="experts",
    )(tile_expert, n_active, xs, w_gate, b_gate, w_up, b_up, w_down, b_down)


def _combine_kernel(pos_ref, x1_ref, tw_ref, p_ref, wg_ref, wp_ref, gple_ref, gfin_ref, ys_hbm,
                    y_ref, gbuf, sem, *, tm):
    def start_row(r, carry):
        for k in range(TOP_K):
            _row_copy(ys_hbm, pos_ref[0, r * TOP_K + k], gbuf.at[k], r, sem).start()
        return carry

    def wait_row(r, carry):
        for k in range(TOP_K):
            _row_copy(ys_hbm, pos_ref[0, r * TOP_K + k], gbuf.at[k], r, sem).wait()
        return carry

    lax.fori_loop(0, tm, start_row, 0, unroll=8)
    lax.fori_loop(0, tm, wait_row, 0, unroll=8)

    x1 = x1_ref[...]
    half = x1.shape[1] // 2
    tw = tw_ref[...]
    moe_lo = jnp.zeros((tm, half), F32)
    moe_hi = jnp.zeros((tm, half), F32)
    for k in range(TOP_K):
        lo, hi = _unpack_bf16_pairs(gbuf[k])
        moe_lo = moe_lo + tw[:, k:k + 1] * lo
        moe_hi = moe_hi + tw[:, k:k + 1] * hi
    x2 = x1 + jnp.concatenate([moe_lo, moe_hi], axis=1)
    hp = (x2 * _rms_scale(x2, NORM_EPS) * gple_ref[...]).astype(BF16)
    gate = jax.nn.sigmoid(jnp.dot(hp, wg_ref[...], preferred_element_type=F32))
    proj = jnp.dot(p_ref[...].astype(BF16), wp_ref[...], preferred_element_type=F32)
    x3 = x2 + proj * gate
    y_ref[...] = x3 * _rms_scale(x3, NORM_EPS) * gfin_ref[...]


def _combine(x1, top_w, pos, ys, p2d, w_ple_gate, w_ple_proj, g_ple, g_final, *, tm):
    n, d = x1.shape
    half = ys.shape[1]
    pos3 = pos.reshape(n // tm, 1, tm * TOP_K)
    row = lambda i: (i, 0)
    const = lambda i: (0, 0)
    full = lambda arr: pl.BlockSpec(arr.shape, const)
    return pl.pallas_call(
        functools.partial(_combine_kernel, tm=tm),
        out_shape=jax.ShapeDtypeStruct((n, d), F32),
        grid=(n // tm,),
        in_specs=[pl.BlockSpec((None, 1, tm * TOP_K), lambda i: (i, 0, 0), memory_space=pltpu.SMEM),
                  pl.BlockSpec((tm, d), row), pl.BlockSpec((tm, TOP_K), row),
                  pl.BlockSpec((tm, p2d.shape[1]), row), full(w_ple_gate), full(w_ple_proj),
                  full(g_ple), full(g_final), pl.BlockSpec(memory_space=pl.ANY)],
        out_specs=pl.BlockSpec((tm, d), row),
        scratch_shapes=[pltpu.VMEM((TOP_K, tm, half), U32), pltpu.SemaphoreType.DMA(())],
        compiler_params=_cparams(1),
        name="combine",
    )(pos3, x1, top_w, p2d, w_ple_gate, w_ple_proj, g_ple, g_final, ys)


def _pick_tile(n, want):
    t = min(n, want)
    assert n % t == 0, (n, t)
    return t


def _layer_group(x, p, past, wts, lambda_init, g_final):
    (g_mix, w_main, w_f, b_f, diff_lambda, subln_g, rel_table, wo_f, wo_d, g_ffn, w_r, b_r,
     w_gate, b_gate, w_up, b_up, w_down, b_down, g_ple, w_ple_gate, w_ple_proj) = wts
    b, t, d = x.shape
    n = b * t
    n_fox = (w_main.shape[1] // 6) // HEAD_DIM
    n_experts = w_gate.shape[0]
    x2d = x.reshape(n, d)

    tm = _pick_tile(n, 512)
    (qf, qd, kf, vf, kd, vd, kf16, vf16, kd16, vd16, logf) = _inproj(
        x2d, g_mix, w_main, w_f, b_f, tm=tm, n_fox=n_fox)
    width = kf.shape[1]
    as_seq = lambda a: a.reshape(b, t, a.shape[-1])

    logf_t = jnp.transpose(as_seq(logf), (0, 2, 1))
    tq = _pick_tile(t, 512)
    tkp = 512
    if past is None:
        c_new = _cumsum_lanes(logf_t)
        fox_past = diff_past = None
    else:
        pk_f, pv_f, plogf, pk_d, pv_d = past
        plen = pk_f.shape[1]
        total = plen + t
        padded = -(-total // LANES) * LANES
        seq = jnp.concatenate([jnp.transpose(plogf.astype(F32), (0, 2, 1)), logf_t], axis=2)
        c_all = _cumsum_lanes(jnp.pad(seq, ((0, 0), (0, 0), (0, padded - total))))
        c_new = c_all[:, :, plen:total]
        fox_past = (pk_f.reshape(b, plen, width), pv_f.reshape(b, plen, width), c_all[:, :, :plen])
        diff_past = (pk_d.reshape(b, plen, width), pv_d.reshape(b, plen, width))

    o_fox = _fox_attention(as_seq(qf), as_seq(kf16), as_seq(vf16), c_new, fox_past, tq=tq, tkp=tkp)
    bias_prev, bias_diag = _bias_tiles(rel_table, tq, tq if past is None else tkp)
    o_diff = _diff_attention(as_seq(qd), as_seq(kd16), as_seq(vd16), bias_prev, bias_diag,
                             diff_lambda, subln_g, diff_past, tq=tq, tkp=tkp,
                             lambda_init=lambda_init)

    tmr = _pick_tile(n, 256)
    x1, hn_packed, top_e, top_w, rank, counts = _router(
        x2d, o_fox.reshape(n, width), o_diff.reshape(n, width), wo_f, wo_d, g_ffn, w_r, b_r, tm=tmr)

    tme = 512 if n * TOP_K >= 64 * 512 else 256
    n_tiles = -(-(n * TOP_K) // tme) + n_experts
    m_pad = n_tiles * tme
    cnt = counts[0, :n_experts].astype(I32)
    group = -(-cnt // tme) * tme
    ends = jnp.cumsum(group)
    starts = ends - group
    pos = starts[top_e] + rank
    n_active = (ends[-1] // tme).astype(I32)
    tile_ids = jnp.arange(n_tiles, dtype=I32)
    tile_expert = jnp.searchsorted(ends // tme, jnp.minimum(tile_ids, n_active - 1),
                                   side="right").astype(I32)
    tile_expert = jnp.minimum(tile_expert, n_experts - 1)
    tail_start = jnp.maximum(ends - tme, 0).astype(I32)
    tail_valid = (group > 0).astype(I32)

    tmd = _pick_tile(n, 256)
    xs = _dispatch(hn_packed, pos, tail_start, tail_valid, m_pad=m_pad, tm=tmd, tme=tme)
    ys = _experts(xs, tile_expert, n_active.reshape(1), w_gate, b_gate, w_up, b_up, w_down, b_down,
                  tme=tme)
    y = _combine(x1, top_w, pos, ys, p.reshape(n, p.shape[-1]), w_ple_gate, w_ple_proj, g_ple,
                 g_final, tm=tmd)
    return y.reshape(b, t, d), (kf, vf, logf, kd, vd)


def kernel(x_prompt, x_sample, p_prompt, p_sample, cache_fox_k, cache_fox_v, cache_fox_logf, cache_diff_k, cache_diff_v, g_mix, w_in, b_forget, diff_lambda, diff_subln_g, rel_bias_table, w_o, g_ffn, w_router, b_router, w_gate, b_gate, w_up, b_up, w_down, b_down, g_ple, w_ple_gate, w_ple_proj, g_final):
    depth = g_mix.shape[0]
    assert depth == 1, "the final norm is fused into the last layer; one layer supported"
    d = x_prompt.shape[-1]
    n_fox = cache_fox_k.shape[3]
    fox_w = n_fox * cache_fox_k.shape[4]
    n_diff = cache_diff_k.shape[3]
    diff_w = n_diff * 2 * cache_diff_k.shape[5]
    assert fox_w == diff_w and fox_w % LANES == 0
    n_experts = w_router.shape[-1]
    assert TOP_K <= n_experts <= LANES
    row2 = lambda a: a.reshape(1, -1).astype(F32)

    i = 0
    lambda_init = 0.8 - 0.6 * math.exp(-0.3 * i)
    off_fk, off_fv, off_ff = fox_w, 2 * fox_w, 3 * fox_w
    off_dq = off_ff + n_fox
    off_dk, off_dv = off_dq + diff_w, off_dq + 2 * diff_w
    w = w_in[i]
    cols = lambda o, wd: w[:, o:o + wd]
    w_main = jnp.concatenate([cols(0, fox_w), cols(off_dq, diff_w), cols(off_fk, fox_w),
                              cols(off_fv, fox_w), cols(off_dk, diff_w), cols(off_dv, diff_w)],
                             axis=1).astype(BF16)
    w_f = jnp.pad(cols(off_ff, n_fox), ((0, 0), (0, LANES - n_fox))).astype(BF16)
    b_f = jnp.pad(row2(b_forget[i]), ((0, 0), (0, LANES - n_fox)))
    w_r = jnp.pad(w_router[i].astype(F32), ((0, 0), (0, LANES - n_experts)))
    b_r = jnp.pad(row2(b_router[i]), ((0, 0), (0, LANES - n_experts)), constant_values=NEG_INF)
    wo = w_o[i].astype(BF16)
    bias3 = lambda a: a.reshape(n_experts, 1, -1).astype(F32)
    wts = (row2(g_mix[i]), w_main, w_f, b_f, diff_lambda[i].astype(F32), row2(diff_subln_g[i]),
           rel_bias_table.astype(F32), wo[:fox_w], wo[fox_w:], row2(g_ffn[i]), w_r, b_r,
           w_gate[i].astype(BF16), bias3(b_gate[i]), w_up[i].astype(BF16), bias3(b_up[i]),
           w_down[i].astype(BF16), bias3(b_down[i]), row2(g_ple[i]),
           w_ple_gate[i].astype(BF16), w_ple_proj[i].astype(BF16))
    gf = row2(g_final)

    yp, new_p = _layer_group(x_prompt, p_prompt[i], None, wts, lambda_init, gf)
    past_s = (cache_fox_k[i], cache_fox_v[i], cache_fox_logf[i], cache_diff_k[i], cache_diff_v[i])
    ys, new_s = _layer_group(x_sample, p_sample[i], past_s, wts, lambda_init, gf)

    def rows(new, b, t):
        kf, vf, logf, kd, vd = new
        return (kf.reshape(1, b, t, n_fox, -1), vf.reshape(1, b, t, n_fox, -1),
                logf.reshape(1, b, t, n_fox), kd.reshape(1, b, t, n_diff, 2, -1),
                vd.reshape(1, b, t, n_diff, -1))

    bp, tp = x_prompt.shape[:2]
    bs, ts = x_sample.shape[:2]
    return (yp, ys) + rows(new_p, bp, tp) + rows(new_s, bs, ts)
```

```python
import functools
import math

import jax
import jax.numpy as jnp
import numpy as np
from jax import lax
from jax.experimental import pallas as pl
from jax.experimental.pallas import tpu as pltpu

HEAD_DIM = 64
CHUNK = 64
NUM_BUCKETS = 32
MAX_DISTANCE = 128
TOP_K = 4
SWIGLU_LIMIT = 7.0
SWIGLU_ALPHA = 1.702
NORM_EPS = 1e-6
SUBLN_EPS = 1e-5
NEG_INF = -1e30

LANES = 128
VMEM_LIMIT_BYTES = 56 * 1024 * 1024

F32 = jnp.float32
BF16 = jnp.bfloat16
I32 = jnp.int32
U32 = jnp.uint32


def _cparams(n_axes):
    return pltpu.CompilerParams(
        dimension_semantics=("arbitrary",) * n_axes, vmem_limit_bytes=VMEM_LIMIT_BYTES)


def _rms_scale(x, eps):
    return lax.rsqrt(jnp.mean(x * x, axis=-1, keepdims=True) + eps)


def _pack_bf16_pairs(lo, hi):
    lo_bits = pltpu.bitcast(lo.astype(BF16).astype(F32), U32)
    hi_bits = pltpu.bitcast(hi.astype(BF16).astype(F32), U32)
    return lo_bits | (hi_bits >> 16)


def _unpack_bf16_pairs(packed):
    lo = pltpu.bitcast(packed & jnp.uint32(0xFFFF0000), F32)
    hi = pltpu.bitcast(packed << 16, F32)
    return lo, hi


def _inproj_kernel(x_ref, g_ref, w_ref, wf_ref, bf_ref,
                   qf_ref, qd_ref, kf_ref, vf_ref, kd_ref, vd_ref,
                   kf16_ref, vf16_ref, kd16_ref, vd16_ref, logf_ref, *, width, n_fox):
    x = x_ref[...]
    h = (x * _rms_scale(x, NORM_EPS) * g_ref[...]).astype(BF16)
    q_scale = 1.0 / math.sqrt(HEAD_DIM)

    def proj(c):
        return jnp.dot(h, w_ref[:, c * width:(c + 1) * width], preferred_element_type=F32)

    qf_ref[...] = (proj(0) * q_scale).astype(BF16)
    qd_ref[...] = (proj(1) * q_scale).astype(BF16)
    for c, (o32, o16) in enumerate(((kf_ref, kf16_ref), (vf_ref, vf16_ref),
                                    (kd_ref, kd16_ref), (vd_ref, vd16_ref))):
        u = proj(2 + c)
        o32[...] = u
        o16[...] = u.astype(BF16)
    f = jnp.dot(h, wf_ref[...], preferred_element_type=F32) + bf_ref[...]
    logf_ref[...] = jax.nn.log_sigmoid(f)[:, :n_fox]


def _inproj(x2d, g_mix, w_main, w_f, b_f, *, tm, n_fox):
    n, d = x2d.shape
    width = w_main.shape[1] // 6
    row = lambda i: (i, 0)
    const = lambda i: (0, 0)
    out_shape = ([jax.ShapeDtypeStruct((n, width), BF16)] * 2
                 + [jax.ShapeDtypeStruct((n, width), F32)] * 4
                 + [jax.ShapeDtypeStruct((n, width), BF16)] * 4
                 + [jax.ShapeDtypeStruct((n, n_fox), F32)])
    out_specs = [pl.BlockSpec((tm, width), row)] * 10 + [pl.BlockSpec((tm, n_fox), row)]
    return pl.pallas_call(
        functools.partial(_inproj_kernel, width=width, n_fox=n_fox),
        out_shape=out_shape,
        grid=(n // tm,),
        in_specs=[pl.BlockSpec((tm, d), row), pl.BlockSpec((1, d), const),
                  pl.BlockSpec(w_main.shape, const), pl.BlockSpec(w_f.shape, const),
                  pl.BlockSpec(b_f.shape, const)],
        out_specs=out_specs,
        compiler_params=_cparams(1),
        name="inproj",
    )(x2d, g_mix, w_main, w_f, b_f)


def _cumsum_kernel(x_ref, o_ref):
    x = x_ref[...]
    length = x.shape[1]
    lane = lax.broadcasted_iota(I32, x.shape, 1)
    shift = 1
    while shift < length:
        x = x + jnp.where(lane >= shift, pltpu.roll(x, shift, axis=1), 0.0)
        shift *= 2
    o_ref[...] = x


def _cumsum_lanes(x):
    b, h, length = x.shape
    spec = pl.BlockSpec((None, h, length), lambda i: (i, 0, 0))
    return pl.pallas_call(
        _cumsum_kernel, out_shape=jax.ShapeDtypeStruct(x.shape, F32), grid=(b,),
        in_specs=[spec], out_specs=spec, compiler_params=_cparams(1), name="cumsum",
    )(x)


def _nt_dot(a, b):
    return lax.dot_general(a, b, (((1,), (1,)), ((), ())), preferred_element_type=F32)


def _fox_kernel(*refs, tq, tkp, n_past):
    if n_past:
        q_ref, kn_ref, vn_ref, cn_ref, kp_ref, vp_ref, cp_ref, o_ref, acc_a, acc_b, m_a, m_b = refs
    else:
        q_ref, kn_ref, vn_ref, cn_ref, o_ref, acc_a, acc_b, m_a, m_b = refs
        kp_ref = vp_ref = cp_ref = None
    i = pl.program_id(2)
    lo = lax.broadcasted_iota(I32, (1, LANES), 1) < HEAD_DIM

    q = q_ref[...]
    zero = jnp.zeros_like(q)
    q_heads = (jnp.where(lo, q, zero), jnp.where(lo, zero, q))
    for acc, m in ((acc_a, m_a), (acc_b, m_b)):
        acc[...] = jnp.zeros_like(acc)
        m[...] = jnp.full_like(m, NEG_INF)

    def chunk(k, v, c_rows, mask):
        k = k.astype(BF16)
        v = v.astype(BF16)
        one = jnp.ones_like(v)
        v_heads = (jnp.where(lo, v, one), jnp.where(lo, one, v))
        for hd, (acc, m) in enumerate(((acc_a, m_a), (acc_b, m_b))):
            s = _nt_dot(q_heads[hd], k) - c_rows[hd:hd + 1, :]
            if mask is not None:
                s = jnp.where(mask, s, NEG_INF)
            m_old = m[...]
            m_new = jnp.maximum(m_old, jnp.max(s, axis=1, keepdims=True))
            p = jnp.exp(s - m_new).astype(BF16)
            acc[...] = jnp.exp(m_old - m_new) * acc[...] + jnp.dot(
                p, v_heads[hd], preferred_element_type=F32)
            m[...] = m_new

    if n_past:
        def past_body(j, carry):
            off = pl.multiple_of(j * tkp, tkp)
            chunk(kp_ref[pl.ds(off, tkp), :], vp_ref[pl.ds(off, tkp), :],
                  cp_ref[:, pl.ds(off, tkp)], None)
            return carry
        lax.fori_loop(0, n_past, past_body, 0)
        off = 0
    else:
        def new_body(j, carry):
            off = pl.multiple_of(j * tq, tq)
            chunk(kn_ref[pl.ds(off, tq), :], vn_ref[pl.ds(off, tq), :],
                  cn_ref[:, pl.ds(off, tq)], None)
            return carry
        lax.fori_loop(0, i, new_body, 0)
        off = pl.multiple_of(i * tq, tq)

    causal = (lax.broadcasted_iota(I32, (tq, tq), 1) <= lax.broadcasted_iota(I32, (tq, tq), 0))
    chunk(kn_ref[pl.ds(off, tq), :], vn_ref[pl.ds(off, tq), :],
          cn_ref[:, pl.ds(off, tq)], causal)

    a = acc_a[...]
    b = acc_b[...]
    out = jnp.where(lo, a / pltpu.roll(a, HEAD_DIM, axis=1), b / pltpu.roll(b, HEAD_DIM, axis=1))
    o_ref[...] = out.astype(o_ref.dtype)


def _fox_attention(q, k_new, v_new, c_new, past, *, tq, tkp):
    b, t, w = q.shape
    n_pairs = w // LANES
    nq = t // tq
    pair_rows = lambda c: c.reshape(b, n_pairs, 2, c.shape[-1])
    qspec = pl.BlockSpec((None, tq, LANES), lambda bi, p, i: (bi, i, p))
    kvspec = pl.BlockSpec((None, t, LANES), lambda bi, p, i: (bi, 0, p))
    cspec = pl.BlockSpec((None, None, 2, t), lambda bi, p, i: (bi, p, 0, 0))
    in_specs = [qspec, kvspec, kvspec, cspec]
    args = [q, k_new, v_new, pair_rows(c_new)]
    n_past = 0
    if past is not None:
        assert nq == 1, "a cached prefix is supported for a single query tile only"
        kp, vp, cp = past
        plen = kp.shape[1]
        n_past = plen // tkp
        assert n_past * tkp == plen
        pspec = pl.BlockSpec((None, plen, LANES), lambda bi, p, i: (bi, 0, p))
        in_specs += [pspec, pspec,
                     pl.BlockSpec((None, None, 2, plen), lambda bi, p, i: (bi, p, 0, 0))]
        args += [kp, vp, pair_rows(cp)]
    return pl.pallas_call(
        functools.partial(_fox_kernel, tq=tq, tkp=tkp, n_past=n_past),
        out_shape=jax.ShapeDtypeStruct((b, t, w), BF16),
        grid=(b, n_pairs, nq),
        in_specs=in_specs,
        out_specs=qspec,
        scratch_shapes=[pltpu.VMEM((tq, LANES), F32), pltpu.VMEM((tq, LANES), F32),
                        pltpu.VMEM((tq, 1), F32), pltpu.VMEM((tq, 1), F32)],
        compiler_params=_cparams(3),
        name="fox_attn",
    )(*args)


def _diff_kernel(*refs, tq, tkp, n_past, lambda_init):
    if n_past:
        (q_ref, kn_ref, vn_ref, bprev_ref, bdiag_ref, lam_ref, g_ref, kp_ref, vp_ref,
         o_ref, acc0, acc1, m0, m1, l0, l1) = refs
    else:
        (q_ref, kn_ref, vn_ref, bprev_ref, bdiag_ref, lam_ref, g_ref,
         o_ref, acc0, acc1, m0, m1, l0, l1) = refs
        kp_ref = vp_ref = None
    i = pl.program_id(2)
    lo = lax.broadcasted_iota(I32, (1, LANES), 1) < HEAD_DIM

    q = q_ref[...]
    zero = jnp.zeros_like(q)
    q_maps = (jnp.where(lo, q, zero), jnp.where(lo, zero, q))
    state = ((acc0, m0, l0), (acc1, m1, l1))
    for acc, m, l in state:
        acc[...] = jnp.zeros_like(acc)
        m[...] = jnp.full_like(m, NEG_INF)
        l[...] = jnp.zeros_like(l)

    def chunk(k, v, bias, mask):
        k = k.astype(BF16)
        v = v.astype(BF16)
        for mp, (acc, m, l) in enumerate(state):
            s = _nt_dot(q_maps[mp], k)
            if bias is not None:
                s = s + bias
            if mask is not None:
                s = jnp.where(mask, s, NEG_INF)
            m_old = m[...]
            m_new = jnp.maximum(m_old, jnp.max(s, axis=1, keepdims=True))
            p = jnp.exp(s - m_new)
            alpha = jnp.exp(m_old - m_new)
            l[...] = alpha * l[...] + jnp.sum(p, axis=1, keepdims=True)
            acc[...] = alpha * acc[...] + jnp.dot(p.astype(BF16), v, preferred_element_type=F32)
            m[...] = m_new

    if n_past:
        def past_body(j, carry):
            off = pl.multiple_of(j * tkp, tkp)
            chunk(kp_ref[pl.ds(off, tkp), :], vp_ref[pl.ds(off, tkp), :], None, None)
            return carry
        lax.fori_loop(0, n_past - 1, past_body, 0)
        off = (n_past - 1) * tkp
        chunk(kp_ref[pl.ds(off, tkp), :], vp_ref[pl.ds(off, tkp), :], bprev_ref[...], None)
        off = 0
    else:
        def new_body(j, carry):
            off = pl.multiple_of(j * tq, tq)
            chunk(kn_ref[pl.ds(off, tq), :], vn_ref[pl.ds(off, tq), :], None, None)
            return carry
        lax.fori_loop(0, jnp.maximum(i - 1, 0), new_body, 0)

        @pl.when(i >= 1)
        def _():
            off = pl.multiple_of((i - 1) * tq, tq)
            chunk(kn_ref[pl.ds(off, tq), :], vn_ref[pl.ds(off, tq), :], bprev_ref[...], None)

        off = pl.multiple_of(i * tq, tq)

    row_chunk = lax.broadcasted_iota(I32, (tq, tq), 0) // CHUNK
    col_chunk = lax.broadcasted_iota(I32, (tq, tq), 1) // CHUNK
    chunk(kn_ref[pl.ds(off, tq), :], vn_ref[pl.ds(off, tq), :], bdiag_ref[...],
          col_chunk <= row_chunk)

    dl = lam_ref[...]
    lam = (jnp.exp(jnp.sum(dl[0:1] * dl[1:2], axis=1, keepdims=True))
           - jnp.exp(jnp.sum(dl[2:3] * dl[3:4], axis=1, keepdims=True)) + lambda_init)
    o = acc0[...] / l0[...] - lam * (acc1[...] / l1[...])
    o = o * _rms_scale(o, SUBLN_EPS) * g_ref[...] * (1.0 - lambda_init)
    o_ref[...] = o.astype(o_ref.dtype)


def _diff_attention(q, k_new, v_new, bias_prev, bias_diag, diff_lambda, subln_g, past, *,
                    tq, tkp, lambda_init):
    b, t, w = q.shape
    n_heads = w // LANES
    nq = t // tq
    qspec = pl.BlockSpec((None, tq, LANES), lambda bi, h, i: (bi, i, h))
    kvspec = pl.BlockSpec((None, t, LANES), lambda bi, h, i: (bi, 0, h))
    head_tile = lambda arr: pl.BlockSpec((None,) + arr.shape[1:], lambda bi, h, i: (h, 0, 0))
    const2 = lambda arr: pl.BlockSpec(arr.shape, lambda bi, h, i: (0, 0))
    in_specs = [qspec, kvspec, kvspec, head_tile(bias_prev), head_tile(bias_diag),
                const2(diff_lambda), const2(subln_g)]
    args = [q, k_new, v_new, bias_prev, bias_diag, diff_lambda, subln_g]
    n_past = 0
    if past is not None:
        assert nq == 1, "a cached prefix is supported for a single query tile only"
        kp, vp = past
        plen = kp.shape[1]
        n_past = plen // tkp
        assert n_past * tkp == plen and tkp >= MAX_DISTANCE
        pspec = pl.BlockSpec((None, plen, LANES), lambda bi, h, i: (bi, 0, h))
        in_specs += [pspec, pspec]
        args += [kp, vp]
    else:
        assert tq >= MAX_DISTANCE
    return pl.pallas_call(
        functools.partial(_diff_kernel, tq=tq, tkp=tkp, n_past=n_past, lambda_init=lambda_init),
        out_shape=jax.ShapeDtypeStruct((b, t, w), BF16),
        grid=(b, n_heads, nq),
        in_specs=in_specs,
        out_specs=qspec,
        scratch_shapes=[pltpu.VMEM((tq, LANES), F32), pltpu.VMEM((tq, LANES), F32)]
        + [pltpu.VMEM((tq, 1), F32)] * 4,
        compiler_params=_cparams(3),
        name="diff_attn",
    )(*args)


def _t5_bucket(rel):
    nb = NUM_BUCKETS // 2
    max_exact = nb // 2
    ret = jnp.where(rel > 0, nb, 0)
    n = jnp.abs(rel)
    nf = jnp.maximum(n, max_exact).astype(F32)
    large = max_exact + (jnp.log(nf / max_exact) / math.log(MAX_DISTANCE / max_exact)
                         * (nb - max_exact)).astype(I32)
    large = jnp.minimum(large, nb - 1)
    return ret + jnp.where(n < max_exact, n, large)


def _bias_tiles(rel_table, tq, tk_prev):
    buckets = jnp.arange(NUM_BUCKETS, dtype=I32)

    def lookup(rel):
        hit = _t5_bucket(rel)[None, :] == buckets[:, None]
        return jnp.sum(jnp.where(hit[:, None, :], rel_table[:, :, None], 0.0), axis=0)

    far = lookup(jnp.full((1,), -4 * MAX_DISTANCE, I32))

    def tile(rel0, tk):
        period = tq + tk
        j = jnp.arange(period, dtype=I32)
        strip = lookup(rel0 + jnp.where(j < tk, j, j - period)) - far
        flat = jnp.tile(strip, (1, tq))[:, :tq * (period - 1)]
        return flat.reshape(-1, tq, period - 1)[:, :, :tk].astype(F32)

    return tile(-tk_prev, tk_prev), tile(0, tq)


def _router_kernel(x_ref, of_ref, od_ref, wof_ref, wod_ref, g_ref, wr_ref, br_ref,
                   x1_ref, hn_ref, te_ref, tw_ref, rank_ref, cnt_ref, *, tm):
    step = pl.program_id(0)

    @pl.when(step == 0)
    def _():
        cnt_ref[...] = jnp.zeros_like(cnt_ref)

    x1 = (x_ref[...] + jnp.dot(of_ref[...], wof_ref[...], preferred_element_type=F32)
          + jnp.dot(od_ref[...], wod_ref[...], preferred_element_type=F32))
    x1_ref[...] = x1
    hn = x1 * _rms_scale(x1, NORM_EPS) * g_ref[...]
    half = hn.shape[1] // 2
    hn_ref[...] = _pack_bf16_pairs(hn[:, :half], hn[:, half:])

    logits = jnp.dot(hn, wr_ref[...], preferred_element_type=F32,
                     precision=lax.Precision.HIGHEST) + br_ref[...]
    lane = lax.broadcasted_iota(I32, logits.shape, 1)
    lane_f = lane.astype(F32)
    sel = jnp.zeros(logits.shape, F32)
    vals, idxs = [], []
    for _ in range(TOP_K):
        mx = jnp.max(logits, axis=1, keepdims=True)
        idx = jnp.min(jnp.where(logits == mx, lane_f, float(LANES)), axis=1, keepdims=True)
        hit = lane_f == idx
        sel = jnp.where(hit, 1.0, sel)
        logits = jnp.where(hit, -jnp.inf, logits)
        vals.append(mx)
        idxs.append(idx)
    exps = [jnp.exp(v - vals[0]) for v in vals]
    denom = exps[0] + exps[1] + exps[2] + exps[3]

    earlier = (lax.broadcasted_iota(I32, (tm, tm), 1)
               < lax.broadcasted_iota(I32, (tm, tm), 0)).astype(BF16)
    before = cnt_ref[...] + jnp.dot(earlier, sel.astype(BF16), preferred_element_type=F32)
    for k in range(TOP_K):
        te_ref[:, k:k + 1] = idxs[k].astype(I32)
        tw_ref[:, k:k + 1] = exps[k] / denom
        rank_ref[:, k:k + 1] = jnp.sum(
            jnp.where(lane_f == idxs[k], before, 0.0), axis=1, keepdims=True).astype(I32)
    cnt_ref[...] += jnp.sum(sel, axis=0, keepdims=True)


def _router(x2d, o_fox, o_diff, wo_f, wo_d, g_ffn, w_r, b_r, *, tm):
    n, d = x2d.shape
    row = lambda i: (i, 0)
    const = lambda i: (0, 0)
    full = lambda arr: pl.BlockSpec(arr.shape, const)
    return pl.pallas_call(
        functools.partial(_router_kernel, tm=tm),
        out_shape=[jax.ShapeDtypeStruct((n, d), F32), jax.ShapeDtypeStruct((n, d // 2), U32),
                   jax.ShapeDtypeStruct((n, TOP_K), I32), jax.ShapeDtypeStruct((n, TOP_K), F32),
                   jax.ShapeDtypeStruct((n, TOP_K), I32), jax.ShapeDtypeStruct((1, LANES), F32)],
        grid=(n // tm,),
        in_specs=[pl.BlockSpec((tm, d), row), pl.BlockSpec((tm, o_fox.shape[1]), row),
                  pl.BlockSpec((tm, o_diff.shape[1]), row), full(wo_f), full(wo_d),
                  full(g_ffn), full(w_r), full(b_r)],
        out_specs=[pl.BlockSpec((tm, d), row), pl.BlockSpec((tm, d // 2), row),
                   pl.BlockSpec((tm, TOP_K), row), pl.BlockSpec((tm, TOP_K), row),
                   pl.BlockSpec((tm, TOP_K), row), pl.BlockSpec((1, LANES), const)],
        compiler_params=_cparams(1),
        name="router",
    )(x2d, o_fox, o_diff, wo_f, wo_d, g_ffn, w_r, b_r)


def _row_copy(src, src_row, dst, dst_row, sem):
    return pltpu.make_async_copy(src.at[pl.ds(src_row, 1), :], dst.at[pl.ds(dst_row, 1), :], sem)


def _dispatch_kernel(tail_start_ref, tail_valid_ref, pos_ref, hn_ref, xs_hbm, zeros_ref, sem,
                     *, tm, tme, n_experts):
    def tail_copy(e):
        return pltpu.make_async_copy(
            zeros_ref, xs_hbm.at[pl.ds(pl.multiple_of(tail_start_ref[e], tme), tme), :], sem.at[0])

    @pl.when(pl.program_id(0) == 0)
    def _():
        zeros_ref[...] = jnp.zeros_like(zeros_ref)
        for e in range(n_experts):
            @pl.when(tail_valid_ref[e] > 0)
            def _():
                tail_copy(e).start()
        for e in range(n_experts):
            @pl.when(tail_valid_ref[e] > 0)
            def _():
                tail_copy(e).wait()

    def start_row(r, carry):
        for k in range(TOP_K):
            _row_copy(hn_ref, r, xs_hbm, pos_ref[r, k], sem.at[1]).start()
        return carry

    def wait_row(r, carry):
        for k in range(TOP_K):
            _row_copy(hn_ref, r, xs_hbm, pos_ref[r, k], sem.at[1]).wait()
        return carry

    lax.fori_loop(0, tm, start_row, 0, unroll=8)
    lax.fori_loop(0, tm, wait_row, 0, unroll=8)


def _dispatch(hn_packed, pos, tail_start, tail_valid, *, m_pad, tm, tme):
    n, half = hn_packed.shape
    n_experts = tail_start.shape[0]
    grid_spec = pltpu.PrefetchScalarGridSpec(
        num_scalar_prefetch=2,
        grid=(n // tm,),
        in_specs=[pl.BlockSpec((tm, TOP_K), lambda i, *_: (i, 0), memory_space=pltpu.SMEM),
                  pl.BlockSpec((tm, half), lambda i, *_: (i, 0))],
        out_specs=pl.BlockSpec(memory_space=pl.ANY),
        scratch_shapes=[pltpu.VMEM((tme, half), U32), pltpu.SemaphoreType.DMA((2,))],
    )
    return pl.pallas_call(
        functools.partial(_dispatch_kernel, tm=tm, tme=tme, n_experts=n_experts),
        out_shape=jax.ShapeDtypeStruct((m_pad, half), U32),
        grid_spec=grid_spec,
        compiler_params=pltpu.CompilerParams(
            dimension_semantics=("arbitrary",), vmem_limit_bytes=VMEM_LIMIT_BYTES,
            has_side_effects=True),
        name="dispatch",
    )(tail_start, tail_valid, pos, hn_packed)


def _experts_kernel(te_ref, nact_ref, xs_ref, wg_ref, bg_ref, wu_ref, bu_ref, wd_ref, bd_ref,
                    ys_ref):
    @pl.when(pl.program_id(0) < nact_ref[0])
    def _():
        half = xs_ref.shape[1]
        x_lo, x_hi = _unpack_bf16_pairs(xs_ref[...])
        x_lo = x_lo.astype(BF16)
        x_hi = x_hi.astype(BF16)

        def proj(w_ref, b_ref):
            return (jnp.dot(x_lo, w_ref[:half, :], preferred_element_type=F32)
                    + jnp.dot(x_hi, w_ref[half:, :], preferred_element_type=F32) + b_ref[...])

        g = jnp.minimum(proj(wg_ref, bg_ref), SWIGLU_LIMIT)
        u = jnp.clip(proj(wu_ref, bu_ref), -SWIGLU_LIMIT, SWIGLU_LIMIT)
        a = ((u + 1.0) * (g * jax.nn.sigmoid(SWIGLU_ALPHA * g))).astype(BF16)
        y = jnp.dot(a, wd_ref[...], preferred_element_type=F32) + bd_ref[...]
        ys_ref[...] = _pack_bf16_pairs(y[:, :half], y[:, half:])


def _experts(xs, tile_expert, n_active, w_gate, b_gate, w_up, b_up, w_down, b_down, *, tme):
    m_pad, half = xs.shape
    n_tiles = m_pad // tme
    d, d_ff = w_gate.shape[1:]
    row = lambda t, te, na: (jnp.minimum(t, na[0] - 1), 0)
    wspec = lambda shape: pl.BlockSpec((None,) + shape, lambda t, te, na: (te[t], 0, 0))
    grid_spec = pltpu.PrefetchScalarGridSpec(
        num_scalar_prefetch=2,
        grid=(n_tiles,),
        in_specs=[pl.BlockSpec((tme, half), row),
                  wspec((d, d_ff)), wspec((1, d_ff)), wspec((d, d_ff)), wspec((1, d_ff)),
                  wspec((d_ff, d)), wspec((1, d))],
        out_specs=pl.BlockSpec((tme, half), row),
    )
    return pl.pallas_call(
        _experts_kernel,
        out_shape=jax.ShapeDtypeStruct((m_pad, half), U32),
        grid_spec=grid_spec,
        compiler_params=_cparams(1),
        name="experts",
    )(tile_expert, n_active, xs, w_gate, b_gate, w_up, b_up, w_down, b_down)


def _combine_kernel(pos_ref, x1_ref, tw_ref, p_ref, wg_ref, wp_ref, gple_ref, gfin_ref, ys_hbm,
                    y_ref, gbuf, sem, *, tm):
    def start_row(r, carry):
        for k in range(TOP_K):
            _row_copy(ys_hbm, pos_ref[r, k], gbuf.at[k], r, sem).start()
        return carry

    def wait_row(r, carry):
        for k in range(TOP_K):
            _row_copy(ys_hbm, pos_ref[r, k], gbuf.at[k], r, sem).wait()
        return carry

    lax.fori_loop(0, tm, start_row, 0, unroll=8)
    lax.fori_loop(0, tm, wait_row, 0, unroll=8)

    x1 = x1_ref[...]
    half = x1.shape[1] // 2
    tw = tw_ref[...]
    moe_lo = jnp.zeros((tm, half), F32)
    moe_hi = jnp.zeros((tm, half), F32)
    for k in range(TOP_K):
        lo, hi = _unpack_bf16_pairs(gbuf[k])
        moe_lo = moe_lo + tw[:, k:k + 1] * lo
        moe_hi = moe_hi + tw[:, k:k + 1] * hi
    x2 = x1 + jnp.concatenate([moe_lo, moe_hi], axis=1)
    hp = (x2 * _rms_scale(x2, NORM_EPS) * gple_ref[...]).astype(BF16)
    gate = jax.nn.sigmoid(jnp.dot(hp, wg_ref[...], preferred_element_type=F32))
    proj = jnp.dot(p_ref[...].astype(BF16), wp_ref[...], preferred_element_type=F32)
    x3 = x2 + proj * gate
    y_ref[...] = x3 * _rms_scale(x3, NORM_EPS) * gfin_ref[...]


def _combine(x1, top_w, pos, ys, p2d, w_ple_gate, w_ple_proj, g_ple, g_final, *, tm):
    n, d = x1.shape
    half = ys.shape[1]
    row = lambda i: (i, 0)
    const = lambda i: (0, 0)
    full = lambda arr: pl.BlockSpec(arr.shape, const)
    return pl.pallas_call(
        functools.partial(_combine_kernel, tm=tm),
        out_shape=jax.ShapeDtypeStruct((n, d), F32),
        grid=(n // tm,),
        in_specs=[pl.BlockSpec((tm, TOP_K), row, memory_space=pltpu.SMEM),
                  pl.BlockSpec((tm, d), row), pl.BlockSpec((tm, TOP_K), row),
                  pl.BlockSpec((tm, p2d.shape[1]), row), full(w_ple_gate), full(w_ple_proj),
                  full(g_ple), full(g_final), pl.BlockSpec(memory_space=pl.ANY)],
        out_specs=pl.BlockSpec((tm, d), row),
        scratch_shapes=[pltpu.VMEM((TOP_K, tm, half), U32), pltpu.SemaphoreType.DMA(())],
        compiler_params=_cparams(1),
        name="combine",
    )(pos, x1, top_w, p2d, w_ple_gate, w_ple_proj, g_ple, g_final, ys)


def _pick_tile(n, want):
    t = min(n, want)
    assert n % t == 0, (n, t)
    return t


def _layer_group(x, p, past, wts, lambda_init, g_final):
    (g_mix, w_main, w_f, b_f, diff_lambda, subln_g, rel_table, wo_f, wo_d, g_ffn, w_r, b_r,
     w_gate, b_gate, w_up, b_up, w_down, b_down, g_ple, w_ple_gate, w_ple_proj) = wts
    b, t, d = x.shape
    n = b * t
    n_fox = (w_main.shape[1] // 6) // HEAD_DIM
    n_experts = w_gate.shape[0]
    x2d = x.reshape(n, d)

    tm = _pick_tile(n, 512)
    (qf, qd, kf, vf, kd, vd, kf16, vf16, kd16, vd16, logf) = _inproj(
        x2d, g_mix, w_main, w_f, b_f, tm=tm, n_fox=n_fox)
    width = kf.shape[1]
    as_seq = lambda a: a.reshape(b, t, a.shape[-1])

    logf_t = jnp.transpose(as_seq(logf), (0, 2, 1))
    tq = _pick_tile(t, 512)
    tkp = 512
    if past is None:
        c_new = _cumsum_lanes(logf_t)
        fox_past = diff_past = None
    else:
        pk_f, pv_f, plogf, pk_d, pv_d = past
        plen = pk_f.shape[1]
        total = plen + t
        padded = -(-total // LANES) * LANES
        seq = jnp.concatenate([jnp.transpose(plogf.astype(F32), (0, 2, 1)), logf_t], axis=2)
        c_all = _cumsum_lanes(jnp.pad(seq, ((0, 0), (0, 0), (0, padded - total))))
        c_new = c_all[:, :, plen:total]
        fox_past = (pk_f.reshape(b, plen, width), pv_f.reshape(b, plen, width), c_all[:, :, :plen])
        diff_past = (pk_d.reshape(b, plen, width), pv_d.reshape(b, plen, width))

    o_fox = _fox_attention(as_seq(qf), as_seq(kf16), as_seq(vf16), c_new, fox_past, tq=tq, tkp=tkp)
    bias_prev, bias_diag = _bias_tiles(rel_table, tq, tq if past is None else tkp)
    o_diff = _diff_attention(as_seq(qd), as_seq(kd16), as_seq(vd16), bias_prev, bias_diag,
                             diff_lambda, subln_g, diff_past, tq=tq, tkp=tkp,
                             lambda_init=lambda_init)

    tmr = _pick_tile(n, 256)
    x1, hn_packed, top_e, top_w, rank, counts = _router(
        x2d, o_fox.reshape(n, width), o_diff.reshape(n, width), wo_f, wo_d, g_ffn, w_r, b_r, tm=tmr)

    tme = 512 if n * TOP_K >= 64 * 512 else 256
    n_tiles = -(-(n * TOP_K) // tme) + n_experts
    m_pad = n_tiles * tme
    cnt = counts[0, :n_experts].astype(I32)
    group = -(-cnt // tme) * tme
    ends = jnp.cumsum(group)
    starts = ends - group
    experts = jnp.arange(n_experts, dtype=I32)
    pos = rank + jnp.sum(jnp.where(top_e[:, :, None] == experts, starts, 0), axis=2)
    n_active = (ends[-1] // tme).astype(I32)
    tile_ids = jnp.arange(n_tiles, dtype=I32)
    tile_expert = jnp.sum(jnp.minimum(tile_ids, n_active - 1)[:, None] >= (ends // tme)[None, :],
                          axis=1).astype(I32)
    tile_expert = jnp.minimum(tile_expert, n_experts - 1)
    tail_start = jnp.maximum(ends - tme, 0).astype(I32)
    tail_valid = (group > 0).astype(I32)

    tmd = _pick_tile(n, 256)
    xs = _dispatch(hn_packed, pos, tail_start, tail_valid, m_pad=m_pad, tm=tmd, tme=tme)
    ys = _experts(xs, tile_expert, n_active.reshape(1), w_gate, b_gate, w_up, b_up, w_down, b_down,
                  tme=tme)
    y = _combine(x1, top_w, pos, ys, p.reshape(n, p.shape[-1]), w_ple_gate, w_ple_proj, g_ple,
                 g_final, tm=tmd)
    return y.reshape(b, t, d), (kf, vf, logf, kd, vd)


def kernel(x_prompt, x_sample, p_prompt, p_sample, cache_fox_k, cache_fox_v, cache_fox_logf, cache_diff_k, cache_diff_v, g_mix, w_in, b_forget, diff_lambda, diff_subln_g, rel_bias_table, w_o, g_ffn, w_router, b_router, w_gate, b_gate, w_up, b_up, w_down, b_down, g_ple, w_ple_gate, w_ple_proj, g_final):
    depth = g_mix.shape[0]
    assert depth == 1, "the final norm is fused into the last layer; one layer supported"
    d = x_prompt.shape[-1]
    n_fox = cache_fox_k.shape[3]
    fox_w = n_fox * cache_fox_k.shape[4]
    n_diff = cache_diff_k.shape[3]
    diff_w = n_diff * 2 * cache_diff_k.shape[5]
    assert fox_w == diff_w and fox_w % LANES == 0
    n_experts = w_router.shape[-1]
    assert TOP_K <= n_experts <= LANES
    row2 = lambda a: a.reshape(1, -1).astype(F32)

    i = 0
    lambda_init = 0.8 - 0.6 * math.exp(-0.3 * i)
    off_fk, off_fv, off_ff = fox_w, 2 * fox_w, 3 * fox_w
    off_dq = off_ff + n_fox
    off_dk, off_dv = off_dq + diff_w, off_dq + 2 * diff_w
    w = w_in[i]
    cols = lambda o, wd: w[:, o:o + wd]
    w_main = jnp.concatenate([cols(0, fox_w), cols(off_dq, diff_w), cols(off_fk, fox_w),
                              cols(off_fv, fox_w), cols(off_dk, diff_w), cols(off_dv, diff_w)],
                             axis=1).astype(BF16)
    w_f = jnp.pad(cols(off_ff, n_fox), ((0, 0), (0, LANES - n_fox))).astype(BF16)
    b_f = jnp.pad(row2(b_forget[i]), ((0, 0), (0, LANES - n_fox)))
    w_r = jnp.pad(w_router[i].astype(F32), ((0, 0), (0, LANES - n_experts)))
    b_r = jnp.pad(row2(b_router[i]), ((0, 0), (0, LANES - n_experts)), constant_values=NEG_INF)
    wo = w_o[i].astype(BF16)
    bias3 = lambda a: a.reshape(n_experts, 1, -1).astype(F32)
    wts = (row2(g_mix[i]), w_main, w_f, b_f, diff_lambda[i].astype(F32), row2(diff_subln_g[i]),
           rel_bias_table.astype(F32), wo[:fox_w], wo[fox_w:], row2(g_ffn[i]), w_r, b_r,
           w_gate[i].astype(BF16), bias3(b_gate[i]), w_up[i].astype(BF16), bias3(b_up[i]),
           w_down[i].astype(BF16), bias3(b_down[i]), row2(g_ple[i]),
           w_ple_gate[i].astype(BF16), w_ple_proj[i].astype(BF16))
    gf = row2(g_final)

    yp, new_p = _layer_group(x_prompt, p_prompt[i], None, wts, lambda_init, gf)
    past_s = (cache_fox_k[i], cache_fox_v[i], cache_fox_logf[i], cache_diff_k[i], cache_diff_v[i])
    ys, new_s = _layer_group(x_sample, p_sample[i], past_s, wts, lambda_init, gf)

    def rows(new, b, t):
        kf, vf, logf, kd, vd = new
        return (kf.reshape(1, b, t, n_fox, -1), vf.reshape(1, b, t, n_fox, -1),
                logf.reshape(1, b, t, n_fox), kd.reshape(1, b, t, n_diff, 2, -1),
                vd.reshape(1, b, t, n_diff, -1))

    bp, tp = x_prompt.shape[:2]
    bs, ts = x_sample.shape[:2]
    return (yp, ys) + rows(new_p, bp, tp) + rows(new_s, bs, ts)
```

```python
import functools
import math

import jax
import jax.numpy as jnp
import numpy as np
from jax import lax
from jax.experimental import pallas as pl
from jax.experimental.pallas import tpu as pltpu

HEAD_DIM = 64
CHUNK = 64
NUM_BUCKETS = 32
MAX_DISTANCE = 128
TOP_K = 4
SWIGLU_LIMIT = 7.0
SWIGLU_ALPHA = 1.702
NORM_EPS = 1e-6
SUBLN_EPS = 1e-5
NEG_INF = -1e30
LOG2E = math.log2(math.e)

LANES = 128
VMEM_LIMIT_BYTES = 56 * 1024 * 1024

F32 = jnp.float32
BF16 = jnp.bfloat16
I32 = jnp.int32
U32 = jnp.uint32


def _cparams(n_axes):
    return pltpu.CompilerParams(
        dimension_semantics=("arbitrary",) * n_axes, vmem_limit_bytes=VMEM_LIMIT_BYTES)


def _rms_scale(x, eps):
    return lax.rsqrt(jnp.mean(x * x, axis=-1, keepdims=True) + eps)


def _pack_bf16_pairs(lo, hi):
    lo_bits = pltpu.bitcast(lo.astype(BF16).astype(F32), U32)
    hi_bits = pltpu.bitcast(hi.astype(BF16).astype(F32), U32)
    return lo_bits | (hi_bits >> 16)


def _unpack_bf16_pairs(packed):
    lo = pltpu.bitcast(packed & jnp.uint32(0xFFFF0000), F32)
    hi = pltpu.bitcast(packed << 16, F32)
    return lo, hi


def _inproj_kernel(x_ref, g_ref, w_ref, wvt_ref, wf_ref, bf_ref,
                   qf_ref, qd_ref, kf_ref, vf_ref, kd_ref, vd_ref,
                   kf16_ref, kd16_ref, vf16_ref, vd16_ref, logf_ref, *, width, values_transposed):
    x = x_ref[...]
    h = (x * _rms_scale(x, NORM_EPS) * g_ref[...]).astype(BF16)
    q_scale = LOG2E / math.sqrt(HEAD_DIM)

    def proj(c):
        return jnp.dot(h, w_ref[:, c * width:(c + 1) * width], preferred_element_type=F32)

    qf_ref[...] = (proj(0) * q_scale).astype(BF16)
    qd_ref[...] = (proj(1) * q_scale).astype(BF16)
    for c, (o32, o16) in enumerate(((kf_ref, kf16_ref), (vf_ref, vf16_ref),
                                    (kd_ref, kd16_ref), (vd_ref, vd16_ref))):
        u = proj(2 + c)
        o32[...] = u
        if not (values_transposed and c in (1, 3)):
            o16[...] = u.astype(BF16)
    if values_transposed:
        vf16_ref[...] = _nt_dot(wvt_ref[:width, :], h).astype(BF16)
        vd16_ref[...] = _nt_dot(wvt_ref[width:, :], h).astype(BF16)
    f = jnp.dot(h, wf_ref[...], preferred_element_type=F32) + bf_ref[...]
    logf_ref[...] = jax.nn.log_sigmoid(f)


def _inproj(x2d, g_mix, w_main, w_vt, w_f, b_f, *, tm, values_transposed):
    n, d = x2d.shape
    width = w_main.shape[1] // 6
    row = lambda i: (i, 0)
    const = lambda i: (0, 0)
    if values_transposed:
        v16_shape, v16_spec = (width, n), pl.BlockSpec((width, tm), lambda i: (0, i))
    else:
        v16_shape, v16_spec = (n, width), pl.BlockSpec((tm, width), row)
    out_shape = ([jax.ShapeDtypeStruct((n, width), BF16)] * 2
                 + [jax.ShapeDtypeStruct((n, width), F32)] * 4
                 + [jax.ShapeDtypeStruct((n, width), BF16)] * 2
                 + [jax.ShapeDtypeStruct(v16_shape, BF16)] * 2
                 + [jax.ShapeDtypeStruct((n, LANES), F32)])
    out_specs = ([pl.BlockSpec((tm, width), row)] * 8 + [v16_spec] * 2
                 + [pl.BlockSpec((tm, LANES), row)])
    return pl.pallas_call(
        functools.partial(_inproj_kernel, width=width, values_transposed=values_transposed),
        out_shape=out_shape,
        grid=(n // tm,),
        in_specs=[pl.BlockSpec((tm, d), row), pl.BlockSpec((1, d), const),
                  pl.BlockSpec(w_main.shape, const), pl.BlockSpec(w_vt.shape, const),
                  pl.BlockSpec(w_f.shape, const), pl.BlockSpec(b_f.shape, const)],
        out_specs=out_specs,
        compiler_params=_cparams(1),
        name="inproj",
    )(x2d, g_mix, w_main, w_vt, w_f, b_f)


def _cumsum_kernel(x_ref, o_ref):
    x = x_ref[...]
    length = x.shape[1]
    lane = lax.broadcasted_iota(I32, x.shape, 1)
    shift = 1
    while shift < length:
        x = x + jnp.where(lane >= shift, pltpu.roll(x, shift, axis=1), 0.0)
        shift *= 2
    o_ref[...] = x


def _cumsum_lanes(x):
    b, h, length = x.shape
    spec = pl.BlockSpec((None, h, length), lambda i: (i, 0, 0))
    return pl.pallas_call(
        _cumsum_kernel, out_shape=jax.ShapeDtypeStruct(x.shape, F32), grid=(b,),
        in_specs=[spec], out_specs=spec, compiler_params=_cparams(1), name="cumsum",
    )(x)


N_DECAY_TERMS = 3


def _key_decay_kernel(x_ref, sel_ref, o_ref):
    x = x_ref[...]
    length = x.shape[0]
    row = lax.broadcasted_iota(I32, x.shape, 0)
    shift = 1
    while shift < length:
        x = x + jnp.where(row >= shift, pltpu.roll(x, shift, axis=0), 0.0)
        shift *= 2
    rest = x * (-LOG2E)
    out = jnp.zeros(o_ref.shape, F32)
    for j in range(N_DECAY_TERMS):
        term = rest.astype(BF16)
        rest = rest - term.astype(F32)
        out = out + jnp.dot(term, sel_ref[j], preferred_element_type=F32)
    o_ref[...] = out.astype(BF16)


def _decay_selectors(n_heads, width):
    sel = np.zeros((N_DECAY_TERMS, LANES, width), np.float32)
    for h in range(n_heads):
        base = (h // 2) * LANES + (HEAD_DIM if h % 2 == 0 else 0)
        for j in range(N_DECAY_TERMS):
            sel[j, h, base + j] = 1.0
    return jnp.asarray(sel, BF16)


def _key_decay(logf, n_heads, width):
    b, length, lanes = logf.shape
    sel = _decay_selectors(n_heads, width)
    return pl.pallas_call(
        _key_decay_kernel, out_shape=jax.ShapeDtypeStruct((b, length, width), BF16), grid=(b,),
        in_specs=[pl.BlockSpec((None, length, lanes), lambda i: (i, 0, 0)),
                  pl.BlockSpec(sel.shape, lambda i: (0, 0, 0))],
        out_specs=pl.BlockSpec((None, length, width), lambda i: (i, 0, 0)),
        compiler_params=_cparams(1), name="key_decay",
    )(logf, sel)


def _nt_dot(a, b):
    return lax.dot_general(a, b, (((1,), (1,)), ((), ())), preferred_element_type=F32)


SCORE_LOOKAHEAD = 4


def _pipelined(units, scores, consume):
    ahead = [scores(unit) for unit in units[:SCORE_LOOKAHEAD]]
    for n, unit in enumerate(units):
        s = ahead.pop(0)
        if n + SCORE_LOOKAHEAD < len(units):
            ahead.append(scores(units[n + SCORE_LOOKAHEAD]))
        consume(unit, s)


def _fox_kernel(*refs, tq, rs, tkp, n_past):
    if n_past:
        q_ref, kn_ref, vn_ref, cn_ref, kp_ref, vp_ref, cp_ref, o_ref, acc_a, acc_b, m_a, m_b = refs
    else:
        q_ref, kn_ref, vn_ref, cn_ref, o_ref, acc_a, acc_b, m_a, m_b = refs
        kp_ref = vp_ref = cp_ref = None
    i = pl.program_id(2)
    n_qs = tq // rs
    lo = lax.broadcasted_iota(I32, (1, LANES), 1) < HEAD_DIM

    q = q_ref[...]
    zero = jnp.zeros_like(q)
    q_heads = (jnp.where(lo, q, zero), jnp.where(lo, zero, q))
    accs = (acc_a, acc_b)
    ms = (m_a, m_b)
    for acc, m in zip(accs, ms):
        acc[...] = jnp.zeros_like(acc)
        m[...] = jnp.full_like(m, NEG_INF)

    def chunk(k, v, c_rows, units):
        k = k.astype(BF16)
        v = v.astype(BF16)
        one = jnp.ones_like(v)
        v_heads = (jnp.where(lo, v, one), jnp.where(lo, one, v))
        c2 = c_rows * LOG2E

        def scores(unit):
            hd, qs, _ = unit
            return _nt_dot(q_heads[hd][qs * rs:(qs + 1) * rs], k)

        def consume(unit, s):
            hd, qs, mask = unit
            rows = pl.ds(qs * rs, rs)
            s = s - c2[hd:hd + 1, :]
            if mask is not None:
                s = jnp.where(mask, s, NEG_INF)
            m_old = ms[hd][rows, :]
            m_new = jnp.maximum(m_old, jnp.max(s, axis=1, keepdims=True))
            p = jnp.exp2(s - m_new).astype(BF16)
            accs[hd][rows, :] = jnp.exp2(m_old - m_new) * accs[hd][rows, :] + jnp.dot(
                p, v_heads[hd], preferred_element_type=F32)
            ms[hd][rows, :] = m_new

        _pipelined(units, scores, consume)

    full = [(hd, qs, None) for qs in range(n_qs) for hd in range(2)]
    causal = (lax.broadcasted_iota(I32, (rs, rs), 1) <= lax.broadcasted_iota(I32, (rs, rs), 0))
    if n_past:
        def past_body(j, carry):
            off = pl.multiple_of(j * tkp, tkp)
            chunk(kp_ref[pl.ds(off, tkp), :], vp_ref[pl.ds(off, tkp), :],
                  cp_ref[:, pl.ds(off, tkp)], full)
            return carry
        lax.fori_loop(0, n_past, past_body, 0)
        chunk(kn_ref[...], vn_ref[...], cn_ref[...], [(hd, 0, causal) for hd in range(2)])
    else:
        def new_body(j, carry):
            off = pl.multiple_of(j * rs, rs)
            chunk(kn_ref[pl.ds(off, rs), :], vn_ref[pl.ds(off, rs), :],
                  cn_ref[:, pl.ds(off, rs)], full)
            return carry
        lax.fori_loop(0, n_qs * i, new_body, 0)
        for d in range(n_qs):
            off = pl.multiple_of((n_qs * i + d) * rs, rs)
            chunk(kn_ref[pl.ds(off, rs), :], vn_ref[pl.ds(off, rs), :], cn_ref[:, pl.ds(off, rs)],
                  [(hd, qs, causal if qs == d else None)
                   for qs in range(d, n_qs) for hd in range(2)])

    a = acc_a[...]
    b = acc_b[...]
    out = jnp.where(lo, a / pltpu.roll(a, HEAD_DIM, axis=1), b / pltpu.roll(b, HEAD_DIM, axis=1))
    o_ref[...] = out.astype(o_ref.dtype)


def _fox_attention(q, k_new, v_new, c_new, past, *, tq, rs, tkp):
    b, t, w = q.shape
    n_pairs = w // LANES
    nq = t // tq
    pair_rows = lambda c: c.reshape(b, n_pairs, 2, c.shape[-1])
    qspec = pl.BlockSpec((None, tq, LANES), lambda bi, p, i: (bi, i, p))
    kvspec = pl.BlockSpec((None, t, LANES), lambda bi, p, i: (bi, 0, p))
    cspec = pl.BlockSpec((None, None, 2, t), lambda bi, p, i: (bi, p, 0, 0))
    in_specs = [qspec, kvspec, kvspec, cspec]
    args = [q, k_new, v_new, pair_rows(c_new)]
    n_past = 0
    if past is not None:
        assert nq == 1 and tq == rs, "a cached prefix is supported for a single query block only"
        kp, vp, cp = past
        plen = kp.shape[1]
        n_past = plen // tkp
        assert n_past * tkp == plen
        pspec = pl.BlockSpec((None, plen, LANES), lambda bi, p, i: (bi, 0, p))
        in_specs += [pspec, pspec,
                     pl.BlockSpec((None, None, 2, plen), lambda bi, p, i: (bi, p, 0, 0))]
        args += [kp, vp, pair_rows(cp)]
    return pl.pallas_call(
        functools.partial(_fox_kernel, tq=tq, rs=rs, tkp=tkp, n_past=n_past),
        out_shape=jax.ShapeDtypeStruct((b, t, w), BF16),
        grid=(b, n_pairs, nq),
        in_specs=in_specs,
        out_specs=qspec,
        scratch_shapes=[pltpu.VMEM((tq, LANES), F32), pltpu.VMEM((tq, LANES), F32),
                        pltpu.VMEM((tq, 1), F32), pltpu.VMEM((tq, 1), F32)],
        compiler_params=_cparams(3),
        name="fox_attn",
    )(*args)


def _diff_kernel(*refs, tq, rs, tkp, n_past, lambda_init):
    if n_past:
        (q_ref, kn_ref, vn_ref, bprev_ref, bdiag_ref, lam_ref, g_ref, kp_ref, vp_ref,
         o_ref, acc0, acc1, m0, m1, l0, l1) = refs
    else:
        (q_ref, kn_ref, vn_ref, bprev_ref, bdiag_ref, lam_ref, g_ref,
         o_ref, acc0, acc1, m0, m1, l0, l1) = refs
        kp_ref = vp_ref = None
    i = pl.program_id(2)
    n_qs = tq // rs
    lo = lax.broadcasted_iota(I32, (1, LANES), 1) < HEAD_DIM

    q = q_ref[...]
    zero = jnp.zeros_like(q)
    q_maps = (jnp.where(lo, q, zero), jnp.where(lo, zero, q))
    accs, ms, ls = (acc0, acc1), (m0, m1), (l0, l1)
    for acc, m, l in zip(accs, ms, ls):
        acc[...] = jnp.zeros_like(acc)
        m[...] = jnp.full_like(m, NEG_INF)
        l[...] = jnp.zeros_like(l)
    visible = (lax.broadcasted_iota(I32, (rs, rs), 1) // CHUNK
               <= lax.broadcasted_iota(I32, (rs, rs), 0) // CHUNK)

    def chunk(k, v, units):
        k = k.astype(BF16)
        v = v.astype(BF16)

        def scores(unit):
            mp, qs, _ = unit
            return _nt_dot(q_maps[mp][qs * rs:(qs + 1) * rs], k)

        def consume(unit, s):
            mp, qs, kind = unit
            rows = pl.ds(qs * rs, rs)
            if kind == "prev":
                s = s + bprev_ref[...]
            elif kind == "diag":
                s = jnp.where(visible, s + bdiag_ref[...], NEG_INF)
            m_old = ms[mp][rows, :]
            m_new = jnp.maximum(m_old, jnp.max(s, axis=1, keepdims=True))
            p = jnp.exp2(s - m_new)
            alpha = jnp.exp2(m_old - m_new)
            ls[mp][rows, :] = alpha * ls[mp][rows, :] + jnp.sum(p, axis=1, keepdims=True)
            accs[mp][rows, :] = alpha * accs[mp][rows, :] + jnp.dot(
                p.astype(BF16), v, preferred_element_type=F32)
            ms[mp][rows, :] = m_new

        _pipelined(units, scores, consume)

    def units_for(kind_of):
        return [(mp, qs, kind_of(qs)) for qs in range(n_qs) for mp in range(2)
                if kind_of(qs) is not None]

    far = units_for(lambda qs: "far")
    if n_past:
        def past_body(j, carry):
            off = pl.multiple_of(j * tkp, tkp)
            chunk(kp_ref[pl.ds(off, tkp), :], vp_ref[pl.ds(off, tkp), :], far)
            return carry
        lax.fori_loop(0, n_past - 1, past_body, 0)
        off = (n_past - 1) * tkp
        chunk(kp_ref[pl.ds(off, tkp), :], vp_ref[pl.ds(off, tkp), :], units_for(lambda qs: "prev"))
        chunk(kn_ref[...], vn_ref[...], units_for(lambda qs: "diag"))
    else:
        def new_body(j, carry):
            off = pl.multiple_of(j * rs, rs)
            chunk(kn_ref[pl.ds(off, rs), :], vn_ref[pl.ds(off, rs), :], far)
            return carry
        lax.fori_loop(0, jnp.maximum(n_qs * i - 1, 0), new_body, 0)

        @pl.when(i >= 1)
        def _():
            off = pl.multiple_of((n_qs * i - 1) * rs, rs)
            chunk(kn_ref[pl.ds(off, rs), :], vn_ref[pl.ds(off, rs), :],
                  units_for(lambda qs: "prev" if qs == 0 else "far"))

        for d in range(n_qs):
            off = pl.multiple_of((n_qs * i + d) * rs, rs)
            kinds = {d: "diag", d + 1: "prev"}
            chunk(kn_ref[pl.ds(off, rs), :], vn_ref[pl.ds(off, rs), :],
                  units_for(lambda qs: kinds.get(qs, "far") if qs >= d else None))

    dl = lam_ref[...]
    lam = (jnp.exp(jnp.sum(dl[0:1] * dl[1:2], axis=1, keepdims=True))
           - jnp.exp(jnp.sum(dl[2:3] * dl[3:4], axis=1, keepdims=True)) + lambda_init)
    o = acc0[...] / l0[...] - lam * (acc1[...] / l1[...])
    o = o * _rms_scale(o, SUBLN_EPS) * g_ref[...] * (1.0 - lambda_init)
    o_ref[...] = o.astype(o_ref.dtype)


def _diff_attention(q, k_new, v_new, bias_prev, bias_diag, diff_lambda, subln_g, past, *,
                    tq, rs, tkp, lambda_init):
    b, t, w = q.shape
    n_heads = w // LANES
    nq = t // tq
    qspec = pl.BlockSpec((None, tq, LANES), lambda bi, h, i: (bi, i, h))
    kvspec = pl.BlockSpec((None, t, LANES), lambda bi, h, i: (bi, 0, h))
    head_tile = lambda arr: pl.BlockSpec((None,) + arr.shape[1:], lambda bi, h, i: (h, 0, 0))
    const2 = lambda arr: pl.BlockSpec(arr.shape, lambda bi, h, i: (0, 0))
    in_specs = [qspec, kvspec, kvspec, head_tile(bias_prev), head_tile(bias_diag),
                const2(diff_lambda), const2(subln_g)]
    args = [q, k_new, v_new, bias_prev, bias_diag, diff_lambda, subln_g]
    n_past = 0
    if past is not None:
        assert nq == 1 and tq == rs, "a cached prefix is supported for a single query block only"
        kp, vp = past
        plen = kp.shape[1]
        n_past = plen // tkp
        assert n_past * tkp == plen and tkp >= MAX_DISTANCE
        pspec = pl.BlockSpec((None, plen, LANES), lambda bi, h, i: (bi, 0, h))
        in_specs += [pspec, pspec]
        args += [kp, vp]
    else:
        assert rs >= MAX_DISTANCE
    return pl.pallas_call(
        functools.partial(_diff_kernel, tq=tq, rs=rs, tkp=tkp, n_past=n_past,
                          lambda_init=lambda_init),
        out_shape=jax.ShapeDtypeStruct((b, t, w), BF16),
        grid=(b, n_heads, nq),
        in_specs=in_specs,
        out_specs=qspec,
        scratch_shapes=[pltpu.VMEM((tq, LANES), F32), pltpu.VMEM((tq, LANES), F32)]
        + [pltpu.VMEM((tq, 1), F32)] * 4,
        compiler_params=_cparams(3),
        name="diff_attn",
    )(*args)


def _memo(fn):
    cache = {}

    def get(key):
        if key not in cache:
            cache[key] = fn(key)
        return cache[key]
    return get


def _fox_prompt_kernel(q_ref, k_ref, kx_ref, vt_ref, o_ref, acc_a, acc_b, m_a, m_b, *, tq, rs):
    i = pl.program_id(2)
    n_qs = tq // rs
    lane = lax.broadcasted_iota(I32, (1, LANES), 1)
    lo = lane < HEAD_DIM
    top = lax.broadcasted_iota(I32, (LANES, rs), 0) < HEAD_DIM

    q = q_ref[...].astype(F32)
    ones_a = jnp.where(lane < HEAD_DIM + N_DECAY_TERMS, 1.0, 0.0)
    ones_b = jnp.where(lane < N_DECAY_TERMS, 1.0, 0.0)
    q_heads = (jnp.where(lo, q, ones_a).astype(BF16), jnp.where(lo, ones_b, q).astype(BF16))
    accs = (acc_a, acc_b)
    ms = (m_a, m_b)
    for acc, m in zip(accs, ms):
        acc[...] = jnp.zeros_like(acc)
        m[...] = jnp.full_like(m, NEG_INF)

    def chunk_group(first, units):
        @_memo
        def operands(kc):
            off = pl.multiple_of((first + kc) * rs, rs)
            k = k_ref[pl.ds(off, rs), :]
            kx = kx_ref[pl.ds(off, rs), :]
            vt = vt_ref[:, pl.ds(off, rs)]
            ones_v = jnp.ones_like(vt)
            return ((jnp.where(lo, k, kx), jnp.where(lo, kx, k)),
                    (jnp.where(top, vt, ones_v), jnp.where(top, ones_v, vt)))

        def scores(unit):
            kc, hd, qs, _ = unit
            return _nt_dot(operands(kc)[0][hd], q_heads[hd][qs * rs:(qs + 1) * rs])

        def consume(unit, s):
            kc, hd, qs, mask = unit
            cols = pl.ds(qs * rs, rs)
            if mask is not None:
                s = jnp.where(mask, s, NEG_INF)
            m_old = ms[hd][:, cols]
            m_new = jnp.maximum(m_old, jnp.max(s, axis=0, keepdims=True))
            p = jnp.exp2(s - m_new).astype(BF16)
            accs[hd][:, cols] = jnp.exp2(m_old - m_new) * accs[hd][:, cols] + jnp.dot(
                operands(kc)[1][hd], p, preferred_element_type=F32)
            ms[hd][:, cols] = m_new

        _pipelined(units, scores, consume)

    def below(j, carry):
        chunk_group(n_qs * j, [(kc, hd, qs, None) for kc in range(n_qs) for qs in range(n_qs)
                               for hd in range(2)])
        return carry
    lax.fori_loop(0, i, below, 0)
    causal = (lax.broadcasted_iota(I32, (rs, rs), 0) <= lax.broadcasted_iota(I32, (rs, rs), 1))
    chunk_group(n_qs * i, [(kc, hd, qs, causal if qs == kc else None) for kc in range(n_qs)
                           for qs in range(kc, n_qs) for hd in range(2)])

    a = acc_a[...]
    b = acc_b[...]
    out_t = jnp.concatenate([a[:HEAD_DIM] / a[HEAD_DIM:HEAD_DIM + 1], b[HEAD_DIM:] / b[0:1]],
                            axis=0)
    o_ref[...] = out_t.T.astype(o_ref.dtype)


def _fox_prompt_attention(q, k, kx, vt, *, tq, rs):
    b, t, w = q.shape
    qspec = pl.BlockSpec((None, tq, LANES), lambda bi, p, i: (bi, i, p))
    kspec = pl.BlockSpec((None, t, LANES), lambda bi, p, i: (bi, 0, p))
    return pl.pallas_call(
        functools.partial(_fox_prompt_kernel, tq=tq, rs=rs),
        out_shape=jax.ShapeDtypeStruct((b, t, w), BF16),
        grid=(b, w // LANES, t // tq),
        in_specs=[qspec, kspec, kspec, pl.BlockSpec((LANES, t), lambda bi, p, i: (p, bi))],
        out_specs=qspec,
        scratch_shapes=[pltpu.VMEM((LANES, tq), F32), pltpu.VMEM((LANES, tq), F32),
                        pltpu.VMEM((1, tq), F32), pltpu.VMEM((1, tq), F32)],
        compiler_params=_cparams(3),
        name="fox_prompt_attn",
    )(q, k, kx, vt)


ONES_ROWS = 16


def _diff_prompt_kernel(q_ref, k_ref, vt_ref, bprev_ref, bdiag_ref, lam_ref, g_ref, o_ref,
                        acc0, acc1, m0, m1, *, tq, rs, lambda_init):
    i = pl.program_id(2)
    n_qs = tq // rs
    lo = lax.broadcasted_iota(I32, (1, LANES), 1) < HEAD_DIM

    q = q_ref[...]
    zero = jnp.zeros_like(q)
    q_maps = (jnp.where(lo, q, zero), jnp.where(lo, zero, q))
    accs, ms = (acc0, acc1), (m0, m1)
    for acc, m in zip(accs, ms):
        acc[...] = jnp.zeros_like(acc)
        m[...] = jnp.full_like(m, NEG_INF)
    visible = (lax.broadcasted_iota(I32, (rs, rs), 0) // CHUNK
               <= lax.broadcasted_iota(I32, (rs, rs), 1) // CHUNK)

    def chunk_group(first, blocks_ahead):
        @_memo
        def operands(kc):
            off = pl.multiple_of((first + kc) * rs, rs)
            return (k_ref[pl.ds(off, rs), :],
                    jnp.concatenate([vt_ref[:, pl.ds(off, rs)], jnp.ones((ONES_ROWS, rs), BF16)],
                                    axis=0))

        def scores(unit):
            kc, mp, qs = unit
            return _nt_dot(operands(kc)[0], q_maps[mp][qs * rs:(qs + 1) * rs])

        def consume(unit, s):
            kc, mp, qs = unit
            cols = pl.ds(qs * rs, rs)
            distance = blocks_ahead + qs - kc
            if distance == 1:
                s = s + bprev_ref[...]
            elif distance == 0:
                s = jnp.where(visible, s + bdiag_ref[...], NEG_INF)
            m_old = ms[mp][:, cols]
            m_new = jnp.maximum(m_old, jnp.max(s, axis=0, keepdims=True))
            p = jnp.exp2(s - m_new).astype(BF16)
            accs[mp][:, cols] = jnp.exp2(m_old - m_new) * accs[mp][:, cols] + jnp.dot(
                operands(kc)[1], p, preferred_element_type=F32)
            ms[mp][:, cols] = m_new

        _pipelined([(kc, mp, qs) for kc in range(n_qs) for qs in range(n_qs) for mp in range(2)
                    if blocks_ahead + qs - kc >= 0], scores, consume)

    def far_below(j, carry):
        chunk_group(n_qs * j, 2 * n_qs)
        return carry
    lax.fori_loop(0, jnp.maximum(i - 1, 0), far_below, 0)

    @pl.when(i >= 1)
    def _():
        chunk_group(n_qs * (i - 1), n_qs)

    chunk_group(n_qs * i, 0)

    dl = lam_ref[...]
    lam = (jnp.exp(jnp.sum(dl[0:1] * dl[1:2], axis=1, keepdims=True))
           - jnp.exp(jnp.sum(dl[2:3] * dl[3:4], axis=1, keepdims=True)) + lambda_init)
    a0 = acc0[...]
    a1 = acc1[...]
    o = a0[:LANES] / a0[LANES:LANES + 1] - lam * (a1[:LANES] / a1[LANES:LANES + 1])
    scale = lax.rsqrt(jnp.mean(o * o, axis=0, keepdims=True) + SUBLN_EPS)
    o = o * scale * g_ref[...] * (1.0 - lambda_init)
    o_ref[...] = o.T.astype(o_ref.dtype)


def _diff_prompt_attention(q, k, vt, bias_prev, bias_diag, diff_lambda, subln_g, *,
                           tq, rs, lambda_init):
    b, t, w = q.shape
    assert rs >= MAX_DISTANCE
    qspec = pl.BlockSpec((None, tq, LANES), lambda bi, h, i: (bi, i, h))
    head_tile = lambda arr: pl.BlockSpec((None,) + arr.shape[1:], lambda bi, h, i: (h, 0, 0))
    const2 = lambda arr: pl.BlockSpec(arr.shape, lambda bi, h, i: (0, 0))
    return pl.pallas_call(
        functools.partial(_diff_prompt_kernel, tq=tq, rs=rs, lambda_init=lambda_init),
        out_shape=jax.ShapeDtypeStruct((b, t, w), BF16),
        grid=(b, w // LANES, t // tq),
        in_specs=[qspec, pl.BlockSpec((None, t, LANES), lambda bi, h, i: (bi, 0, h)),
                  pl.BlockSpec((LANES, t), lambda bi, h, i: (h, bi)),
                  head_tile(bias_prev), head_tile(bias_diag), const2(diff_lambda),
                  const2(subln_g)],
        out_specs=qspec,
        scratch_shapes=[pltpu.VMEM((LANES + ONES_ROWS, tq), F32)] * 2
        + [pltpu.VMEM((1, tq), F32)] * 2,
        compiler_params=_cparams(3),
        name="diff_prompt_attn",
    )(q, k, vt, bias_prev, bias_diag, diff_lambda, subln_g)


def _t5_bucket(rel):
    nb = NUM_BUCKETS // 2
    max_exact = nb // 2
    ret = jnp.where(rel > 0, nb, 0)
    n = jnp.abs(rel)
    nf = jnp.maximum(n, max_exact).astype(F32)
    large = max_exact + (jnp.log(nf / max_exact) / math.log(MAX_DISTANCE / max_exact)
                         * (nb - max_exact)).astype(I32)
    large = jnp.minimum(large, nb - 1)
    return ret + jnp.where(n < max_exact, n, large)


def _bias_tiles(rel_table, tq, tk_prev):
    buckets = jnp.arange(NUM_BUCKETS, dtype=I32)

    def lookup(rel):
        hit = _t5_bucket(rel)[None, :] == buckets[:, None]
        return jnp.sum(jnp.where(hit[:, None, :], rel_table[:, :, None], 0.0), axis=0)

    far = lookup(jnp.full((1,), -4 * MAX_DISTANCE, I32))

    def tile(rel0, tk):
        period = tq + tk
        j = jnp.arange(period, dtype=I32)
        strip = lookup(rel0 + jnp.where(j < tk, j, j - period)) - far
        flat = jnp.tile(strip, (1, tq))[:, :tq * (period - 1)]
        return (flat.reshape(-1, tq, period - 1)[:, :, :tk] * LOG2E).astype(F32)

    return tile(-tk_prev, tk_prev), tile(0, tq)


def _router_kernel(x_ref, of_ref, od_ref, wof_ref, wod_ref, g_ref, wr_ref, br_ref,
                   x1_ref, hn_ref, te_ref, tw_ref, rank_ref, cnt_ref, *, tm):
    step = pl.program_id(0)

    @pl.when(step == 0)
    def _():
        cnt_ref[...] = jnp.zeros_like(cnt_ref)

    x1 = (x_ref[...] + jnp.dot(of_ref[...], wof_ref[...], preferred_element_type=F32)
          + jnp.dot(od_ref[...], wod_ref[...], preferred_element_type=F32))
    x1_ref[...] = x1
    hn = x1 * _rms_scale(x1, NORM_EPS) * g_ref[...]
    half = hn.shape[1] // 2
    hn_ref[...] = _pack_bf16_pairs(hn[:, :half], hn[:, half:])

    logits = jnp.dot(hn, wr_ref[...], preferred_element_type=F32,
                     precision=lax.Precision.HIGHEST) + br_ref[...]
    lane = lax.broadcasted_iota(I32, logits.shape, 1)
    lane_f = lane.astype(F32)
    sel = jnp.zeros(logits.shape, F32)
    vals, idxs = [], []
    for _ in range(TOP_K):
        mx = jnp.max(logits, axis=1, keepdims=True)
        idx = jnp.min(jnp.where(logits == mx, lane_f, float(LANES)), axis=1, keepdims=True)
        hit = lane_f == idx
        sel = jnp.where(hit, 1.0, sel)
        logits = jnp.where(hit, -jnp.inf, logits)
        vals.append(mx)
        idxs.append(idx)
    exps = [jnp.exp(v - vals[0]) for v in vals]
    denom = exps[0] + exps[1] + exps[2] + exps[3]

    earlier = (lax.broadcasted_iota(I32, (tm, tm), 1)
               < lax.broadcasted_iota(I32, (tm, tm), 0)).astype(BF16)
    before = cnt_ref[...] + jnp.dot(earlier, sel.astype(BF16), preferred_element_type=F32)
    for k in range(TOP_K):
        te_ref[:, k:k + 1] = idxs[k].astype(I32)
        tw_ref[:, k:k + 1] = exps[k] / denom
        rank_ref[:, k:k + 1] = jnp.sum(
            jnp.where(lane_f == idxs[k], before, 0.0), axis=1, keepdims=True).astype(I32)
    cnt_ref[...] += jnp.sum(sel, axis=0, keepdims=True)


def _router(x2d, o_fox, o_diff, wo_f, wo_d, g_ffn, w_r, b_r, *, tm):
    n, d = x2d.shape
    row = lambda i: (i, 0)
    const = lambda i: (0, 0)
    full = lambda arr: pl.BlockSpec(arr.shape, const)
    return pl.pallas_call(
        functools.partial(_router_kernel, tm=tm),
        out_shape=[jax.ShapeDtypeStruct((n, d), F32), jax.ShapeDtypeStruct((n, d // 2), U32),
                   jax.ShapeDtypeStruct((n, TOP_K), I32), jax.ShapeDtypeStruct((n, TOP_K), F32),
                   jax.ShapeDtypeStruct((n, TOP_K), I32), jax.ShapeDtypeStruct((1, LANES), F32)],
        grid=(n // tm,),
        in_specs=[pl.BlockSpec((tm, d), row), pl.BlockSpec((tm, o_fox.shape[1]), row),
                  pl.BlockSpec((tm, o_diff.shape[1]), row), full(wo_f), full(wo_d),
                  full(g_ffn), full(w_r), full(b_r)],
        out_specs=[pl.BlockSpec((tm, d), row), pl.BlockSpec((tm, d // 2), row),
                   pl.BlockSpec((tm, TOP_K), row), pl.BlockSpec((tm, TOP_K), row),
                   pl.BlockSpec((tm, TOP_K), row), pl.BlockSpec((1, LANES), const)],
        compiler_params=_cparams(1),
        name="router",
    )(x2d, o_fox, o_diff, wo_f, wo_d, g_ffn, w_r, b_r)


def _row_copy(src, src_row, dst, dst_row, sem):
    return pltpu.make_async_copy(src.at[pl.ds(src_row, 1), :], dst.at[pl.ds(dst_row, 1), :], sem)


def _dispatch_kernel(tail_start_ref, tail_valid_ref, pos_ref, hn_ref, xs_hbm, zeros_ref, sem,
                     *, tm, tme, n_experts):
    def tail_copy(e):
        return pltpu.make_async_copy(
            zeros_ref, xs_hbm.at[pl.ds(pl.multiple_of(tail_start_ref[e], tme), tme), :], sem.at[0])

    @pl.when(pl.program_id(0) == 0)
    def _():
        zeros_ref[...] = jnp.zeros_like(zeros_ref)
        for e in range(n_experts):
            @pl.when(tail_valid_ref[e] > 0)
            def _():
                tail_copy(e).start()
        for e in range(n_experts):
            @pl.when(tail_valid_ref[e] > 0)
            def _():
                tail_copy(e).wait()

    def start_row(r, carry):
        for k in range(TOP_K):
            _row_copy(hn_ref, r, xs_hbm, pos_ref[r, k], sem.at[1]).start(priority=k % 2)
        return carry

    def wait_row(r, carry):
        for k in range(TOP_K):
            _row_copy(hn_ref, r, xs_hbm, pos_ref[r, k], sem.at[1]).wait()
        return carry

    lax.fori_loop(0, tm, start_row, 0, unroll=8)
    lax.fori_loop(0, tm, wait_row, 0, unroll=8)


def _dispatch(hn_packed, pos, tail_start, tail_valid, *, m_pad, tm, tme):
    n, half = hn_packed.shape
    n_experts = tail_start.shape[0]
    grid_spec = pltpu.PrefetchScalarGridSpec(
        num_scalar_prefetch=2,
        grid=(n // tm,),
        in_specs=[pl.BlockSpec((tm, TOP_K), lambda i, *_: (i, 0), memory_space=pltpu.SMEM),
                  pl.BlockSpec((tm, half), lambda i, *_: (i, 0))],
        out_specs=pl.BlockSpec(memory_space=pl.ANY),
        scratch_shapes=[pltpu.VMEM((tme, half), U32), pltpu.SemaphoreType.DMA((2,))],
    )
    return pl.pallas_call(
        functools.partial(_dispatch_kernel, tm=tm, tme=tme, n_experts=n_experts),
        out_shape=jax.ShapeDtypeStruct((m_pad, half), U32),
        grid_spec=grid_spec,
        compiler_params=pltpu.CompilerParams(
            dimension_semantics=("arbitrary",), vmem_limit_bytes=VMEM_LIMIT_BYTES,
            has_side_effects=True),
        name="dispatch",
    )(tail_start, tail_valid, pos, hn_packed)


def _experts_kernel(te_ref, nact_ref, xs_ref, wg_ref, bg_ref, wu_ref, bu_ref, wd_ref, bd_ref,
                    ys_ref):
    @pl.when(pl.program_id(0) < nact_ref[0])
    def _():
        half = xs_ref.shape[1]
        x_lo, x_hi = _unpack_bf16_pairs(xs_ref[...])
        x_lo = x_lo.astype(BF16)
        x_hi = x_hi.astype(BF16)

        def proj(w_ref, b_ref):
            return (jnp.dot(x_lo, w_ref[:half, :], preferred_element_type=F32)
                    + jnp.dot(x_hi, w_ref[half:, :], preferred_element_type=F32) + b_ref[...])

        g = jnp.minimum(proj(wg_ref, bg_ref), SWIGLU_LIMIT)
        u = jnp.clip(proj(wu_ref, bu_ref), -SWIGLU_LIMIT, SWIGLU_LIMIT)
        a = ((u + 1.0) * (g * jax.nn.sigmoid(SWIGLU_ALPHA * g))).astype(BF16)
        y = jnp.dot(a, wd_ref[...], preferred_element_type=F32) + bd_ref[...]
        ys_ref[...] = _pack_bf16_pairs(y[:, :half], y[:, half:])


def _experts(xs, tile_expert, n_active, w_gate, b_gate, w_up, b_up, w_down, b_down, *, tme):
    m_pad, half = xs.shape
    n_tiles = m_pad // tme
    d, d_ff = w_gate.shape[1:]
    row = lambda t, te, na: (jnp.minimum(t, na[0] - 1), 0)
    wspec = lambda shape: pl.BlockSpec((None,) + shape, lambda t, te, na: (te[t], 0, 0))
    grid_spec = pltpu.PrefetchScalarGridSpec(
        num_scalar_prefetch=2,
        grid=(n_tiles,),
        in_specs=[pl.BlockSpec((tme, half), row),
                  wspec((d, d_ff)), wspec((1, d_ff)), wspec((d, d_ff)), wspec((1, d_ff)),
                  wspec((d_ff, d)), wspec((1, d))],
        out_specs=pl.BlockSpec((tme, half), row),
    )
    return pl.pallas_call(
        _experts_kernel,
        out_shape=jax.ShapeDtypeStruct((m_pad, half), U32),
        grid_spec=grid_spec,
        compiler_params=_cparams(1),
        name="experts",
    )(tile_expert, n_active, xs, w_gate, b_gate, w_up, b_up, w_down, b_down)


def _combine_kernel(pos_ref, x1_ref, tw_ref, p_ref, wg_ref, wp_ref, gple_ref, gfin_ref, ys_hbm,
                    y_ref, gbuf, sem, *, tm):
    def start_row(r, carry):
        for k in range(TOP_K):
            _row_copy(ys_hbm, pos_ref[r, k], gbuf.at[k], r, sem).start(priority=k % 2)
        return carry

    def wait_row(r, carry):
        for k in range(TOP_K):
            _row_copy(ys_hbm, pos_ref[r, k], gbuf.at[k], r, sem).wait()
        return carry

    lax.fori_loop(0, tm, start_row, 0, unroll=8)
    lax.fori_loop(0, tm, wait_row, 0, unroll=8)

    x1 = x1_ref[...]
    half = x1.shape[1] // 2
    tw = tw_ref[...]
    moe_lo = jnp.zeros((tm, half), F32)
    moe_hi = jnp.zeros((tm, half), F32)
    for k in range(TOP_K):
        lo, hi = _unpack_bf16_pairs(gbuf[k])
        moe_lo = moe_lo + tw[:, k:k + 1] * lo
        moe_hi = moe_hi + tw[:, k:k + 1] * hi
    x2 = x1 + jnp.concatenate([moe_lo, moe_hi], axis=1)
    hp = (x2 * _rms_scale(x2, NORM_EPS) * gple_ref[...]).astype(BF16)
    gate = jax.nn.sigmoid(jnp.dot(hp, wg_ref[...], preferred_element_type=F32))
    proj = jnp.dot(p_ref[...].astype(BF16), wp_ref[...], preferred_element_type=F32)
    x3 = x2 + proj * gate
    y_ref[...] = x3 * _rms_scale(x3, NORM_EPS) * gfin_ref[...]


def _combine(x1, top_w, pos, ys, p2d, w_ple_gate, w_ple_proj, g_ple, g_final, *, tm):
    n, d = x1.shape
    half = ys.shape[1]
    row = lambda i: (i, 0)
    const = lambda i: (0, 0)
    full = lambda arr: pl.BlockSpec(arr.shape, const)
    return pl.pallas_call(
        functools.partial(_combine_kernel, tm=tm),
        out_shape=jax.ShapeDtypeStruct((n, d), F32),
        grid=(n // tm,),
        in_specs=[pl.BlockSpec((tm, TOP_K), row, memory_space=pltpu.SMEM),
                  pl.BlockSpec((tm, d), row), pl.BlockSpec((tm, TOP_K), row),
                  pl.BlockSpec((tm, p2d.shape[1]), row), full(w_ple_gate), full(w_ple_proj),
                  full(g_ple), full(g_final), pl.BlockSpec(memory_space=pl.ANY)],
        out_specs=pl.BlockSpec((tm, d), row),
        scratch_shapes=[pltpu.VMEM((TOP_K, tm, half), U32), pltpu.SemaphoreType.DMA(())],
        compiler_params=_cparams(1),
        name="combine",
    )(pos, x1, top_w, p2d, w_ple_gate, w_ple_proj, g_ple, g_final, ys)


def _pick_tile(n, want):
    t = min(n, want)
    assert n % t == 0, (n, t)
    return t


def _layer_group(x, p, past, wts, lambda_init, g_final):
    (g_mix, w_main, w_vt, w_f, b_f, diff_lambda, subln_g, rel_table, wo_f, wo_d, g_ffn, w_r, b_r,
     w_gate, b_gate, w_up, b_up, w_down, b_down, g_ple, w_ple_gate, w_ple_proj) = wts
    b, t, d = x.shape
    n = b * t
    n_fox = (w_main.shape[1] // 6) // HEAD_DIM
    n_experts = w_gate.shape[0]
    x2d = x.reshape(n, d)

    tm = _pick_tile(n, 512)
    (qf, qd, kf, vf, kd, vd, kf16, kd16, vf16, vd16, logf_lanes) = _inproj(
        x2d, g_mix, w_main, w_vt, w_f, b_f, tm=tm, values_transposed=past is None)
    logf = logf_lanes[:, :n_fox]
    width = kf.shape[1]
    as_seq = lambda a: a.reshape(b, t, a.shape[-1])

    tq = _pick_tile(t, 1024)
    rs = _pick_tile(tq, 256)
    if past is None:
        kx = _key_decay(as_seq(logf_lanes), n_fox, width)
        o_fox = _fox_prompt_attention(as_seq(qf), as_seq(kf16), kx, vf16, tq=tq, rs=rs)
        bias_prev, bias_diag = _bias_tiles(rel_table, rs, rs)
        o_diff = _diff_prompt_attention(
            as_seq(qd), as_seq(kd16), vd16, jnp.swapaxes(bias_prev, 1, 2),
            jnp.swapaxes(bias_diag, 1, 2), diff_lambda, subln_g.reshape(-1, 1), tq=tq, rs=rs,
            lambda_init=lambda_init)
    else:
        pk_f, pv_f, plogf, pk_d, pv_d = past
        plen = pk_f.shape[1]
        tkp = 512
        total = plen + t
        padded = -(-total // LANES) * LANES
        seq = jnp.concatenate([jnp.transpose(plogf.astype(F32), (0, 2, 1)),
                               jnp.transpose(as_seq(logf), (0, 2, 1))], axis=2)
        c_all = _cumsum_lanes(jnp.pad(seq, ((0, 0), (0, 0), (0, padded - total))))
        fox_past = (pk_f.reshape(b, plen, width), pv_f.reshape(b, plen, width), c_all[:, :, :plen])
        diff_past = (pk_d.reshape(b, plen, width), pv_d.reshape(b, plen, width))
        o_fox = _fox_attention(as_seq(qf), as_seq(kf16), as_seq(vf16), c_all[:, :, plen:total],
                               fox_past, tq=tq, rs=rs, tkp=tkp)
        bias_prev, bias_diag = _bias_tiles(rel_table, rs, tkp)
        o_diff = _diff_attention(as_seq(qd), as_seq(kd16), as_seq(vd16), bias_prev, bias_diag,
                                 diff_lambda, subln_g, diff_past, tq=tq, rs=rs, tkp=tkp,
                                 lambda_init=lambda_init)

    tmr = _pick_tile(n, 256)
    x1, hn_packed, top_e, top_w, rank, counts = _router(
        x2d, o_fox.reshape(n, width), o_diff.reshape(n, width), wo_f, wo_d, g_ffn, w_r, b_r, tm=tmr)

    tme = 512 if n * TOP_K >= 64 * 512 else 256
    n_tiles = -(-(n * TOP_K) // tme) + n_experts
    m_pad = n_tiles * tme
    cnt = counts[0, :n_experts].astype(I32)
    group = -(-cnt // tme) * tme
    ends = jnp.cumsum(group)
    starts = ends - group
    experts = jnp.arange(n_experts, dtype=I32)
    pos = rank + jnp.sum(jnp.where(top_e[:, :, None] == experts, starts, 0), axis=2)
    n_active = (ends[-1] // tme).astype(I32)
    tile_ids = jnp.arange(n_tiles, dtype=I32)
    tile_expert = jnp.sum(jnp.minimum(tile_ids, n_active - 1)[:, None] >= (ends // tme)[None, :],
                          axis=1).astype(I32)
    tile_expert = jnp.minimum(tile_expert, n_experts - 1)
    tail_start = jnp.maximum(ends - tme, 0).astype(I32)
    tail_valid = (group > 0).astype(I32)

    tmd = _pick_tile(n, 256)
    xs = _dispatch(hn_packed, pos, tail_start, tail_valid, m_pad=m_pad, tm=tmd, tme=tme)
    ys = _experts(xs, tile_expert, n_active.reshape(1), w_gate, b_gate, w_up, b_up, w_down, b_down,
                  tme=tme)
    y = _combine(x1, top_w, pos, ys, p.reshape(n, p.shape[-1]), w_ple_gate, w_ple_proj, g_ple,
                 g_final, tm=tmd)
    return y.reshape(b, t, d), (kf, vf, logf, kd, vd)


def kernel(x_prompt, x_sample, p_prompt, p_sample, cache_fox_k, cache_fox_v, cache_fox_logf, cache_diff_k, cache_diff_v, g_mix, w_in, b_forget, diff_lambda, diff_subln_g, rel_bias_table, w_o, g_ffn, w_router, b_router, w_gate, b_gate, w_up, b_up, w_down, b_down, g_ple, w_ple_gate, w_ple_proj, g_final):
    depth = g_mix.shape[0]
    assert depth == 1, "the final norm is fused into the last layer; one layer supported"
    d = x_prompt.shape[-1]
    n_fox = cache_fox_k.shape[3]
    fox_w = n_fox * cache_fox_k.shape[4]
    n_diff = cache_diff_k.shape[3]
    diff_w = n_diff * 2 * cache_diff_k.shape[5]
    assert fox_w == diff_w and fox_w % LANES == 0
    n_experts = w_router.shape[-1]
    assert TOP_K <= n_experts <= LANES
    row2 = lambda a: a.reshape(1, -1).astype(F32)

    i = 0
    lambda_init = 0.8 - 0.6 * math.exp(-0.3 * i)
    off_fk, off_fv, off_ff = fox_w, 2 * fox_w, 3 * fox_w
    off_dq = off_ff + n_fox
    off_dk, off_dv = off_dq + diff_w, off_dq + 2 * diff_w
    w = w_in[i]
    cols = lambda o, wd: w[:, o:o + wd]
    w_main = jnp.concatenate([cols(0, fox_w), cols(off_dq, diff_w), cols(off_fk, fox_w),
                              cols(off_fv, fox_w), cols(off_dk, diff_w), cols(off_dv, diff_w)],
                             axis=1).astype(BF16)
    w_vt = jnp.concatenate([cols(off_fv, fox_w), cols(off_dv, diff_w)], axis=1).T.astype(BF16)
    w_f = jnp.pad(cols(off_ff, n_fox), ((0, 0), (0, LANES - n_fox))).astype(BF16)
    b_f = jnp.pad(row2(b_forget[i]), ((0, 0), (0, LANES - n_fox)))
    w_r = jnp.pad(w_router[i].astype(F32), ((0, 0), (0, LANES - n_experts)))
    b_r = jnp.pad(row2(b_router[i]), ((0, 0), (0, LANES - n_experts)), constant_values=NEG_INF)
    wo = w_o[i].astype(BF16)
    bias3 = lambda a: a.reshape(n_experts, 1, -1).astype(F32)
    wts = (row2(g_mix[i]), w_main, w_vt, w_f, b_f, diff_lambda[i].astype(F32), row2(diff_subln_g[i]),
           rel_bias_table.astype(F32), wo[:fox_w], wo[fox_w:], row2(g_ffn[i]), w_r, b_r,
           w_gate[i].astype(BF16), bias3(b_gate[i]), w_up[i].astype(BF16), bias3(b_up[i]),
           w_down[i].astype(BF16), bias3(b_down[i]), row2(g_ple[i]),
           w_ple_gate[i].astype(BF16), w_ple_proj[i].astype(BF16))
    gf = row2(g_final)

    yp, new_p = _layer_group(x_prompt, p_prompt[i], None, wts, lambda_init, gf)
    past_s = (cache_fox_k[i], cache_fox_v[i], cache_fox_logf[i], cache_diff_k[i], cache_diff_v[i])
    ys, new_s = _layer_group(x_sample, p_sample[i], past_s, wts, lambda_init, gf)

    def rows(new, b, t):
        kf, vf, logf, kd, vd = new
        return (kf.reshape(1, b, t, n_fox, -1), vf.reshape(1, b, t, n_fox, -1),
                logf.reshape(1, b, t, n_fox), kd.reshape(1, b, t, n_diff, 2, -1),
                vd.reshape(1, b, t, n_diff, -1))

    bp, tp = x_prompt.shape[:2]
    bs, ts = x_sample.shape[:2]
    return (yp, ys) + rows(new_p, bp, tp) + rows(new_s, bs, ts)
```

```python
import functools
import math

import jax
import jax.numpy as jnp
import numpy as np
from jax import lax
from jax.experimental import pallas as pl
from jax.experimental.pallas import tpu as pltpu

HEAD_DIM = 64
CHUNK = 64
NUM_BUCKETS = 32
MAX_DISTANCE = 128
TOP_K = 4
SWIGLU_LIMIT = 7.0
SWIGLU_ALPHA = 1.702
NORM_EPS = 1e-6
SUBLN_EPS = 1e-5
NEG_INF = -1e30
LOG2E = math.log2(math.e)

LANES = 128
VMEM_LIMIT_BYTES = 56 * 1024 * 1024

F32 = jnp.float32
BF16 = jnp.bfloat16
I32 = jnp.int32
U32 = jnp.uint32


def _cparams(n_axes):
    return pltpu.CompilerParams(
        dimension_semantics=("arbitrary",) * n_axes, vmem_limit_bytes=VMEM_LIMIT_BYTES)


def _rms_scale(x, eps):
    return lax.rsqrt(jnp.mean(x * x, axis=-1, keepdims=True) + eps)


def _pack_bf16_pairs(lo, hi):
    lo_bits = pltpu.bitcast(lo.astype(BF16).astype(F32), U32)
    hi_bits = pltpu.bitcast(hi.astype(BF16).astype(F32), U32)
    return lo_bits | (hi_bits >> 16)


def _unpack_bf16_pairs(packed):
    lo = pltpu.bitcast(packed & jnp.uint32(0xFFFF0000), F32)
    hi = pltpu.bitcast(packed << 16, F32)
    return lo, hi


def _inproj_kernel(x_ref, g_ref, w_ref, wvt_ref, wf_ref, bf_ref,
                   qf_ref, qd_ref, kf_ref, vf_ref, kd_ref, vd_ref,
                   kf16_ref, kd16_ref, vf16_ref, vd16_ref, logf_ref, *, width, values_transposed):
    x = x_ref[...]
    h = (x * _rms_scale(x, NORM_EPS) * g_ref[...]).astype(BF16)
    q_scale = LOG2E / math.sqrt(HEAD_DIM)

    def proj(c):
        return jnp.dot(h, w_ref[:, c * width:(c + 1) * width], preferred_element_type=F32)

    qf_ref[...] = (proj(0) * q_scale).astype(BF16)
    qd_ref[...] = (proj(1) * q_scale).astype(BF16)
    for c, (o32, o16) in enumerate(((kf_ref, kf16_ref), (vf_ref, vf16_ref),
                                    (kd_ref, kd16_ref), (vd_ref, vd16_ref))):
        u = proj(2 + c)
        o32[...] = u
        if not (values_transposed and c in (1, 3)):
            o16[...] = u.astype(BF16)
    if values_transposed:
        vf16_ref[...] = _nt_dot(wvt_ref[:width, :], h).astype(BF16)
        vd16_ref[...] = _nt_dot(wvt_ref[width:, :], h).astype(BF16)
    f = jnp.dot(h, wf_ref[...], preferred_element_type=F32) + bf_ref[...]
    logf_ref[...] = jax.nn.log_sigmoid(f)


def _inproj(x2d, g_mix, w_main, w_vt, w_f, b_f, *, tm, values_transposed):
    n, d = x2d.shape
    width = w_main.shape[1] // 6
    row = lambda i: (i, 0)
    const = lambda i: (0, 0)
    if values_transposed:
        v16_shape, v16_spec = (width, n), pl.BlockSpec((width, tm), lambda i: (0, i))
    else:
        v16_shape, v16_spec = (n, width), pl.BlockSpec((tm, width), row)
    out_shape = ([jax.ShapeDtypeStruct((n, width), BF16)] * 2
                 + [jax.ShapeDtypeStruct((n, width), F32)] * 4
                 + [jax.ShapeDtypeStruct((n, width), BF16)] * 2
                 + [jax.ShapeDtypeStruct(v16_shape, BF16)] * 2
                 + [jax.ShapeDtypeStruct((n, LANES), F32)])
    out_specs = ([pl.BlockSpec((tm, width), row)] * 8 + [v16_spec] * 2
                 + [pl.BlockSpec((tm, LANES), row)])
    return pl.pallas_call(
        functools.partial(_inproj_kernel, width=width, values_transposed=values_transposed),
        out_shape=out_shape,
        grid=(n // tm,),
        in_specs=[pl.BlockSpec((tm, d), row), pl.BlockSpec((1, d), const),
                  pl.BlockSpec(w_main.shape, const), pl.BlockSpec(w_vt.shape, const),
                  pl.BlockSpec(w_f.shape, const), pl.BlockSpec(b_f.shape, const)],
        out_specs=out_specs,
        compiler_params=_cparams(1),
        name="inproj",
    )(x2d, g_mix, w_main, w_vt, w_f, b_f)


def _cumsum_kernel(x_ref, o_ref):
    x = x_ref[...]
    length = x.shape[1]
    lane = lax.broadcasted_iota(I32, x.shape, 1)
    shift = 1
    while shift < length:
        x = x + jnp.where(lane >= shift, pltpu.roll(x, shift, axis=1), 0.0)
        shift *= 2
    o_ref[...] = x


def _cumsum_lanes(x):
    b, h, length = x.shape
    spec = pl.BlockSpec((None, h, length), lambda i: (i, 0, 0))
    return pl.pallas_call(
        _cumsum_kernel, out_shape=jax.ShapeDtypeStruct(x.shape, F32), grid=(b,),
        in_specs=[spec], out_specs=spec, compiler_params=_cparams(1), name="cumsum",
    )(x)


N_DECAY_TERMS = 3


def _key_decay_kernel(x_ref, sel_ref, o_ref):
    x = x_ref[...]
    length = x.shape[0]
    row = lax.broadcasted_iota(I32, x.shape, 0)
    shift = 1
    while shift < length:
        x = x + jnp.where(row >= shift, pltpu.roll(x, shift, axis=0), 0.0)
        shift *= 2
    rest = x * (-LOG2E)
    out = jnp.zeros(o_ref.shape, F32)
    for j in range(N_DECAY_TERMS):
        term = rest.astype(BF16)
        rest = rest - term.astype(F32)
        out = out + jnp.dot(term, sel_ref[j], preferred_element_type=F32)
    o_ref[...] = out.astype(BF16)


def _decay_selectors(n_heads, width):
    sel = np.zeros((N_DECAY_TERMS, LANES, width), np.float32)
    for h in range(n_heads):
        base = (h // 2) * LANES + (HEAD_DIM if h % 2 == 0 else 0)
        for j in range(N_DECAY_TERMS):
            sel[j, h, base + j] = 1.0
    return jnp.asarray(sel, BF16)


def _key_decay(logf, n_heads, width):
    b, length, lanes = logf.shape
    sel = _decay_selectors(n_heads, width)
    return pl.pallas_call(
        _key_decay_kernel, out_shape=jax.ShapeDtypeStruct((b, length, width), BF16), grid=(b,),
        in_specs=[pl.BlockSpec((None, length, lanes), lambda i: (i, 0, 0)),
                  pl.BlockSpec(sel.shape, lambda i: (0, 0, 0))],
        out_specs=pl.BlockSpec((None, length, width), lambda i: (i, 0, 0)),
        compiler_params=_cparams(1), name="key_decay",
    )(logf, sel)


def _nt_dot(a, b):
    return lax.dot_general(a, b, (((1,), (1,)), ((), ())), preferred_element_type=F32)


SCORE_LOOKAHEAD = 4


def _pipelined(units, scores, consume):
    ahead = [scores(unit) for unit in units[:SCORE_LOOKAHEAD]]
    for n, unit in enumerate(units):
        s = ahead.pop(0)
        if n + SCORE_LOOKAHEAD < len(units):
            ahead.append(scores(units[n + SCORE_LOOKAHEAD]))
        consume(unit, s)


def _online_softmax_step(s, v, m_ref, l_ref, acc_ref):
    m_old = m_ref[...]
    m_new = jnp.maximum(m_old, jnp.max(s, axis=1, keepdims=True))
    p = jnp.exp2(s - m_new)
    alpha = jnp.exp2(m_old - m_new)
    l_ref[...] = alpha * l_ref[...] + jnp.sum(p, axis=1, keepdims=True)
    acc_ref[...] = alpha * acc_ref[...] + jnp.dot(p.astype(BF16), v, preferred_element_type=F32)
    m_ref[...] = m_new


def _fox_cached_kernel(q_ref, kn_ref, vn_ref, cn_ref, cp_ref, kp_ref, vp_ref,
                       o_ref, acc_a, acc_b, m_a, m_b, l_a, l_b, *, tkp, n_past):
    t = q_ref.shape[0]
    q = q_ref[...]
    kn = kn_ref[...]
    vn = vn_ref[...]
    causal = lax.broadcasted_iota(I32, (t, t), 1) <= lax.broadcasted_iota(I32, (t, t), 0)
    for hd, (acc, m, l) in enumerate(((acc_a, m_a, l_a), (acc_b, m_b, l_b))):
        lanes = slice(hd * HEAD_DIM, (hd + 1) * HEAD_DIM)
        q_h = q[:, lanes]
        acc[...] = jnp.zeros_like(acc)
        l[...] = jnp.zeros_like(l)
        m[...] = jnp.full_like(m, NEG_INF)

        def past_body(j, carry):
            off = pl.multiple_of(j * tkp, tkp)
            s = (_nt_dot(q_h, kp_ref[pl.ds(off, tkp), lanes].astype(BF16))
                 - cp_ref[hd:hd + 1, pl.ds(off, tkp)] * LOG2E)
            _online_softmax_step(s, vp_ref[pl.ds(off, tkp), lanes].astype(BF16), m, l, acc)
            return carry
        lax.fori_loop(0, n_past, past_body, 0)

        s = _nt_dot(q_h, kn[:, lanes]) - cn_ref[hd:hd + 1, :] * LOG2E
        _online_softmax_step(jnp.where(causal, s, NEG_INF), vn[:, lanes], m, l, acc)
        o_ref[:, lanes] = (acc[...] / l[...]).astype(o_ref.dtype)


def _fox_cached_attention(q, k_new, v_new, c_new, k_past, v_past, c_past, *, tkp):
    b, t, w = q.shape
    n_pairs = w // LANES
    plen = k_past.shape[1]
    n_past = plen // tkp
    assert n_past * tkp == plen
    pair_rows = lambda c: c.reshape(b, n_pairs, 2, c.shape[-1])
    seqspec = lambda n: pl.BlockSpec((None, n, LANES), lambda bi, p: (bi, 0, p))
    cspec = lambda n: pl.BlockSpec((None, None, 2, n), lambda bi, p: (bi, p, 0, 0))
    return pl.pallas_call(
        functools.partial(_fox_cached_kernel, tkp=tkp, n_past=n_past),
        out_shape=jax.ShapeDtypeStruct((b, t, w), BF16),
        grid=(b, n_pairs),
        in_specs=[seqspec(t), seqspec(t), seqspec(t), cspec(t), cspec(plen),
                  seqspec(plen), seqspec(plen)],
        out_specs=seqspec(t),
        scratch_shapes=[pltpu.VMEM((t, HEAD_DIM), F32)] * 2 + [pltpu.VMEM((t, 1), F32)] * 4,
        compiler_params=_cparams(2),
        name="fox_cached_attn",
    )(q, k_new, v_new, pair_rows(c_new), pair_rows(c_past), k_past, v_past)


def _diff_lambda(dl, lambda_init):
    return (jnp.exp(jnp.sum(dl[0:1] * dl[1:2], axis=1, keepdims=True))
            - jnp.exp(jnp.sum(dl[2:3] * dl[3:4], axis=1, keepdims=True)) + lambda_init)


def _diff_cached_kernel(q_ref, kn_ref, vn_ref, bprev_ref, bdiag_ref, lam_ref, g_ref,
                        kp_ref, vp_ref, o_ref, acc0, acc1, m0, m1, l0, l1,
                        *, tkp, n_past, lambda_init):
    q = q_ref[...]
    kn = kn_ref[...]
    vn = vn_ref[...]
    for mp, (acc, m, l) in enumerate(((acc0, m0, l0), (acc1, m1, l1))):
        lanes = slice(mp * HEAD_DIM, (mp + 1) * HEAD_DIM)
        q_m = q[:, lanes]
        acc[...] = jnp.zeros_like(acc)
        l[...] = jnp.zeros_like(l)
        m[...] = jnp.full_like(m, NEG_INF)

        def past_scores(off):
            return _nt_dot(q_m, kp_ref[pl.ds(off, tkp), lanes].astype(BF16))

        def past_body(j, carry):
            off = pl.multiple_of(j * tkp, tkp)
            _online_softmax_step(past_scores(off), vp_ref[pl.ds(off, tkp), :].astype(BF16),
                                 m, l, acc)
            return carry
        lax.fori_loop(0, n_past - 1, past_body, 0)
        off = (n_past - 1) * tkp
        _online_softmax_step(past_scores(off) + bprev_ref[...],
                             vp_ref[pl.ds(off, tkp), :].astype(BF16), m, l, acc)
        _online_softmax_step(_nt_dot(q_m, kn[:, lanes]) + bdiag_ref[...], vn, m, l, acc)

    lam = _diff_lambda(lam_ref[...], lambda_init)
    o = acc0[...] / l0[...] - lam * (acc1[...] / l1[...])
    o = o * _rms_scale(o, SUBLN_EPS) * g_ref[...] * (1.0 - lambda_init)
    o_ref[...] = o.astype(o_ref.dtype)


def _diff_cached_attention(q, k_new, v_new, bias_prev, bias_diag, diff_lambda, subln_g,
                           k_past, v_past, *, tkp, lambda_init):
    b, t, w = q.shape
    plen = k_past.shape[1]
    n_past = plen // tkp
    assert n_past * tkp == plen and tkp >= MAX_DISTANCE and t <= CHUNK and plen % CHUNK == 0
    seqspec = lambda n: pl.BlockSpec((None, n, LANES), lambda bi, h: (bi, 0, h))
    head_tile = lambda arr: pl.BlockSpec((None,) + arr.shape[1:], lambda bi, h: (h, 0, 0))
    const2 = lambda arr: pl.BlockSpec(arr.shape, lambda bi, h: (0, 0))
    return pl.pallas_call(
        functools.partial(_diff_cached_kernel, tkp=tkp, n_past=n_past, lambda_init=lambda_init),
        out_shape=jax.ShapeDtypeStruct((b, t, w), BF16),
        grid=(b, w // LANES),
        in_specs=[seqspec(t), seqspec(t), seqspec(t), head_tile(bias_prev), head_tile(bias_diag),
                  const2(diff_lambda), const2(subln_g), seqspec(plen), seqspec(plen)],
        out_specs=seqspec(t),
        scratch_shapes=[pltpu.VMEM((t, LANES), F32)] * 2 + [pltpu.VMEM((t, 1), F32)] * 4,
        compiler_params=_cparams(2),
        name="diff_cached_attn",
    )(q, k_new, v_new, bias_prev, bias_diag, diff_lambda, subln_g, k_past, v_past)


def _memo(fn):
    cache = {}

    def get(key):
        if key not in cache:
            cache[key] = fn(key)
        return cache[key]
    return get


def _fox_prompt_kernel(q_ref, k_ref, kx_ref, vt_ref, o_ref, acc_a, acc_b, m_a, m_b, *, tq, rs):
    i = pl.program_id(2)
    n_qs = tq // rs
    lane = lax.broadcasted_iota(I32, (1, LANES), 1)
    lo = lane < HEAD_DIM
    top = lax.broadcasted_iota(I32, (LANES, rs), 0) < HEAD_DIM

    q = q_ref[...].astype(F32)
    ones_a = jnp.where(lane < HEAD_DIM + N_DECAY_TERMS, 1.0, 0.0)
    ones_b = jnp.where(lane < N_DECAY_TERMS, 1.0, 0.0)
    q_heads = (jnp.where(lo, q, ones_a).astype(BF16), jnp.where(lo, ones_b, q).astype(BF16))
    accs = (acc_a, acc_b)
    ms = (m_a, m_b)
    for acc, m in zip(accs, ms):
        acc[...] = jnp.zeros_like(acc)
        m[...] = jnp.full_like(m, NEG_INF)

    def chunk_group(first, units):
        @_memo
        def operands(kc):
            off = pl.multiple_of((first + kc) * rs, rs)
            k = k_ref[pl.ds(off, rs), :]
            kx = kx_ref[pl.ds(off, rs), :]
            vt = vt_ref[:, pl.ds(off, rs)]
            ones_v = jnp.ones_like(vt)
            return ((jnp.where(lo, k, kx), jnp.where(lo, kx, k)),
                    (jnp.where(top, vt, ones_v), jnp.where(top, ones_v, vt)))

        def scores(unit):
            kc, hd, qs, _ = unit
            return _nt_dot(operands(kc)[0][hd], q_heads[hd][qs * rs:(qs + 1) * rs])

        def consume(unit, s):
            kc, hd, qs, mask = unit
            cols = pl.ds(qs * rs, rs)
            if mask is not None:
                s = jnp.where(mask, s, NEG_INF)
            m_old = ms[hd][:, cols]
            m_new = jnp.maximum(m_old, jnp.max(s, axis=0, keepdims=True))
            p = jnp.exp2(s - m_new).astype(BF16)
            accs[hd][:, cols] = jnp.exp2(m_old - m_new) * accs[hd][:, cols] + jnp.dot(
                operands(kc)[1][hd], p, preferred_element_type=F32)
            ms[hd][:, cols] = m_new

        _pipelined(units, scores, consume)

    def below(j, carry):
        chunk_group(n_qs * j, [(kc, hd, qs, None) for kc in range(n_qs) for qs in range(n_qs)
                               for hd in range(2)])
        return carry
    lax.fori_loop(0, i, below, 0)
    causal = (lax.broadcasted_iota(I32, (rs, rs), 0) <= lax.broadcasted_iota(I32, (rs, rs), 1))
    chunk_group(n_qs * i, [(kc, hd, qs, causal if qs == kc else None) for kc in range(n_qs)
                           for qs in range(kc, n_qs) for hd in range(2)])

    a = acc_a[...]
    b = acc_b[...]
    out_t = jnp.concatenate([a[:HEAD_DIM] / a[HEAD_DIM:HEAD_DIM + 1], b[HEAD_DIM:] / b[0:1]],
                            axis=0)
    o_ref[...] = out_t.T.astype(o_ref.dtype)


def _fox_prompt_attention(q, k, kx, vt, *, tq, rs):
    b, t, w = q.shape
    qspec = pl.BlockSpec((None, tq, LANES), lambda bi, p, i: (bi, i, p))
    kspec = pl.BlockSpec((None, t, LANES), lambda bi, p, i: (bi, 0, p))
    return pl.pallas_call(
        functools.partial(_fox_prompt_kernel, tq=tq, rs=rs),
        out_shape=jax.ShapeDtypeStruct((b, t, w), BF16),
        grid=(b, w // LANES, t // tq),
        in_specs=[qspec, kspec, kspec, pl.BlockSpec((LANES, t), lambda bi, p, i: (p, bi))],
        out_specs=qspec,
        scratch_shapes=[pltpu.VMEM((LANES, tq), F32), pltpu.VMEM((LANES, tq), F32),
                        pltpu.VMEM((1, tq), F32), pltpu.VMEM((1, tq), F32)],
        compiler_params=_cparams(3),
        name="fox_prompt_attn",
    )(q, k, kx, vt)


ONES_ROWS = 16


def _diff_prompt_kernel(q_ref, k_ref, vt_ref, bprev_ref, bdiag_ref, lam_ref, g_ref, o_ref,
                        acc0, acc1, m0, m1, *, tq, rs, lambda_init):
    i = pl.program_id(2)
    n_qs = tq // rs
    lo = lax.broadcasted_iota(I32, (1, LANES), 1) < HEAD_DIM

    q = q_ref[...]
    zero = jnp.zeros_like(q)
    q_maps = (jnp.where(lo, q, zero), jnp.where(lo, zero, q))
    accs, ms = (acc0, acc1), (m0, m1)
    for acc, m in zip(accs, ms):
        acc[...] = jnp.zeros_like(acc)
        m[...] = jnp.full_like(m, NEG_INF)

    def chunk_group(first, blocks_ahead):
        @_memo
        def operands(kc):
            off = pl.multiple_of((first + kc) * rs, rs)
            return (k_ref[pl.ds(off, rs), :],
                    jnp.concatenate([vt_ref[:, pl.ds(off, rs)], jnp.ones((ONES_ROWS, rs), BF16)],
                                    axis=0))

        def scores(unit):
            kc, mp, qs = unit
            return _nt_dot(operands(kc)[0], q_maps[mp][qs * rs:(qs + 1) * rs])

        def consume(unit, s):
            kc, mp, qs = unit
            cols = pl.ds(qs * rs, rs)
            distance = blocks_ahead + qs - kc
            if distance == 1:
                s = jnp.maximum(s, NEG_INF) + bprev_ref[...]
            elif distance == 0:
                s = jnp.maximum(s, NEG_INF) + bdiag_ref[...]
            m_old = ms[mp][:, cols]
            m_new = jnp.maximum(m_old, jnp.max(s, axis=0, keepdims=True))
            p = jnp.exp2(s - m_new).astype(BF16)
            accs[mp][:, cols] = jnp.exp2(m_old - m_new) * accs[mp][:, cols] + jnp.dot(
                operands(kc)[1], p, preferred_element_type=F32)
            ms[mp][:, cols] = m_new

        _pipelined([(kc, mp, qs) for kc in range(n_qs) for qs in range(n_qs) for mp in range(2)
                    if blocks_ahead + qs - kc >= 0], scores, consume)

    def far_below(j, carry):
        chunk_group(n_qs * j, 2 * n_qs)
        return carry
    lax.fori_loop(0, jnp.maximum(i - 1, 0), far_below, 0)

    @pl.when(i >= 1)
    def _():
        chunk_group(n_qs * (i - 1), n_qs)

    chunk_group(n_qs * i, 0)

    lam = _diff_lambda(lam_ref[...], lambda_init)
    a0 = acc0[...]
    a1 = acc1[...]
    o = a0[:LANES] / a0[LANES:LANES + 1] - lam * (a1[:LANES] / a1[LANES:LANES + 1])
    scale = lax.rsqrt(jnp.mean(o * o, axis=0, keepdims=True) + SUBLN_EPS)
    o = o * scale * g_ref[...] * (1.0 - lambda_init)
    o_ref[...] = o.T.astype(o_ref.dtype)


def _diff_prompt_attention(q, k, vt, bias_prev, bias_diag, diff_lambda, subln_g, *,
                           tq, rs, lambda_init):
    b, t, w = q.shape
    assert rs >= MAX_DISTANCE
    qspec = pl.BlockSpec((None, tq, LANES), lambda bi, h, i: (bi, i, h))
    head_tile = lambda arr: pl.BlockSpec((None,) + arr.shape[1:], lambda bi, h, i: (h, 0, 0))
    const2 = lambda arr: pl.BlockSpec(arr.shape, lambda bi, h, i: (0, 0))
    return pl.pallas_call(
        functools.partial(_diff_prompt_kernel, tq=tq, rs=rs, lambda_init=lambda_init),
        out_shape=jax.ShapeDtypeStruct((b, t, w), BF16),
        grid=(b, w // LANES, t // tq),
        in_specs=[qspec, pl.BlockSpec((None, t, LANES), lambda bi, h, i: (bi, 0, h)),
                  pl.BlockSpec((LANES, t), lambda bi, h, i: (h, bi)),
                  head_tile(bias_prev), head_tile(bias_diag), const2(diff_lambda),
                  const2(subln_g)],
        out_specs=qspec,
        scratch_shapes=[pltpu.VMEM((LANES + ONES_ROWS, tq), F32)] * 2
        + [pltpu.VMEM((1, tq), F32)] * 2,
        compiler_params=_cparams(3),
        name="diff_prompt_attn",
    )(q, k, vt, bias_prev, bias_diag, diff_lambda, subln_g)


def _t5_bucket(rel):
    nb = NUM_BUCKETS // 2
    max_exact = nb // 2
    ret = jnp.where(rel > 0, nb, 0)
    n = jnp.abs(rel)
    nf = jnp.maximum(n, max_exact).astype(F32)
    large = max_exact + (jnp.log(nf / max_exact) / math.log(MAX_DISTANCE / max_exact)
                         * (nb - max_exact)).astype(I32)
    large = jnp.minimum(large, nb - 1)
    return ret + jnp.where(n < max_exact, n, large)


def _bias_tiles(rel_table, tq, tk_prev):
    buckets = jnp.arange(NUM_BUCKETS, dtype=I32)

    def lookup(rel):
        hit = _t5_bucket(rel)[None, :] == buckets[:, None]
        return jnp.sum(jnp.where(hit[:, None, :], rel_table[:, :, None], 0.0), axis=0)

    far = lookup(jnp.full((1,), -4 * MAX_DISTANCE, I32))

    def tile(rel0, tk):
        period = tq + tk
        j = jnp.arange(period, dtype=I32)
        strip = lookup(rel0 + jnp.where(j < tk, j, j - period)) - far
        flat = jnp.tile(strip, (1, tq))[:, :tq * (period - 1)]
        return (flat.reshape(-1, tq, period - 1)[:, :, :tk] * LOG2E).astype(F32)

    return tile(-tk_prev, tk_prev), tile(0, tq)


def _router_kernel(x_ref, of_ref, od_ref, wof_ref, wod_ref, g_ref, wr_ref, br_ref,
                   x1_ref, hn_ref, te_ref, tw_ref, rank_ref, cnt_ref, *, tm):
    step = pl.program_id(0)

    @pl.when(step == 0)
    def _():
        cnt_ref[...] = jnp.zeros_like(cnt_ref)

    x1 = (x_ref[...] + jnp.dot(of_ref[...], wof_ref[...], preferred_element_type=F32)
          + jnp.dot(od_ref[...], wod_ref[...], preferred_element_type=F32))
    x1_ref[...] = x1
    hn = x1 * _rms_scale(x1, NORM_EPS) * g_ref[...]
    half = hn.shape[1] // 2
    hn_ref[...] = _pack_bf16_pairs(hn[:, :half], hn[:, half:])

    hn_hi = hn.astype(BF16)
    hn_lo = (hn - hn_hi.astype(F32)).astype(BF16)
    both = jnp.dot(hn_hi, wr_ref[...], preferred_element_type=F32)
    logits = (both[:, :LANES] + both[:, LANES:] + br_ref[...]
              + jnp.dot(hn_lo, wr_ref[:, :LANES], preferred_element_type=F32))
    lane = lax.broadcasted_iota(I32, logits.shape, 1)
    lane_f = lane.astype(F32)
    sel = jnp.zeros(logits.shape, F32)
    vals, idxs = [], []
    for _ in range(TOP_K):
        mx = jnp.max(logits, axis=1, keepdims=True)
        idx = jnp.min(jnp.where(logits == mx, lane_f, float(LANES)), axis=1, keepdims=True)
        hit = lane_f == idx
        sel = jnp.where(hit, 1.0, sel)
        logits = jnp.where(hit, -jnp.inf, logits)
        vals.append(mx)
        idxs.append(idx)
    exps = [jnp.exp(v - vals[0]) for v in vals]
    denom = exps[0] + exps[1] + exps[2] + exps[3]

    earlier = (lax.broadcasted_iota(I32, (tm, tm), 1)
               < lax.broadcasted_iota(I32, (tm, tm), 0)).astype(BF16)
    before = cnt_ref[...] + jnp.dot(earlier, sel.astype(BF16), preferred_element_type=F32)
    for k in range(TOP_K):
        te_ref[:, k:k + 1] = idxs[k].astype(I32)
        tw_ref[:, k:k + 1] = exps[k] / denom
        rank_ref[:, k:k + 1] = jnp.sum(
            jnp.where(lane_f == idxs[k], before, 0.0), axis=1, keepdims=True).astype(I32)
    cnt_ref[...] += jnp.sum(sel, axis=0, keepdims=True)


def _router(x2d, o_fox, o_diff, wo_f, wo_d, g_ffn, w_r, b_r, *, tm):
    n, d = x2d.shape
    row = lambda i: (i, 0)
    const = lambda i: (0, 0)
    full = lambda arr: pl.BlockSpec(arr.shape, const)
    return pl.pallas_call(
        functools.partial(_router_kernel, tm=tm),
        out_shape=[jax.ShapeDtypeStruct((n, d), F32), jax.ShapeDtypeStruct((n, d // 2), U32),
                   jax.ShapeDtypeStruct((n, TOP_K), I32), jax.ShapeDtypeStruct((n, TOP_K), F32),
                   jax.ShapeDtypeStruct((n, TOP_K), I32), jax.ShapeDtypeStruct((1, LANES), F32)],
        grid=(n // tm,),
        in_specs=[pl.BlockSpec((tm, d), row), pl.BlockSpec((tm, o_fox.shape[1]), row),
                  pl.BlockSpec((tm, o_diff.shape[1]), row), full(wo_f), full(wo_d),
                  full(g_ffn), full(w_r), full(b_r)],
        out_specs=[pl.BlockSpec((tm, d), row), pl.BlockSpec((tm, d // 2), row),
                   pl.BlockSpec((tm, TOP_K), row), pl.BlockSpec((tm, TOP_K), row),
                   pl.BlockSpec((tm, TOP_K), row), pl.BlockSpec((1, LANES), const)],
        compiler_params=_cparams(1),
        name="router",
    )(x2d, o_fox, o_diff, wo_f, wo_d, g_ffn, w_r, b_r)


def _row_copy(src, src_row, dst, dst_row, sem):
    return pltpu.make_async_copy(src.at[pl.ds(src_row, 1), :], dst.at[pl.ds(dst_row, 1), :], sem)


def _dispatch_kernel(tail_start_ref, tail_valid_ref, pos_ref, hn_ref, xs_hbm, zeros_ref, sem,
                     *, tm, tme, n_experts):
    def tail_copy(e):
        return pltpu.make_async_copy(
            zeros_ref, xs_hbm.at[pl.ds(pl.multiple_of(tail_start_ref[e], tme), tme), :], sem.at[0])

    @pl.when(pl.program_id(0) == 0)
    def _():
        zeros_ref[...] = jnp.zeros_like(zeros_ref)
        for e in range(n_experts):
            @pl.when(tail_valid_ref[e] > 0)
            def _():
                tail_copy(e).start()
        for e in range(n_experts):
            @pl.when(tail_valid_ref[e] > 0)
            def _():
                tail_copy(e).wait()

    def start_row(r, carry):
        for k in range(TOP_K):
            _row_copy(hn_ref, r, xs_hbm, pos_ref[r, k], sem.at[1]).start(priority=k % 2)
        return carry

    def wait_row(r, carry):
        for k in range(TOP_K):
            _row_copy(hn_ref, r, xs_hbm, pos_ref[r, k], sem.at[1]).wait()
        return carry

    lax.fori_loop(0, tm, start_row, 0, unroll=8)
    lax.fori_loop(0, tm, wait_row, 0, unroll=8)


def _dispatch(hn_packed, pos, tail_start, tail_valid, *, m_pad, tm, tme):
    n, half = hn_packed.shape
    n_experts = tail_start.shape[0]
    grid_spec = pltpu.PrefetchScalarGridSpec(
        num_scalar_prefetch=2,
        grid=(n // tm,),
        in_specs=[pl.BlockSpec((tm, TOP_K), lambda i, *_: (i, 0), memory_space=pltpu.SMEM),
                  pl.BlockSpec((tm, half), lambda i, *_: (i, 0))],
        out_specs=pl.BlockSpec(memory_space=pl.ANY),
        scratch_shapes=[pltpu.VMEM((tme, half), U32), pltpu.SemaphoreType.DMA((2,))],
    )
    return pl.pallas_call(
        functools.partial(_dispatch_kernel, tm=tm, tme=tme, n_experts=n_experts),
        out_shape=jax.ShapeDtypeStruct((m_pad, half), U32),
        grid_spec=grid_spec,
        compiler_params=pltpu.CompilerParams(
            dimension_semantics=("arbitrary",), vmem_limit_bytes=VMEM_LIMIT_BYTES,
            has_side_effects=True),
        name="dispatch",
    )(tail_start, tail_valid, pos, hn_packed)


def _experts_kernel(te_ref, nact_ref, xs_ref, wg_ref, bg_ref, wu_ref, bu_ref, wd_ref, bd_ref,
                    ys_ref):
    @pl.when(pl.program_id(0) < nact_ref[0])
    def _():
        half = xs_ref.shape[1]
        x_lo, x_hi = _unpack_bf16_pairs(xs_ref[...])
        x_lo = x_lo.astype(BF16)
        x_hi = x_hi.astype(BF16)

        def proj(w_ref, b_ref):
            return (jnp.dot(x_lo, w_ref[:half, :], preferred_element_type=F32)
                    + jnp.dot(x_hi, w_ref[half:, :], preferred_element_type=F32) + b_ref[...])

        g = jnp.minimum(proj(wg_ref, bg_ref), SWIGLU_LIMIT)
        u = jnp.clip(proj(wu_ref, bu_ref), -SWIGLU_LIMIT, SWIGLU_LIMIT)
        a = ((u + 1.0) * (g * jax.nn.sigmoid(SWIGLU_ALPHA * g))).astype(BF16)
        y = jnp.dot(a, wd_ref[...], preferred_element_type=F32) + bd_ref[...]
        ys_ref[...] = _pack_bf16_pairs(y[:, :half], y[:, half:])


def _experts(xs, tile_expert, n_active, w_gate, b_gate, w_up, b_up, w_down, b_down, *, tme):
    m_pad, half = xs.shape
    n_tiles = m_pad // tme
    d, d_ff = w_gate.shape[1:]
    row = lambda t, te, na: (jnp.minimum(t, na[0] - 1), 0)
    wspec = lambda shape: pl.BlockSpec((None,) + shape, lambda t, te, na: (te[t], 0, 0))
    grid_spec = pltpu.PrefetchScalarGridSpec(
        num_scalar_prefetch=2,
        grid=(n_tiles,),
        in_specs=[pl.BlockSpec((tme, half), row),
                  wspec((d, d_ff)), wspec((1, d_ff)), wspec((d, d_ff)), wspec((1, d_ff)),
                  wspec((d_ff, d)), wspec((1, d))],
        out_specs=pl.BlockSpec((tme, half), row),
    )
    return pl.pallas_call(
        _experts_kernel,
        out_shape=jax.ShapeDtypeStruct((m_pad, half), U32),
        grid_spec=grid_spec,
        compiler_params=_cparams(1),
        name="experts",
    )(tile_expert, n_active, xs, w_gate, b_gate, w_up, b_up, w_down, b_down)


def _combine_kernel(pos_ref, pos_next_ref, x1_ref, tw_ref, p_ref, wg_ref, wp_ref, gple_ref,
                    gfin_ref, ys_hbm, y_ref, gbuf, sem, *, tm):
    step = pl.program_id(0)
    slot = step % 2

    def gather(tile_pos_ref, tile_slot, wait):
        def row(r, carry):
            for k in range(TOP_K):
                copy = _row_copy(ys_hbm, tile_pos_ref[r, k], gbuf.at[tile_slot, k], r,
                                 sem.at[tile_slot])
                if wait:
                    copy.wait()
                else:
                    copy.start(priority=k % 2)
            return carry
        lax.fori_loop(0, tm, row, 0, unroll=8)

    @pl.when(step == 0)
    def _():
        gather(pos_ref, 0, wait=False)

    @pl.when(step + 1 < pl.num_programs(0))
    def _():
        gather(pos_next_ref, 1 - slot, wait=False)

    gather(pos_ref, slot, wait=True)

    x1 = x1_ref[...]
    half = x1.shape[1] // 2
    tw = tw_ref[...]
    moe_lo = jnp.zeros((tm, half), F32)
    moe_hi = jnp.zeros((tm, half), F32)
    for k in range(TOP_K):
        lo, hi = _unpack_bf16_pairs(gbuf[slot, k])
        moe_lo = moe_lo + tw[:, k:k + 1] * lo
        moe_hi = moe_hi + tw[:, k:k + 1] * hi
    x2 = x1 + jnp.concatenate([moe_lo, moe_hi], axis=1)
    hp = (x2 * _rms_scale(x2, NORM_EPS) * gple_ref[...]).astype(BF16)
    gate = jax.nn.sigmoid(jnp.dot(hp, wg_ref[...], preferred_element_type=F32))
    proj = jnp.dot(p_ref[...].astype(BF16), wp_ref[...], preferred_element_type=F32)
    x3 = x2 + proj * gate
    y_ref[...] = x3 * _rms_scale(x3, NORM_EPS) * gfin_ref[...]


def _combine(x1, top_w, pos, ys, p2d, w_ple_gate, w_ple_proj, g_ple, g_final, *, tm):
    n, d = x1.shape
    half = ys.shape[1]
    row = lambda i: (i, 0)
    const = lambda i: (0, 0)
    full = lambda arr: pl.BlockSpec(arr.shape, const)
    return pl.pallas_call(
        functools.partial(_combine_kernel, tm=tm),
        out_shape=jax.ShapeDtypeStruct((n, d), F32),
        grid=(n // tm,),
        in_specs=[pl.BlockSpec((tm, TOP_K), row, memory_space=pltpu.SMEM),
                  pl.BlockSpec((tm, TOP_K), lambda i: (jnp.minimum(i + 1, n // tm - 1), 0),
                               memory_space=pltpu.SMEM),
                  pl.BlockSpec((tm, d), row), pl.BlockSpec((tm, TOP_K), row),
                  pl.BlockSpec((tm, p2d.shape[1]), row), full(w_ple_gate), full(w_ple_proj),
                  full(g_ple), full(g_final), pl.BlockSpec(memory_space=pl.ANY)],
        out_specs=pl.BlockSpec((tm, d), row),
        scratch_shapes=[pltpu.VMEM((2, TOP_K, tm, half), U32), pltpu.SemaphoreType.DMA((2,))],
        compiler_params=_cparams(1),
        name="combine",
    )(pos, pos, x1, top_w, p2d, w_ple_gate, w_ple_proj, g_ple, g_final, ys)


def _pick_tile(n, want):
    t = min(n, want)
    assert n % t == 0, (n, t)
    return t


def _layer_group(x, p, past, wts, lambda_init, g_final):
    (g_mix, w_main, w_vt, w_f, b_f, diff_lambda, subln_g, rel_table, wo_f, wo_d, g_ffn, w_r, b_r,
     w_gate, b_gate, w_up, b_up, w_down, b_down, g_ple, w_ple_gate, w_ple_proj) = wts
    b, t, d = x.shape
    n = b * t
    n_fox = (w_main.shape[1] // 6) // HEAD_DIM
    n_experts = w_gate.shape[0]
    x2d = x.reshape(n, d)

    tm = _pick_tile(n, 512)
    (qf, qd, kf, vf, kd, vd, kf16, kd16, vf16, vd16, logf_lanes) = _inproj(
        x2d, g_mix, w_main, w_vt, w_f, b_f, tm=tm, values_transposed=past is None)
    logf = logf_lanes[:, :n_fox]
    width = kf.shape[1]
    as_seq = lambda a: a.reshape(b, t, a.shape[-1])

    tq = _pick_tile(t, 1024)
    rs = _pick_tile(tq, 256)
    if past is None:
        kx = _key_decay(as_seq(logf_lanes), n_fox, width)
        o_fox = _fox_prompt_attention(as_seq(qf), as_seq(kf16), kx, vf16, tq=tq, rs=rs)
        bias_prev, bias_diag = _bias_tiles(rel_table, rs, rs)
        block = jnp.arange(rs, dtype=I32) // CHUNK
        bias_diag = jnp.where(block[:, None] <= block[None, :], jnp.swapaxes(bias_diag, 1, 2),
                              NEG_INF)
        o_diff = _diff_prompt_attention(
            as_seq(qd), as_seq(kd16), vd16, jnp.swapaxes(bias_prev, 1, 2), bias_diag,
            diff_lambda, subln_g.reshape(-1, 1), tq=tq, rs=rs, lambda_init=lambda_init)
    else:
        pk_f, pv_f, plogf, pk_d, pv_d = past
        plen = pk_f.shape[1]
        tkp = 512
        total = plen + t
        padded = -(-total // LANES) * LANES
        seq = jnp.concatenate([jnp.transpose(plogf.astype(F32), (0, 2, 1)),
                               jnp.transpose(as_seq(logf), (0, 2, 1))], axis=2)
        c_all = _cumsum_lanes(jnp.pad(seq, ((0, 0), (0, 0), (0, padded - total))))
        flat = lambda a: a.reshape(b, plen, width)
        o_fox = _fox_cached_attention(as_seq(qf), as_seq(kf16), as_seq(vf16),
                                      c_all[:, :, plen:total], flat(pk_f), flat(pv_f),
                                      c_all[:, :, :plen], tkp=tkp)
        bias_prev, bias_diag = _bias_tiles(rel_table, t, tkp)
        o_diff = _diff_cached_attention(as_seq(qd), as_seq(kd16), as_seq(vd16), bias_prev,
                                        bias_diag, diff_lambda, subln_g, flat(pk_d), flat(pv_d),
                                        tkp=tkp, lambda_init=lambda_init)

    tmr = _pick_tile(n, 256)
    x1, hn_packed, top_e, top_w, rank, counts = _router(
        x2d, o_fox.reshape(n, width), o_diff.reshape(n, width), wo_f, wo_d, g_ffn, w_r, b_r, tm=tmr)

    tme = 512 if n * TOP_K >= 64 * 512 else 256
    n_tiles = -(-(n * TOP_K) // tme) + n_experts
    m_pad = n_tiles * tme
    cnt = counts[0, :n_experts].astype(I32)
    group = -(-cnt // tme) * tme
    ends = jnp.cumsum(group)
    starts = ends - group
    experts = jnp.arange(n_experts, dtype=I32)
    pos = rank + jnp.sum(jnp.where(top_e[:, :, None] == experts, starts, 0), axis=2)
    n_active = (ends[-1] // tme).astype(I32)
    tile_ids = jnp.arange(n_tiles, dtype=I32)
    tile_expert = jnp.sum(jnp.minimum(tile_ids, n_active - 1)[:, None] >= (ends // tme)[None, :],
                          axis=1).astype(I32)
    tile_expert = jnp.minimum(tile_expert, n_experts - 1)
    tail_start = jnp.maximum(ends - tme, 0).astype(I32)
    tail_valid = (group > 0).astype(I32)

    tmd = _pick_tile(n, 256)
    xs = _dispatch(hn_packed, pos, tail_start, tail_valid, m_pad=m_pad, tm=tmd, tme=tme)
    ys = _experts(xs, tile_expert, n_active.reshape(1), w_gate, b_gate, w_up, b_up, w_down, b_down,
                  tme=tme)
    y = _combine(x1, top_w, pos, ys, p.reshape(n, p.shape[-1]), w_ple_gate, w_ple_proj, g_ple,
                 g_final, tm=tmd)
    return y.reshape(b, t, d), (kf, vf, logf, kd, vd)


def kernel(x_prompt, x_sample, p_prompt, p_sample, cache_fox_k, cache_fox_v, cache_fox_logf, cache_diff_k, cache_diff_v, g_mix, w_in, b_forget, diff_lambda, diff_subln_g, rel_bias_table, w_o, g_ffn, w_router, b_router, w_gate, b_gate, w_up, b_up, w_down, b_down, g_ple, w_ple_gate, w_ple_proj, g_final):
    depth = g_mix.shape[0]
    assert depth == 1, "the final norm is fused into the last layer; one layer supported"
    d = x_prompt.shape[-1]
    n_fox = cache_fox_k.shape[3]
    fox_w = n_fox * cache_fox_k.shape[4]
    n_diff = cache_diff_k.shape[3]
    diff_w = n_diff * 2 * cache_diff_k.shape[5]
    assert fox_w == diff_w and fox_w % LANES == 0
    n_experts = w_router.shape[-1]
    assert TOP_K <= n_experts <= LANES
    row2 = lambda a: a.reshape(1, -1).astype(F32)

    i = 0
    lambda_init = 0.8 - 0.6 * math.exp(-0.3 * i)
    off_fk, off_fv, off_ff = fox_w, 2 * fox_w, 3 * fox_w
    off_dq = off_ff + n_fox
    off_dk, off_dv = off_dq + diff_w, off_dq + 2 * diff_w
    w = w_in[i]
    cols = lambda o, wd: w[:, o:o + wd]
    w_main = jnp.concatenate([cols(0, fox_w), cols(off_dq, diff_w), cols(off_fk, fox_w),
                              cols(off_fv, fox_w), cols(off_dk, diff_w), cols(off_dv, diff_w)],
                             axis=1).astype(BF16)
    w_vt = jnp.concatenate([cols(off_fv, fox_w), cols(off_dv, diff_w)], axis=1).T.astype(BF16)
    w_f = jnp.pad(cols(off_ff, n_fox), ((0, 0), (0, LANES - n_fox))).astype(BF16)
    b_f = jnp.pad(row2(b_forget[i]), ((0, 0), (0, LANES - n_fox)))
    w_r = jnp.pad(w_router[i].astype(F32), ((0, 0), (0, LANES - n_experts)))
    w_r_hi = w_r.astype(BF16)
    w_r = jnp.concatenate([w_r_hi, (w_r - w_r_hi.astype(F32)).astype(BF16)], axis=1)
    b_r = jnp.pad(row2(b_router[i]), ((0, 0), (0, LANES - n_experts)), constant_values=NEG_INF)
    wo = w_o[i].astype(BF16)
    bias3 = lambda a: a.reshape(n_experts, 1, -1).astype(F32)
    wts = (row2(g_mix[i]), w_main, w_vt, w_f, b_f, diff_lambda[i].astype(F32), row2(diff_subln_g[i]),
           rel_bias_table.astype(F32), wo[:fox_w], wo[fox_w:], row2(g_ffn[i]), w_r, b_r,
           w_gate[i].astype(BF16), bias3(b_gate[i]), w_up[i].astype(BF16), bias3(b_up[i]),
           w_down[i].astype(BF16), bias3(b_down[i]), row2(g_ple[i]),
           w_ple_gate[i].astype(BF16), w_ple_proj[i].astype(BF16))
    gf = row2(g_final)

    yp, new_p = _layer_group(x_prompt, p_prompt[i], None, wts, lambda_init, gf)
    past_s = (cache_fox_k[i], cache_fox_v[i], cache_fox_logf[i], cache_diff_k[i], cache_diff_v[i])
    ys, new_s = _layer_group(x_sample, p_sample[i], past_s, wts, lambda_init, gf)

    def rows(new, b, t):
        kf, vf, logf, kd, vd = new
        return (kf.reshape(1, b, t, n_fox, -1), vf.reshape(1, b, t, n_fox, -1),
                logf.reshape(1, b, t, n_fox), kd.reshape(1, b, t, n_diff, 2, -1),
                vd.reshape(1, b, t, n_diff, -1))

    bp, tp = x_prompt.shape[:2]
    bs, ts = x_sample.shape[:2]
    return (yp, ys) + rows(new_p, bp, tp) + rows(new_s, bs, ts)
```

```python
import functools
import math

import jax
import jax.numpy as jnp
import numpy as np
from jax import lax
from jax.experimental import pallas as pl
from jax.experimental.pallas import tpu as pltpu

HEAD_DIM = 64
CHUNK = 64
NUM_BUCKETS = 32
MAX_DISTANCE = 128
TOP_K = 4
SWIGLU_LIMIT = 7.0
SWIGLU_ALPHA = 1.702
NORM_EPS = 1e-6
SUBLN_EPS = 1e-5
NEG_INF = -1e30
LOG2E = math.log2(math.e)

LANES = 128
VMEM_LIMIT_BYTES = 56 * 1024 * 1024

F32 = jnp.float32
BF16 = jnp.bfloat16
I32 = jnp.int32
U32 = jnp.uint32


def _cparams(n_axes):
    return pltpu.CompilerParams(
        dimension_semantics=("arbitrary",) * n_axes, vmem_limit_bytes=VMEM_LIMIT_BYTES)


def _rms_scale(x, eps):
    return lax.rsqrt(jnp.mean(x * x, axis=-1, keepdims=True) + eps)


def _pack_bf16_pairs(lo, hi):
    lo_bits = pltpu.bitcast(lo.astype(BF16).astype(F32), U32)
    hi_bits = pltpu.bitcast(hi.astype(BF16).astype(F32), U32)
    return lo_bits | (hi_bits >> 16)


def _unpack_bf16_pairs(packed):
    lo = pltpu.bitcast(packed & jnp.uint32(0xFFFF0000), F32)
    hi = pltpu.bitcast(packed << 16, F32)
    return lo, hi


def _inproj_kernel(x_ref, g_ref, w_ref, wvt_ref, wf_ref, bf_ref,
                   qf_ref, qd_ref, kf_ref, vf_ref, kd_ref, vd_ref,
                   kf16_ref, kd16_ref, vf16_ref, vd16_ref, logf_ref, *, width, values_transposed):
    x = x_ref[...]
    h = (x * _rms_scale(x, NORM_EPS) * g_ref[...]).astype(BF16)
    q_scale = LOG2E / math.sqrt(HEAD_DIM)

    def proj(c):
        return jnp.dot(h, w_ref[:, c * width:(c + 1) * width], preferred_element_type=F32)

    qf_ref[...] = (proj(0) * q_scale).astype(BF16)
    qd_ref[...] = (proj(1) * q_scale).astype(BF16)
    for c, (o32, o16) in enumerate(((kf_ref, kf16_ref), (vf_ref, vf16_ref),
                                    (kd_ref, kd16_ref), (vd_ref, vd16_ref))):
        u = proj(2 + c)
        if len(o32.shape) == 3:
            o32[...] = pltpu.einshape("m(hd)->mhd", u, h=o32.shape[1])
        else:
            o32[...] = pltpu.einshape("m(hed)->mhed", u, h=o32.shape[1], e=o32.shape[2])
        if not (values_transposed and c in (1, 3)):
            o16[...] = u.astype(BF16)
    if values_transposed:
        vf16_ref[...] = _nt_dot(wvt_ref[:width, :], h).astype(BF16)
        vd16_ref[...] = _nt_dot(wvt_ref[width:, :], h).astype(BF16)
    f = jnp.dot(h, wf_ref[...], preferred_element_type=F32) + bf_ref[...]
    logf_ref[...] = jax.nn.log_sigmoid(f)


def _inproj(x2d, g_mix, w_main, w_vt, w_f, b_f, *, tm, values_transposed):
    n, d = x2d.shape
    width = w_main.shape[1] // 6
    row = lambda i: (i, 0)
    const = lambda i: (0, 0)
    if values_transposed:
        v16_shape, v16_spec = (width, n), pl.BlockSpec((width, tm), lambda i: (0, i))
    else:
        v16_shape, v16_spec = (n, width), pl.BlockSpec((tm, width), row)
    head_shapes = [(width // HEAD_DIM, HEAD_DIM)] * 2 + [
        (width // (2 * HEAD_DIM), 2, HEAD_DIM), (width // (2 * HEAD_DIM), 2 * HEAD_DIM)]
    out_shape = ([jax.ShapeDtypeStruct((n, width), BF16)] * 2
                 + [jax.ShapeDtypeStruct((n,) + s, F32) for s in head_shapes]
                 + [jax.ShapeDtypeStruct((n, width), BF16)] * 2
                 + [jax.ShapeDtypeStruct(v16_shape, BF16)] * 2
                 + [jax.ShapeDtypeStruct((n, LANES), F32)])
    out_specs = ([pl.BlockSpec((tm, width), row)] * 2
                 + [pl.BlockSpec((tm,) + s, lambda i, s=s: (i,) + (0,) * len(s))
                    for s in head_shapes]
                 + [pl.BlockSpec((tm, width), row)] * 2 + [v16_spec] * 2
                 + [pl.BlockSpec((tm, LANES), row)])
    return pl.pallas_call(
        functools.partial(_inproj_kernel, width=width, values_transposed=values_transposed),
        out_shape=out_shape,
        grid=(n // tm,),
        in_specs=[pl.BlockSpec((tm, d), row), pl.BlockSpec((1, d), const),
                  pl.BlockSpec(w_main.shape, const), pl.BlockSpec(w_vt.shape, const),
                  pl.BlockSpec(w_f.shape, const), pl.BlockSpec(b_f.shape, const)],
        out_specs=out_specs,
        compiler_params=_cparams(1),
        name="inproj",
    )(x2d, g_mix, w_main, w_vt, w_f, b_f)


def _cumsum_kernel(x_ref, o_ref):
    x = x_ref[...]
    length = x.shape[1]
    lane = lax.broadcasted_iota(I32, x.shape, 1)
    shift = 1
    while shift < length:
        x = x + jnp.where(lane >= shift, pltpu.roll(x, shift, axis=1), 0.0)
        shift *= 2
    o_ref[...] = x


def _cumsum_lanes(x):
    b, h, length = x.shape
    spec = pl.BlockSpec((None, h, length), lambda i: (i, 0, 0))
    return pl.pallas_call(
        _cumsum_kernel, out_shape=jax.ShapeDtypeStruct(x.shape, F32), grid=(b,),
        in_specs=[spec], out_specs=spec, compiler_params=_cparams(1), name="cumsum",
    )(x)


N_DECAY_TERMS = 3


def _key_decay_kernel(x_ref, sel_ref, o_ref):
    x = x_ref[...]
    length = x.shape[0]
    row = lax.broadcasted_iota(I32, x.shape, 0)
    shift = 1
    while shift < length:
        x = x + jnp.where(row >= shift, pltpu.roll(x, shift, axis=0), 0.0)
        shift *= 2
    rest = x * (-LOG2E)
    out = jnp.zeros(o_ref.shape, F32)
    for j in range(N_DECAY_TERMS):
        term = rest.astype(BF16)
        rest = rest - term.astype(F32)
        out = out + jnp.dot(term, sel_ref[j], preferred_element_type=F32)
    o_ref[...] = out.astype(BF16)


def _decay_selectors(n_heads, width):
    sel = np.zeros((N_DECAY_TERMS, LANES, width), np.float32)
    for h in range(n_heads):
        base = (h // 2) * LANES + (HEAD_DIM if h % 2 == 0 else 0)
        for j in range(N_DECAY_TERMS):
            sel[j, h, base + j] = 1.0
    return jnp.asarray(sel, BF16)


def _key_decay(logf, n_heads, width):
    b, length, lanes = logf.shape
    sel = _decay_selectors(n_heads, width)
    return pl.pallas_call(
        _key_decay_kernel, out_shape=jax.ShapeDtypeStruct((b, length, width), BF16), grid=(b,),
        in_specs=[pl.BlockSpec((None, length, lanes), lambda i: (i, 0, 0)),
                  pl.BlockSpec(sel.shape, lambda i: (0, 0, 0))],
        out_specs=pl.BlockSpec((None, length, width), lambda i: (i, 0, 0)),
        compiler_params=_cparams(1), name="key_decay",
    )(logf, sel)


def _nt_dot(a, b):
    return lax.dot_general(a, b, (((1,), (1,)), ((), ())), preferred_element_type=F32)


SCORE_LOOKAHEAD = 4


def _pipelined(units, scores, consume):
    ahead = [scores(unit) for unit in units[:SCORE_LOOKAHEAD]]
    for n, unit in enumerate(units):
        s = ahead.pop(0)
        if n + SCORE_LOOKAHEAD < len(units):
            ahead.append(scores(units[n + SCORE_LOOKAHEAD]))
        consume(unit, s)


def _memo(fn):
    cache = {}

    def get(key):
        if key not in cache:
            cache[key] = fn(key)
        return cache[key]
    return get


CACHED_CHUNK_GROUP = 2


def _fox_cached_kernel(q_ref, kn_ref, vn_ref, cn_ref, cp_ref, kp_ref, vp_ref,
                       o_ref, acc_a, acc_b, m_a, m_b, *, tkp, n_past):
    t = q_ref.shape[0]
    lo = lax.broadcasted_iota(I32, (1, LANES), 1) < HEAD_DIM
    q = q_ref[...]
    zero = jnp.zeros_like(q)
    q_heads = (jnp.where(lo, q, zero), jnp.where(lo, zero, q))
    accs = (acc_a, acc_b)
    ms = (m_a, m_b)
    for acc, m in zip(accs, ms):
        acc[...] = jnp.zeros_like(acc)
        m[...] = jnp.full_like(m, NEG_INF)

    def run(chunks):
        @_memo
        def operands(n):
            k, v, c_rows, _ = chunks[n]
            v = v.astype(BF16)
            one = jnp.ones_like(v)
            return k.astype(BF16), (jnp.where(lo, v, one), jnp.where(lo, one, v)), c_rows * LOG2E

        def scores(unit):
            n, hd = unit
            return _nt_dot(q_heads[hd], operands(n)[0])

        def consume(unit, s):
            n, hd = unit
            _, v_heads, c2 = operands(n)
            s = s - c2[hd:hd + 1, :]
            if chunks[n][3] is not None:
                s = jnp.where(chunks[n][3], s, NEG_INF)
            m_old = ms[hd][...]
            m_new = jnp.maximum(m_old, jnp.max(s, axis=1, keepdims=True))
            p = jnp.exp2(s - m_new).astype(BF16)
            accs[hd][...] = jnp.exp2(m_old - m_new) * accs[hd][...] + jnp.dot(
                p, v_heads[hd], preferred_element_type=F32)
            ms[hd][...] = m_new

        _pipelined([(n, hd) for n in range(len(chunks)) for hd in range(2)], scores, consume)

    def cached(c):
        off = pl.multiple_of(c * tkp, tkp)
        return (kp_ref[pl.ds(off, tkp), :], vp_ref[pl.ds(off, tkp), :],
                cp_ref[:, pl.ds(off, tkp)], None)

    def past_body(j, carry):
        run([cached(j * CACHED_CHUNK_GROUP + g) for g in range(CACHED_CHUNK_GROUP)])
        return carry
    lax.fori_loop(0, n_past // CACHED_CHUNK_GROUP, past_body, 0)
    causal = lax.broadcasted_iota(I32, (t, t), 1) <= lax.broadcasted_iota(I32, (t, t), 0)
    run([cached(c) for c in range(n_past - n_past % CACHED_CHUNK_GROUP, n_past)]
        + [(kn_ref[...], vn_ref[...], cn_ref[...], causal)])

    a = acc_a[...]
    b = acc_b[...]
    out = jnp.where(lo, a / pltpu.roll(a, HEAD_DIM, axis=1), b / pltpu.roll(b, HEAD_DIM, axis=1))
    o_ref[...] = out.astype(o_ref.dtype)


def _fox_cached_attention(q, k_new, v_new, c_new, k_past, v_past, c_past, *, tkp):
    b, t, w = q.shape
    n_pairs = w // LANES
    plen = k_past.shape[1]
    n_past = plen // tkp
    assert n_past * tkp == plen
    pair_rows = lambda c: c.reshape(b, n_pairs, 2, c.shape[-1])
    seqspec = lambda n: pl.BlockSpec((None, n, LANES), lambda bi, p: (bi, 0, p))
    cspec = lambda n: pl.BlockSpec((None, None, 2, n), lambda bi, p: (bi, p, 0, 0))
    return pl.pallas_call(
        functools.partial(_fox_cached_kernel, tkp=tkp, n_past=n_past),
        out_shape=jax.ShapeDtypeStruct((b, t, w), BF16),
        grid=(b, n_pairs),
        in_specs=[seqspec(t), seqspec(t), seqspec(t), cspec(t), cspec(plen),
                  seqspec(plen), seqspec(plen)],
        out_specs=seqspec(t),
        scratch_shapes=[pltpu.VMEM((t, LANES), F32)] * 2 + [pltpu.VMEM((t, 1), F32)] * 2,
        compiler_params=_cparams(2),
        name="fox_cached_attn",
    )(q, k_new, v_new, pair_rows(c_new), pair_rows(c_past), k_past, v_past)


def _diff_lambda(dl, lambda_init):
    return (jnp.exp(jnp.sum(dl[0:1] * dl[1:2], axis=1, keepdims=True))
            - jnp.exp(jnp.sum(dl[2:3] * dl[3:4], axis=1, keepdims=True)) + lambda_init)


def _diff_cached_kernel(q_ref, kn_ref, vn_ref, bprev_ref, bdiag_ref, lam_ref, g_ref,
                        kp_ref, vp_ref, o_ref, acc0, acc1, m0, m1, l0, l1,
                        *, tkp, n_past, lambda_init):
    lo = lax.broadcasted_iota(I32, (1, LANES), 1) < HEAD_DIM
    q = q_ref[...]
    zero = jnp.zeros_like(q)
    q_maps = (jnp.where(lo, q, zero), jnp.where(lo, zero, q))
    accs, ms, ls = (acc0, acc1), (m0, m1), (l0, l1)
    for acc, m, l in zip(accs, ms, ls):
        acc[...] = jnp.zeros_like(acc)
        l[...] = jnp.zeros_like(l)
        m[...] = jnp.full_like(m, NEG_INF)

    def run(chunks):
        @_memo
        def operands(n):
            return chunks[n][0].astype(BF16), chunks[n][1].astype(BF16)

        def scores(unit):
            n, mp = unit
            return _nt_dot(q_maps[mp], operands(n)[0])

        def consume(unit, s):
            n, mp = unit
            if chunks[n][2] is not None:
                s = jnp.maximum(s, NEG_INF) + chunks[n][2]
            m_old = ms[mp][...]
            m_new = jnp.maximum(m_old, jnp.max(s, axis=1, keepdims=True))
            p = jnp.exp2(s - m_new)
            alpha = jnp.exp2(m_old - m_new)
            ls[mp][...] = alpha * ls[mp][...] + jnp.sum(p, axis=1, keepdims=True)
            accs[mp][...] = alpha * accs[mp][...] + jnp.dot(
                p.astype(BF16), operands(n)[1], preferred_element_type=F32)
            ms[mp][...] = m_new

        _pipelined([(n, mp) for n in range(len(chunks)) for mp in range(2)], scores, consume)

    def cached(c, bias=None):
        off = pl.multiple_of(c * tkp, tkp)
        return kp_ref[pl.ds(off, tkp), :], vp_ref[pl.ds(off, tkp), :], bias

    def past_body(j, carry):
        run([cached(j * CACHED_CHUNK_GROUP + g) for g in range(CACHED_CHUNK_GROUP)])
        return carry
    n_far = n_past - 1
    lax.fori_loop(0, n_far // CACHED_CHUNK_GROUP, past_body, 0)
    run([cached(c) for c in range(n_far - n_far % CACHED_CHUNK_GROUP, n_far)]
        + [cached(n_far, bprev_ref[...]), (kn_ref[...], vn_ref[...], bdiag_ref[...])])

    lam = _diff_lambda(lam_ref[...], lambda_init)
    o = acc0[...] / l0[...] - lam * (acc1[...] / l1[...])
    o = o * _rms_scale(o, SUBLN_EPS) * g_ref[...] * (1.0 - lambda_init)
    o_ref[...] = o.astype(o_ref.dtype)


def _diff_cached_attention(q, k_new, v_new, bias_prev, bias_diag, diff_lambda, subln_g,
                           k_past, v_past, *, tkp, lambda_init):
    b, t, w = q.shape
    plen = k_past.shape[1]
    n_past = plen // tkp
    assert n_past * tkp == plen and tkp >= MAX_DISTANCE and t <= CHUNK and plen % CHUNK == 0
    seqspec = lambda n: pl.BlockSpec((None, n, LANES), lambda bi, h: (bi, 0, h))
    head_tile = lambda arr: pl.BlockSpec((None,) + arr.shape[1:], lambda bi, h: (h, 0, 0))
    const2 = lambda arr: pl.BlockSpec(arr.shape, lambda bi, h: (0, 0))
    return pl.pallas_call(
        functools.partial(_diff_cached_kernel, tkp=tkp, n_past=n_past, lambda_init=lambda_init),
        out_shape=jax.ShapeDtypeStruct((b, t, w), BF16),
        grid=(b, w // LANES),
        in_specs=[seqspec(t), seqspec(t), seqspec(t), head_tile(bias_prev), head_tile(bias_diag),
                  const2(diff_lambda), const2(subln_g), seqspec(plen), seqspec(plen)],
        out_specs=seqspec(t),
        scratch_shapes=[pltpu.VMEM((t, LANES), F32)] * 2 + [pltpu.VMEM((t, 1), F32)] * 4,
        compiler_params=_cparams(2),
        name="diff_cached_attn",
    )(q, k_new, v_new, bias_prev, bias_diag, diff_lambda, subln_g, k_past, v_past)


def _fox_prompt_kernel(q_ref, k_ref, kx_ref, vt_ref, o_ref, acc_a, acc_b, m_a, m_b, *, tq, rs):
    i = pl.program_id(2)
    n_qs = tq // rs
    lane = lax.broadcasted_iota(I32, (1, LANES), 1)
    lo = lane < HEAD_DIM
    top = lax.broadcasted_iota(I32, (LANES, rs), 0) < HEAD_DIM

    q = q_ref[...].astype(F32)
    ones_a = jnp.where(lane < HEAD_DIM + N_DECAY_TERMS, 1.0, 0.0)
    ones_b = jnp.where(lane < N_DECAY_TERMS, 1.0, 0.0)
    q_heads = (jnp.where(lo, q, ones_a).astype(BF16), jnp.where(lo, ones_b, q).astype(BF16))
    accs = (acc_a, acc_b)
    ms = (m_a, m_b)
    for acc, m in zip(accs, ms):
        acc[...] = jnp.zeros_like(acc)
        m[...] = jnp.full_like(m, NEG_INF)

    def chunk_group(first, units):
        @_memo
        def operands(kc):
            off = pl.multiple_of((first + kc) * rs, rs)
            k = k_ref[pl.ds(off, rs), :]
            kx = kx_ref[pl.ds(off, rs), :]
            vt = vt_ref[:, pl.ds(off, rs)]
            ones_v = jnp.ones_like(vt)
            return ((jnp.where(lo, k, kx), jnp.where(lo, kx, k)),
                    (jnp.where(top, vt, ones_v), jnp.where(top, ones_v, vt)))

        def scores(unit):
            kc, hd, qs, _ = unit
            return _nt_dot(operands(kc)[0][hd], q_heads[hd][qs * rs:(qs + 1) * rs])

        def consume(unit, s):
            kc, hd, qs, mask = unit
            cols = pl.ds(qs * rs, rs)
            if mask is not None:
                s = jnp.where(mask, s, NEG_INF)
            m_old = ms[hd][:, cols]
            m_new = jnp.maximum(m_old, jnp.max(s, axis=0, keepdims=True))
            p = jnp.exp2(s - m_new).astype(BF16)
            accs[hd][:, cols] = jnp.exp2(m_old - m_new) * accs[hd][:, cols] + jnp.dot(
                operands(kc)[1][hd], p, preferred_element_type=F32)
            ms[hd][:, cols] = m_new

        _pipelined(units, scores, consume)

    def below(j, carry):
        chunk_group(n_qs * j, [(kc, hd, qs, None) for kc in range(n_qs) for qs in range(n_qs)
                               for hd in range(2)])
        return carry
    lax.fori_loop(0, i, below, 0)
    causal = (lax.broadcasted_iota(I32, (rs, rs), 0) <= lax.broadcasted_iota(I32, (rs, rs), 1))
    chunk_group(n_qs * i, [(kc, hd, qs, causal if qs == kc else None) for kc in range(n_qs)
                           for qs in range(kc, n_qs) for hd in range(2)])

    a = acc_a[...]
    b = acc_b[...]
    out_t = jnp.concatenate([a[:HEAD_DIM] / a[HEAD_DIM:HEAD_DIM + 1], b[HEAD_DIM:] / b[0:1]],
                            axis=0)
    o_ref[...] = out_t.T.astype(o_ref.dtype)


def _fox_prompt_attention(q, k, kx, vt, *, tq, rs):
    b, t, w = q.shape
    qspec = pl.BlockSpec((None, tq, LANES), lambda bi, p, i: (bi, i, p))
    kspec = pl.BlockSpec((None, t, LANES), lambda bi, p, i: (bi, 0, p))
    return pl.pallas_call(
        functools.partial(_fox_prompt_kernel, tq=tq, rs=rs),
        out_shape=jax.ShapeDtypeStruct((b, t, w), BF16),
        grid=(b, w // LANES, t // tq),
        in_specs=[qspec, kspec, kspec, pl.BlockSpec((LANES, t), lambda bi, p, i: (p, bi))],
        out_specs=qspec,
        scratch_shapes=[pltpu.VMEM((LANES, tq), F32), pltpu.VMEM((LANES, tq), F32),
                        pltpu.VMEM((1, tq), F32), pltpu.VMEM((1, tq), F32)],
        compiler_params=_cparams(3),
        name="fox_prompt_attn",
    )(q, k, kx, vt)


ONES_ROWS = 16


def _diff_prompt_kernel(q_ref, k_ref, vt_ref, bprev_ref, bdiag_ref, lam_ref, g_ref, o_ref,
                        acc0, acc1, m0, m1, *, tq, rs, lambda_init):
    i = pl.program_id(2)
    n_qs = tq // rs
    lo = lax.broadcasted_iota(I32, (1, LANES), 1) < HEAD_DIM

    q = q_ref[...]
    zero = jnp.zeros_like(q)
    q_maps = (jnp.where(lo, q, zero), jnp.where(lo, zero, q))
    accs, ms = (acc0, acc1), (m0, m1)
    for acc, m in zip(accs, ms):
        acc[...] = jnp.zeros_like(acc)
        m[...] = jnp.full_like(m, NEG_INF)

    def chunk_group(first, blocks_ahead):
        @_memo
        def operands(kc):
            off = pl.multiple_of((first + kc) * rs, rs)
            return (k_ref[pl.ds(off, rs), :],
                    jnp.concatenate([vt_ref[:, pl.ds(off, rs)], jnp.ones((ONES_ROWS, rs), BF16)],
                                    axis=0))

        def scores(unit):
            kc, mp, qs = unit
            return _nt_dot(operands(kc)[0], q_maps[mp][qs * rs:(qs + 1) * rs])

        def consume(unit, s):
            kc, mp, qs = unit
            cols = pl.ds(qs * rs, rs)
            distance = blocks_ahead + qs - kc
            if distance == 1:
                s = jnp.maximum(s, NEG_INF) + bprev_ref[...]
            elif distance == 0:
                s = jnp.maximum(s, NEG_INF) + bdiag_ref[...]
            m_old = ms[mp][:, cols]
            m_new = jnp.maximum(m_old, jnp.max(s, axis=0, keepdims=True))
            p = jnp.exp2(s - m_new).astype(BF16)
            accs[mp][:, cols] = jnp.exp2(m_old - m_new) * accs[mp][:, cols] + jnp.dot(
                operands(kc)[1], p, preferred_element_type=F32)
            ms[mp][:, cols] = m_new

        _pipelined([(kc, mp, qs) for kc in range(n_qs) for qs in range(n_qs) for mp in range(2)
                    if blocks_ahead + qs - kc >= 0], scores, consume)

    def far_below(j, carry):
        chunk_group(n_qs * j, 2 * n_qs)
        return carry
    lax.fori_loop(0, jnp.maximum(i - 1, 0), far_below, 0)

    @pl.when(i >= 1)
    def _():
        chunk_group(n_qs * (i - 1), n_qs)

    chunk_group(n_qs * i, 0)

    lam = _diff_lambda(lam_ref[...], lambda_init)
    a0 = acc0[...]
    a1 = acc1[...]
    o = a0[:LANES] / a0[LANES:LANES + 1] - lam * (a1[:LANES] / a1[LANES:LANES + 1])
    scale = lax.rsqrt(jnp.mean(o * o, axis=0, keepdims=True) + SUBLN_EPS)
    o = o * scale * g_ref[...] * (1.0 - lambda_init)
    o_ref[...] = o.T.astype(o_ref.dtype)


def _diff_prompt_attention(q, k, vt, bias_prev, bias_diag, diff_lambda, subln_g, *,
                           tq, rs, lambda_init):
    b, t, w = q.shape
    assert rs >= MAX_DISTANCE
    qspec = pl.BlockSpec((None, tq, LANES), lambda bi, h, i: (bi, i, h))
    head_tile = lambda arr: pl.BlockSpec((None,) + arr.shape[1:], lambda bi, h, i: (h, 0, 0))
    const2 = lambda arr: pl.BlockSpec(arr.shape, lambda bi, h, i: (0, 0))
    return pl.pallas_call(
        functools.partial(_diff_prompt_kernel, tq=tq, rs=rs, lambda_init=lambda_init),
        out_shape=jax.ShapeDtypeStruct((b, t, w), BF16),
        grid=(b, w // LANES, t // tq),
        in_specs=[qspec, pl.BlockSpec((None, t, LANES), lambda bi, h, i: (bi, 0, h)),
                  pl.BlockSpec((LANES, t), lambda bi, h, i: (h, bi)),
                  head_tile(bias_prev), head_tile(bias_diag), const2(diff_lambda),
                  const2(subln_g)],
        out_specs=qspec,
        scratch_shapes=[pltpu.VMEM((LANES + ONES_ROWS, tq), F32)] * 2
        + [pltpu.VMEM((1, tq), F32)] * 2,
        compiler_params=_cparams(3),
        name="diff_prompt_attn",
    )(q, k, vt, bias_prev, bias_diag, diff_lambda, subln_g)


def _t5_bucket(rel):
    nb = NUM_BUCKETS // 2
    max_exact = nb // 2
    ret = jnp.where(rel > 0, nb, 0)
    n = jnp.abs(rel)
    nf = jnp.maximum(n, max_exact).astype(F32)
    large = max_exact + (jnp.log(nf / max_exact) / math.log(MAX_DISTANCE / max_exact)
                         * (nb - max_exact)).astype(I32)
    large = jnp.minimum(large, nb - 1)
    return ret + jnp.where(n < max_exact, n, large)


def _bias_tiles(rel_table, tq, tk_prev):
    buckets = jnp.arange(NUM_BUCKETS, dtype=I32)

    def lookup(rel):
        hit = _t5_bucket(rel)[None, :] == buckets[:, None]
        return jnp.sum(jnp.where(hit[:, None, :], rel_table[:, :, None], 0.0), axis=0)

    far = lookup(jnp.full((1,), -4 * MAX_DISTANCE, I32))

    def tile(rel0, tk):
        period = tq + tk
        j = jnp.arange(period, dtype=I32)
        strip = lookup(rel0 + jnp.where(j < tk, j, j - period)) - far
        flat = jnp.tile(strip, (1, tq))[:, :tq * (period - 1)]
        return (flat.reshape(-1, tq, period - 1)[:, :, :tk] * LOG2E).astype(F32)

    return tile(-tk_prev, tk_prev), tile(0, tq)


def _router_kernel(x_ref, of_ref, od_ref, wof_ref, wod_ref, g_ref, wr_ref, br_ref,
                   x1_ref, hn_ref, te_ref, tw_ref, rank_ref, cnt_ref, *, tm):
    step = pl.program_id(0)

    @pl.when(step == 0)
    def _():
        cnt_ref[...] = jnp.zeros_like(cnt_ref)

    x1 = (x_ref[...] + jnp.dot(of_ref[...], wof_ref[...], preferred_element_type=F32)
          + jnp.dot(od_ref[...], wod_ref[...], preferred_element_type=F32))
    x1_ref[...] = x1
    hn = x1 * _rms_scale(x1, NORM_EPS) * g_ref[...]
    half = hn.shape[1] // 2
    hn_ref[...] = _pack_bf16_pairs(hn[:, :half], hn[:, half:])

    hn_hi = hn.astype(BF16)
    hn_lo = (hn - hn_hi.astype(F32)).astype(BF16)
    both = jnp.dot(hn_hi, wr_ref[...], preferred_element_type=F32)
    logits = (both[:, :LANES] + both[:, LANES:] + br_ref[...]
              + jnp.dot(hn_lo, wr_ref[:, :LANES], preferred_element_type=F32))
    lane = lax.broadcasted_iota(I32, logits.shape, 1)
    lane_f = lane.astype(F32)
    sel = jnp.zeros(logits.shape, F32)
    vals, idxs = [], []
    for _ in range(TOP_K):
        mx = jnp.max(logits, axis=1, keepdims=True)
        idx = jnp.min(jnp.where(logits == mx, lane_f, float(LANES)), axis=1, keepdims=True)
        hit = lane_f == idx
        sel = jnp.where(hit, 1.0, sel)
        logits = jnp.where(hit, -jnp.inf, logits)
        vals.append(mx)
        idxs.append(idx)
    exps = [jnp.exp(v - vals[0]) for v in vals]
    denom = exps[0] + exps[1] + exps[2] + exps[3]

    earlier = (lax.broadcasted_iota(I32, (tm, tm), 1)
               < lax.broadcasted_iota(I32, (tm, tm), 0)).astype(BF16)
    before = cnt_ref[...] + jnp.dot(earlier, sel.astype(BF16), preferred_element_type=F32)
    for k in range(TOP_K):
        te_ref[:, k:k + 1] = idxs[k].astype(I32)
        tw_ref[:, k:k + 1] = exps[k] / denom
        rank_ref[:, k:k + 1] = jnp.sum(
            jnp.where(lane_f == idxs[k], before, 0.0), axis=1, keepdims=True).astype(I32)
    cnt_ref[...] += jnp.sum(sel, axis=0, keepdims=True)


def _router(x2d, o_fox, o_diff, wo_f, wo_d, g_ffn, w_r, b_r, *, tm):
    n, d = x2d.shape
    row = lambda i: (i, 0)
    const = lambda i: (0, 0)
    full = lambda arr: pl.BlockSpec(arr.shape, const)
    return pl.pallas_call(
        functools.partial(_router_kernel, tm=tm),
        out_shape=[jax.ShapeDtypeStruct((n, d), F32), jax.ShapeDtypeStruct((n, d // 2), U32),
                   jax.ShapeDtypeStruct((n, TOP_K), I32), jax.ShapeDtypeStruct((n, TOP_K), F32),
                   jax.ShapeDtypeStruct((n, TOP_K), I32), jax.ShapeDtypeStruct((1, LANES), F32)],
        grid=(n // tm,),
        in_specs=[pl.BlockSpec((tm, d), row), pl.BlockSpec((tm, o_fox.shape[1]), row),
                  pl.BlockSpec((tm, o_diff.shape[1]), row), full(wo_f), full(wo_d),
                  full(g_ffn), full(w_r), full(b_r)],
        out_specs=[pl.BlockSpec((tm, d), row), pl.BlockSpec((tm, d // 2), row),
                   pl.BlockSpec((tm, TOP_K), row), pl.BlockSpec((tm, TOP_K), row),
                   pl.BlockSpec((tm, TOP_K), row), pl.BlockSpec((1, LANES), const)],
        compiler_params=_cparams(1),
        name="router",
    )(x2d, o_fox, o_diff, wo_f, wo_d, g_ffn, w_r, b_r)


def _row_copy(src, src_row, dst, dst_row, sem):
    return pltpu.make_async_copy(src.at[pl.ds(src_row, 1), :], dst.at[pl.ds(dst_row, 1), :], sem)


def _dispatch_kernel(tail_start_ref, tail_valid_ref, pos_ref, hn_ref, xs_hbm, zeros_ref, sem,
                     *, tm, tme, n_experts):
    def tail_copy(e):
        return pltpu.make_async_copy(
            zeros_ref, xs_hbm.at[pl.ds(pl.multiple_of(tail_start_ref[e], tme), tme), :], sem.at[0])

    @pl.when(pl.program_id(0) == 0)
    def _():
        zeros_ref[...] = jnp.zeros_like(zeros_ref)
        for e in range(n_experts):
            @pl.when(tail_valid_ref[e] > 0)
            def _():
                tail_copy(e).start()
        for e in range(n_experts):
            @pl.when(tail_valid_ref[e] > 0)
            def _():
                tail_copy(e).wait()

    def start_row(r, carry):
        for k in range(TOP_K):
            _row_copy(hn_ref, r, xs_hbm, pos_ref[r, k], sem.at[1]).start(priority=k % 2)
        return carry

    def wait_row(r, carry):
        for k in range(TOP_K):
            _row_copy(hn_ref, r, xs_hbm, pos_ref[r, k], sem.at[1]).wait()
        return carry

    lax.fori_loop(0, tm, start_row, 0, unroll=8)
    lax.fori_loop(0, tm, wait_row, 0, unroll=8)


def _dispatch(hn_packed, pos, tail_start, tail_valid, *, m_pad, tm, tme):
    n, half = hn_packed.shape
    n_experts = tail_start.shape[0]
    grid_spec = pltpu.PrefetchScalarGridSpec(
        num_scalar_prefetch=2,
        grid=(n // tm,),
        in_specs=[pl.BlockSpec((tm, TOP_K), lambda i, *_: (i, 0), memory_space=pltpu.SMEM),
                  pl.BlockSpec((tm, half), lambda i, *_: (i, 0))],
        out_specs=pl.BlockSpec(memory_space=pl.ANY),
        scratch_shapes=[pltpu.VMEM((tme, half), U32), pltpu.SemaphoreType.DMA((2,))],
    )
    return pl.pallas_call(
        functools.partial(_dispatch_kernel, tm=tm, tme=tme, n_experts=n_experts),
        out_shape=jax.ShapeDtypeStruct((m_pad, half), U32),
        grid_spec=grid_spec,
        compiler_params=pltpu.CompilerParams(
            dimension_semantics=("arbitrary",), vmem_limit_bytes=VMEM_LIMIT_BYTES,
            has_side_effects=True),
        name="dispatch",
    )(tail_start, tail_valid, pos, hn_packed)


def _experts_kernel(te_ref, nact_ref, xs_ref, wg_ref, bg_ref, wu_ref, bu_ref, wd_ref, bd_ref,
                    ys_ref):
    @pl.when(pl.program_id(0) < nact_ref[0])
    def _():
        half = xs_ref.shape[1]
        x_lo, x_hi = _unpack_bf16_pairs(xs_ref[...])
        x_lo = x_lo.astype(BF16)
        x_hi = x_hi.astype(BF16)

        def proj(w_ref, b_ref):
            return (jnp.dot(x_lo, w_ref[:half, :], preferred_element_type=F32)
                    + jnp.dot(x_hi, w_ref[half:, :], preferred_element_type=F32) + b_ref[...])

        g = jnp.minimum(proj(wg_ref, bg_ref), SWIGLU_LIMIT)
        u = jnp.clip(proj(wu_ref, bu_ref), -SWIGLU_LIMIT, SWIGLU_LIMIT)
        a = ((u + 1.0) * (g * jax.nn.sigmoid(SWIGLU_ALPHA * g))).astype(BF16)
        y = jnp.dot(a, wd_ref[...], preferred_element_type=F32) + bd_ref[...]
        ys_ref[...] = _pack_bf16_pairs(y[:, :half], y[:, half:])


def _experts(xs, tile_expert, n_active, w_gate, b_gate, w_up, b_up, w_down, b_down, *, tme):
    m_pad, half = xs.shape
    n_tiles = m_pad // tme
    d, d_ff = w_gate.shape[1:]
    row = lambda t, te, na: (jnp.minimum(t, na[0] - 1), 0)
    wspec = lambda shape: pl.BlockSpec((None,) + shape, lambda t, te, na: (te[t], 0, 0))
    grid_spec = pltpu.PrefetchScalarGridSpec(
        num_scalar_prefetch=2,
        grid=(n_tiles,),
        in_specs=[pl.BlockSpec((tme, half), row),
                  wspec((d, d_ff)), wspec((1, d_ff)), wspec((d, d_ff)), wspec((1, d_ff)),
                  wspec((d_ff, d)), wspec((1, d))],
        out_specs=pl.BlockSpec((tme, half), row),
    )
    return pl.pallas_call(
        _experts_kernel,
        out_shape=jax.ShapeDtypeStruct((m_pad, half), U32),
        grid_spec=grid_spec,
        compiler_params=_cparams(1),
        name="experts",
    )(tile_expert, n_active, xs, w_gate, b_gate, w_up, b_up, w_down, b_down)


def _combine_kernel(pos_ref, pos_next_ref, x1_ref, tw_ref, p_ref, wg_ref, wp_ref, gple_ref,
                    gfin_ref, ys_hbm, y_ref, gbuf, sem, *, tm):
    step = pl.program_id(0)
    slot = step % 2

    def gather(tile_pos_ref, tile_slot, wait):
        def row(r, carry):
            for k in range(TOP_K):
                copy = _row_copy(ys_hbm, tile_pos_ref[r, k], gbuf.at[tile_slot, k], r,
                                 sem.at[tile_slot])
                if wait:
                    copy.wait()
                else:
                    copy.start(priority=k % 2)
            return carry
        lax.fori_loop(0, tm, row, 0, unroll=8)

    @pl.when(step == 0)
    def _():
        gather(pos_ref, 0, wait=False)

    @pl.when(step + 1 < pl.num_programs(0))
    def _():
        gather(pos_next_ref, 1 - slot, wait=False)

    gather(pos_ref, slot, wait=True)

    x1 = x1_ref[...]
    half = x1.shape[1] // 2
    tw = tw_ref[...]
    moe_lo = jnp.zeros((tm, half), F32)
    moe_hi = jnp.zeros((tm, half), F32)
    for k in range(TOP_K):
        lo, hi = _unpack_bf16_pairs(gbuf[slot, k])
        moe_lo = moe_lo + tw[:, k:k + 1] * lo
        moe_hi = moe_hi + tw[:, k:k + 1] * hi
    x2 = x1 + jnp.concatenate([moe_lo, moe_hi], axis=1)
    hp = (x2 * _rms_scale(x2, NORM_EPS) * gple_ref[...]).astype(BF16)
    gate = jax.nn.sigmoid(jnp.dot(hp, wg_ref[...], preferred_element_type=F32))
    proj = jnp.dot(p_ref[...].astype(BF16), wp_ref[...], preferred_element_type=F32)
    x3 = x2 + proj * gate
    y_ref[...] = x3 * _rms_scale(x3, NORM_EPS) * gfin_ref[...]


def _combine(x1, top_w, pos, ys, p2d, w_ple_gate, w_ple_proj, g_ple, g_final, *, tm):
    n, d = x1.shape
    half = ys.shape[1]
    row = lambda i: (i, 0)
    const = lambda i: (0, 0)
    full = lambda arr: pl.BlockSpec(arr.shape, const)
    return pl.pallas_call(
        functools.partial(_combine_kernel, tm=tm),
        out_shape=jax.ShapeDtypeStruct((n, d), F32),
        grid=(n // tm,),
        in_specs=[pl.BlockSpec((tm, TOP_K), row, memory_space=pltpu.SMEM),
                  pl.BlockSpec((tm, TOP_K), lambda i: (jnp.minimum(i + 1, n // tm - 1), 0),
                               memory_space=pltpu.SMEM),
                  pl.BlockSpec((tm, d), row), pl.BlockSpec((tm, TOP_K), row),
                  pl.BlockSpec((tm, p2d.shape[1]), row), full(w_ple_gate), full(w_ple_proj),
                  full(g_ple), full(g_final), pl.BlockSpec(memory_space=pl.ANY)],
        out_specs=pl.BlockSpec((tm, d), row),
        scratch_shapes=[pltpu.VMEM((2, TOP_K, tm, half), U32), pltpu.SemaphoreType.DMA((2,))],
        compiler_params=_cparams(1),
        name="combine",
    )(pos, pos, x1, top_w, p2d, w_ple_gate, w_ple_proj, g_ple, g_final, ys)


def _pick_tile(n, want):
    t = min(n, want)
    assert n % t == 0, (n, t)
    return t


def _layer_group(x, p, past, wts, lambda_init, g_final):
    (g_mix, w_main, w_vt, w_f, b_f, diff_lambda, subln_g, rel_table, wo_f, wo_d, g_ffn, w_r, b_r,
     w_gate, b_gate, w_up, b_up, w_down, b_down, g_ple, w_ple_gate, w_ple_proj) = wts
    b, t, d = x.shape
    n = b * t
    n_fox = (w_main.shape[1] // 6) // HEAD_DIM
    n_experts = w_gate.shape[0]
    x2d = x.reshape(n, d)

    tm = _pick_tile(n, 512)
    (qf, qd, kf, vf, kd, vd, kf16, kd16, vf16, vd16, logf_lanes) = _inproj(
        x2d, g_mix, w_main, w_vt, w_f, b_f, tm=tm, values_transposed=past is None)
    logf = logf_lanes[:, :n_fox]
    width = qf.shape[1]
    as_seq = lambda a: a.reshape(b, t, a.shape[-1])

    tq = _pick_tile(t, 1024)
    rs = _pick_tile(tq, 256)
    if past is None:
        kx = _key_decay(as_seq(logf_lanes), n_fox, width)
        o_fox = _fox_prompt_attention(as_seq(qf), as_seq(kf16), kx, vf16, tq=tq, rs=rs)
        bias_prev, bias_diag = _bias_tiles(rel_table, rs, rs)
        block = jnp.arange(rs, dtype=I32) // CHUNK
        bias_diag = jnp.where(block[:, None] <= block[None, :], jnp.swapaxes(bias_diag, 1, 2),
                              NEG_INF)
        o_diff = _diff_prompt_attention(
            as_seq(qd), as_seq(kd16), vd16, jnp.swapaxes(bias_prev, 1, 2), bias_diag,
            diff_lambda, subln_g.reshape(-1, 1), tq=tq, rs=rs, lambda_init=lambda_init)
    else:
        pk_f, pv_f, plogf, pk_d, pv_d = past
        plen = pk_f.shape[1]
        tkp = 512
        total = plen + t
        padded = -(-total // LANES) * LANES
        seq = jnp.concatenate([jnp.transpose(plogf.astype(F32), (0, 2, 1)),
                               jnp.transpose(as_seq(logf), (0, 2, 1))], axis=2)
        c_all = _cumsum_lanes(jnp.pad(seq, ((0, 0), (0, 0), (0, padded - total))))
        flat = lambda a: a.reshape(b, plen, width)
        o_fox = _fox_cached_attention(as_seq(qf), as_seq(kf16), as_seq(vf16),
                                      c_all[:, :, plen:total], flat(pk_f), flat(pv_f),
                                      c_all[:, :, :plen], tkp=tkp)
        bias_prev, bias_diag = _bias_tiles(rel_table, t, tkp)
        o_diff = _diff_cached_attention(as_seq(qd), as_seq(kd16), as_seq(vd16), bias_prev,
                                        bias_diag, diff_lambda, subln_g, flat(pk_d), flat(pv_d),
                                        tkp=tkp, lambda_init=lambda_init)

    tmr = _pick_tile(n, 256)
    x1, hn_packed, top_e, top_w, rank, counts = _router(
        x2d, o_fox.reshape(n, width), o_diff.reshape(n, width), wo_f, wo_d, g_ffn, w_r, b_r, tm=tmr)

    tme = 512 if n * TOP_K >= 64 * 512 else 256
    n_tiles = -(-(n * TOP_K) // tme) + n_experts
    m_pad = n_tiles * tme
    cnt = counts[0, :n_experts].astype(I32)
    group = -(-cnt // tme) * tme
    ends = jnp.cumsum(group)
    starts = ends - group
    experts = jnp.arange(n_experts, dtype=I32)
    pos = rank + jnp.sum(jnp.where(top_e[:, :, None] == experts, starts, 0), axis=2)
    n_active = (ends[-1] // tme).astype(I32)
    tile_ids = jnp.arange(n_tiles, dtype=I32)
    tile_expert = jnp.sum(jnp.minimum(tile_ids, n_active - 1)[:, None] >= (ends // tme)[None, :],
                          axis=1).astype(I32)
    tile_expert = jnp.minimum(tile_expert, n_experts - 1)
    tail_start = jnp.maximum(ends - tme, 0).astype(I32)
    tail_valid = (group > 0).astype(I32)

    tmd = _pick_tile(n, 256)
    xs = _dispatch(hn_packed, pos, tail_start, tail_valid, m_pad=m_pad, tm=tmd, tme=tme)
    ys = _experts(xs, tile_expert, n_active.reshape(1), w_gate, b_gate, w_up, b_up, w_down, b_down,
                  tme=tme)
    y = _combine(x1, top_w, pos, ys, p.reshape(n, p.shape[-1]), w_ple_gate, w_ple_proj, g_ple,
                 g_final, tm=tmd)
    return y.reshape(b, t, d), (kf, vf, logf, kd, vd)


def kernel(x_prompt, x_sample, p_prompt, p_sample, cache_fox_k, cache_fox_v, cache_fox_logf, cache_diff_k, cache_diff_v, g_mix, w_in, b_forget, diff_lambda, diff_subln_g, rel_bias_table, w_o, g_ffn, w_router, b_router, w_gate, b_gate, w_up, b_up, w_down, b_down, g_ple, w_ple_gate, w_ple_proj, g_final):
    depth = g_mix.shape[0]
    assert depth == 1, "the final norm is fused into the last layer; one layer supported"
    d = x_prompt.shape[-1]
    n_fox = cache_fox_k.shape[3]
    fox_w = n_fox * cache_fox_k.shape[4]
    n_diff = cache_diff_k.shape[3]
    diff_w = n_diff * 2 * cache_diff_k.shape[5]
    assert fox_w == diff_w and fox_w % LANES == 0
    n_experts = w_router.shape[-1]
    assert TOP_K <= n_experts <= LANES
    row2 = lambda a: a.reshape(1, -1).astype(F32)

    i = 0
    lambda_init = 0.8 - 0.6 * math.exp(-0.3 * i)
    off_fk, off_fv, off_ff = fox_w, 2 * fox_w, 3 * fox_w
    off_dq = off_ff + n_fox
    off_dk, off_dv = off_dq + diff_w, off_dq + 2 * diff_w
    w = w_in[i]
    cols = lambda o, wd: w[:, o:o + wd]
    w_main = jnp.concatenate([cols(0, fox_w), cols(off_dq, diff_w), cols(off_fk, fox_w),
                              cols(off_fv, fox_w), cols(off_dk, diff_w), cols(off_dv, diff_w)],
                             axis=1).astype(BF16)
    w_vt = jnp.concatenate([cols(off_fv, fox_w), cols(off_dv, diff_w)], axis=1).T.astype(BF16)
    w_f = jnp.pad(cols(off_ff, n_fox), ((0, 0), (0, LANES - n_fox))).astype(BF16)
    b_f = jnp.pad(row2(b_forget[i]), ((0, 0), (0, LANES - n_fox)))
    w_r = jnp.pad(w_router[i].astype(F32), ((0, 0), (0, LANES - n_experts)))
    w_r_hi = w_r.astype(BF16)
    w_r = jnp.concatenate([w_r_hi, (w_r - w_r_hi.astype(F32)).astype(BF16)], axis=1)
    b_r = jnp.pad(row2(b_router[i]), ((0, 0), (0, LANES - n_experts)), constant_values=NEG_INF)
    wo = w_o[i].astype(BF16)
    bias3 = lambda a: a.reshape(n_experts, 1, -1).astype(F32)
    wts = (row2(g_mix[i]), w_main, w_vt, w_f, b_f, diff_lambda[i].astype(F32), row2(diff_subln_g[i]),
           rel_bias_table.astype(F32), wo[:fox_w], wo[fox_w:], row2(g_ffn[i]), w_r, b_r,
           w_gate[i].astype(BF16), bias3(b_gate[i]), w_up[i].astype(BF16), bias3(b_up[i]),
           w_down[i].astype(BF16), bias3(b_down[i]), row2(g_ple[i]),
           w_ple_gate[i].astype(BF16), w_ple_proj[i].astype(BF16))
    gf = row2(g_final)

    yp, new_p = _layer_group(x_prompt, p_prompt[i], None, wts, lambda_init, gf)
    past_s = (cache_fox_k[i], cache_fox_v[i], cache_fox_logf[i], cache_diff_k[i], cache_diff_v[i])
    ys, new_s = _layer_group(x_sample, p_sample[i], past_s, wts, lambda_init, gf)

    def rows(new, b, t):
        kf, vf, logf, kd, vd = new
        return (kf.reshape(1, b, t, n_fox, -1), vf.reshape(1, b, t, n_fox, -1),
                logf.reshape(1, b, t, n_fox), kd.reshape(1, b, t, n_diff, 2, -1),
                vd.reshape(1, b, t, n_diff, -1))

    bp, tp = x_prompt.shape[:2]
    bs, ts = x_sample.shape[:2]
    return (yp, ys) + rows(new_p, bp, tp) + rows(new_s, bs, ts)
```

```python
import functools
import math

import jax
import jax.numpy as jnp
import numpy as np
from jax import lax
from jax.experimental import pallas as pl
from jax.experimental.pallas import tpu as pltpu

HEAD_DIM = 64
CHUNK = 64
NUM_BUCKETS = 32
MAX_DISTANCE = 128
TOP_K = 4
SWIGLU_LIMIT = 7.0
SWIGLU_ALPHA = 1.702
NORM_EPS = 1e-6
SUBLN_EPS = 1e-5
NEG_INF = -1e30
LOG2E = math.log2(math.e)

LANES = 128
VMEM_LIMIT_BYTES = 56 * 1024 * 1024

F32 = jnp.float32
BF16 = jnp.bfloat16
I32 = jnp.int32
U32 = jnp.uint32


def _cparams(n_axes):
    return pltpu.CompilerParams(
        dimension_semantics=("arbitrary",) * n_axes, vmem_limit_bytes=VMEM_LIMIT_BYTES)


def _rms_scale(x, eps):
    return lax.rsqrt(jnp.mean(x * x, axis=-1, keepdims=True) + eps)


def _pack_bf16_pairs(lo, hi):
    lo_bits = pltpu.bitcast(lo.astype(BF16).astype(F32), U32)
    hi_bits = pltpu.bitcast(hi.astype(BF16).astype(F32), U32)
    return lo_bits | (hi_bits >> 16)


def _unpack_bf16_pairs(packed):
    lo = pltpu.bitcast(packed & jnp.uint32(0xFFFF0000), F32)
    hi = pltpu.bitcast(packed << 16, F32)
    return lo, hi


def _inproj_kernel(x_ref, g_ref, w_ref, wvt_ref, wf_ref, bf_ref,
                   qf_ref, qd_ref, kf_ref, vf_ref, kd_ref, vd_ref,
                   kf16_ref, kd16_ref, vf16_ref, vd16_ref, logf_ref, *, width, values_transposed):
    x = x_ref[...]
    h = (x * _rms_scale(x, NORM_EPS) * g_ref[...]).astype(BF16)
    q_scale = LOG2E / math.sqrt(HEAD_DIM)

    def proj(c):
        return jnp.dot(h, w_ref[:, c * width:(c + 1) * width], preferred_element_type=F32)

    qf_ref[...] = (proj(0) * q_scale).astype(BF16)
    qd_ref[...] = (proj(1) * q_scale).astype(BF16)
    for c, (o32, o16) in enumerate(((kf_ref, kf16_ref), (vf_ref, vf16_ref),
                                    (kd_ref, kd16_ref), (vd_ref, vd16_ref))):
        u = proj(2 + c)
        if len(o32.shape) == 3:
            o32[...] = pltpu.einshape("m(hd)->mhd", u, h=o32.shape[1])
        else:
            o32[...] = pltpu.einshape("m(hed)->mhed", u, h=o32.shape[1], e=o32.shape[2])
        if not (values_transposed and c in (1, 3)):
            o16[...] = u.astype(BF16)
    if values_transposed:
        vf16_ref[...] = _nt_dot(wvt_ref[:width, :], h).astype(BF16)
        vd16_ref[...] = _nt_dot(wvt_ref[width:, :], h).astype(BF16)
    f = jnp.dot(h, wf_ref[...], preferred_element_type=F32) + bf_ref[...]
    logf_ref[...] = jax.nn.log_sigmoid(f)


def _inproj(x2d, g_mix, w_main, w_vt, w_f, b_f, *, tm, values_transposed):
    n, d = x2d.shape
    width = w_main.shape[1] // 6
    row = lambda i: (i, 0)
    const = lambda i: (0, 0)
    if values_transposed:
        v16_shape, v16_spec = (width, n), pl.BlockSpec((width, tm), lambda i: (0, i))
    else:
        v16_shape, v16_spec = (n, width), pl.BlockSpec((tm, width), row)
    head_shapes = [(width // HEAD_DIM, HEAD_DIM)] * 2 + [
        (width // (2 * HEAD_DIM), 2, HEAD_DIM), (width // (2 * HEAD_DIM), 2 * HEAD_DIM)]
    out_shape = ([jax.ShapeDtypeStruct((n, width), BF16)] * 2
                 + [jax.ShapeDtypeStruct((n,) + s, F32) for s in head_shapes]
                 + [jax.ShapeDtypeStruct((n, width), BF16)] * 2
                 + [jax.ShapeDtypeStruct(v16_shape, BF16)] * 2
                 + [jax.ShapeDtypeStruct((n, LANES), F32)])
    out_specs = ([pl.BlockSpec((tm, width), row)] * 2
                 + [pl.BlockSpec((tm,) + s, lambda i, s=s: (i,) + (0,) * len(s))
                    for s in head_shapes]
                 + [pl.BlockSpec((tm, width), row)] * 2 + [v16_spec] * 2
                 + [pl.BlockSpec((tm, LANES), row)])
    return pl.pallas_call(
        functools.partial(_inproj_kernel, width=width, values_transposed=values_transposed),
        out_shape=out_shape,
        grid=(n // tm,),
        in_specs=[pl.BlockSpec((tm, d), row), pl.BlockSpec((1, d), const),
                  pl.BlockSpec(w_main.shape, const), pl.BlockSpec(w_vt.shape, const),
                  pl.BlockSpec(w_f.shape, const), pl.BlockSpec(b_f.shape, const)],
        out_specs=out_specs,
        compiler_params=_cparams(1),
        name="inproj",
    )(x2d, g_mix, w_main, w_vt, w_f, b_f)


def _cumsum_kernel(x_ref, o_ref):
    x = x_ref[...]
    length = x.shape[1]
    lane = lax.broadcasted_iota(I32, x.shape, 1)
    shift = 1
    while shift < length:
        x = x + jnp.where(lane >= shift, pltpu.roll(x, shift, axis=1), 0.0)
        shift *= 2
    o_ref[...] = x


def _cumsum_lanes(x):
    b, h, length = x.shape
    spec = pl.BlockSpec((None, h, length), lambda i: (i, 0, 0))
    return pl.pallas_call(
        _cumsum_kernel, out_shape=jax.ShapeDtypeStruct(x.shape, F32), grid=(b,),
        in_specs=[spec], out_specs=spec, compiler_params=_cparams(1), name="cumsum",
    )(x)


N_DECAY_TERMS = 3


def _key_decay_kernel(x_ref, sel_ref, o_ref):
    x = x_ref[...]
    length = x.shape[0]
    row = lax.broadcasted_iota(I32, x.shape, 0)
    shift = 1
    while shift < length:
        x = x + jnp.where(row >= shift, pltpu.roll(x, shift, axis=0), 0.0)
        shift *= 2
    rest = x * (-LOG2E)
    out = jnp.zeros(o_ref.shape, F32)
    for j in range(N_DECAY_TERMS):
        term = rest.astype(BF16)
        rest = rest - term.astype(F32)
        out = out + jnp.dot(term, sel_ref[j], preferred_element_type=F32)
    o_ref[...] = out.astype(BF16)


def _decay_selectors(n_heads, width):
    sel = np.zeros((N_DECAY_TERMS, LANES, width), np.float32)
    for h in range(n_heads):
        base = (h // 2) * LANES + (HEAD_DIM if h % 2 == 0 else 0)
        for j in range(N_DECAY_TERMS):
            sel[j, h, base + j] = 1.0
    return jnp.asarray(sel, BF16)


def _key_decay(logf, n_heads, width):
    b, length, lanes = logf.shape
    sel = _decay_selectors(n_heads, width)
    return pl.pallas_call(
        _key_decay_kernel, out_shape=jax.ShapeDtypeStruct((b, length, width), BF16), grid=(b,),
        in_specs=[pl.BlockSpec((None, length, lanes), lambda i: (i, 0, 0)),
                  pl.BlockSpec(sel.shape, lambda i: (0, 0, 0))],
        out_specs=pl.BlockSpec((None, length, width), lambda i: (i, 0, 0)),
        compiler_params=_cparams(1), name="key_decay",
    )(logf, sel)


def _nt_dot(a, b):
    return lax.dot_general(a, b, (((1,), (1,)), ((), ())), preferred_element_type=F32)


SCORE_LOOKAHEAD = 4


def _pipelined(units, scores, consume):
    ahead = [scores(unit) for unit in units[:SCORE_LOOKAHEAD]]
    for n, unit in enumerate(units):
        s = ahead.pop(0)
        if n + SCORE_LOOKAHEAD < len(units):
            ahead.append(scores(units[n + SCORE_LOOKAHEAD]))
        consume(unit, s)


def _memo(fn):
    cache = {}

    def get(key):
        if key not in cache:
            cache[key] = fn(key)
        return cache[key]
    return get


def _online_softmax_step(s, v, m_ref, l_ref, acc_ref, h):
    m_old = m_ref[h]
    m_new = jnp.maximum(m_old, jnp.max(s, axis=1, keepdims=True))
    p = jnp.exp2(s - m_new)
    alpha = jnp.exp2(m_old - m_new)
    l_ref[h] = alpha * l_ref[h] + jnp.sum(p, axis=1, keepdims=True)
    acc_ref[h] = alpha * acc_ref[h] + jnp.dot(p.astype(BF16), v, preferred_element_type=F32)
    m_ref[h] = m_new


def _head_rows(ref, h, n_heads, count):
    return ref[pl.ds(h, count, stride=n_heads), :].astype(BF16)


def _fox_cached_kernel(q_ref, kn_ref, vn_ref, cn_ref, cp_ref, kp_ref, vp_ref,
                       o_ref, acc, m, l, *, tkp, n_heads):
    j = pl.program_id(1)
    t = q_ref.shape[0]
    q = q_ref[...]
    lanes = lambda h: slice(h * HEAD_DIM, (h + 1) * HEAD_DIM)

    @pl.when(j == 0)
    def _():
        acc[...] = jnp.zeros_like(acc)
        l[...] = jnp.zeros_like(l)
        m[...] = jnp.full_like(m, NEG_INF)

    def scores(h):
        return _nt_dot(q[:, lanes(h)], _head_rows(kp_ref, h, n_heads, tkp))

    def consume(h, s):
        s = s - cp_ref[h:h + 1, :] * LOG2E
        _online_softmax_step(s, _head_rows(vp_ref, h, n_heads, tkp), m, l, acc, h)

    _pipelined(list(range(n_heads)), scores, consume)

    @pl.when(j == pl.num_programs(1) - 1)
    def _():
        kn = kn_ref[...]
        vn = vn_ref[...]
        causal = lax.broadcasted_iota(I32, (t, t), 1) <= lax.broadcasted_iota(I32, (t, t), 0)
        for h in range(n_heads):
            s = _nt_dot(q[:, lanes(h)], kn[:, lanes(h)]) - cn_ref[h:h + 1, :] * LOG2E
            _online_softmax_step(jnp.where(causal, s, NEG_INF), vn[:, lanes(h)], m, l, acc, h)
            o_ref[:, lanes(h)] = (acc[h] / l[h]).astype(o_ref.dtype)


def _fox_cached_attention(q, k_new, v_new, c_new, k_past, v_past, c_past, *, tkp):
    b, t, w = q.shape
    plen, n_heads, dim = k_past.shape[1:]
    assert plen % tkp == 0 and dim == HEAD_DIM and n_heads * dim == w
    rows = lambda a: a.reshape(b, plen * n_heads, dim)
    seq = pl.BlockSpec((None, t, w), lambda bi, j: (bi, 0, 0))
    cache = pl.BlockSpec((None, tkp * n_heads, dim), lambda bi, j: (bi, j, 0))
    return pl.pallas_call(
        functools.partial(_fox_cached_kernel, tkp=tkp, n_heads=n_heads),
        out_shape=jax.ShapeDtypeStruct((b, t, w), BF16),
        grid=(b, plen // tkp),
        in_specs=[seq, seq, seq, pl.BlockSpec((None, n_heads, t), lambda bi, j: (bi, 0, 0)),
                  pl.BlockSpec((None, n_heads, tkp), lambda bi, j: (bi, 0, j)), cache, cache],
        out_specs=seq,
        scratch_shapes=[pltpu.VMEM((n_heads, t, dim), F32), pltpu.VMEM((n_heads, t, 1), F32),
                        pltpu.VMEM((n_heads, t, 1), F32)],
        compiler_params=_cparams(2),
        name="fox_cached_attn",
    )(q, k_new, v_new, c_new, c_past, rows(k_past), rows(v_past))


def _diff_lambda(dl, lambda_init):
    return (jnp.exp(jnp.sum(dl[0:1] * dl[1:2], axis=1, keepdims=True))
            - jnp.exp(jnp.sum(dl[2:3] * dl[3:4], axis=1, keepdims=True)) + lambda_init)


def _diff_cached_kernel(q_ref, kn_ref, vn_ref, bprev_ref, bdiag_ref, lam_ref, g_ref,
                        kp_ref, vp_ref, o_ref, acc, m, l, *, tkp, n_heads, lambda_init):
    j = pl.program_id(1)
    last = pl.num_programs(1) - 1
    q = q_ref[...]
    lanes = lambda u: slice(u * HEAD_DIM, (u + 1) * HEAD_DIM)
    units = list(range(2 * n_heads))

    @pl.when(j == 0)
    def _():
        acc[...] = jnp.zeros_like(acc)
        l[...] = jnp.zeros_like(l)
        m[...] = jnp.full_like(m, NEG_INF)

    def scores(u):
        return _nt_dot(q[:, lanes(u)], _head_rows(kp_ref, u, 2 * n_heads, tkp))

    def values(u):
        return _head_rows(vp_ref, u // 2, n_heads, tkp)

    @pl.when(j < last)
    def _():
        _pipelined(units, scores, lambda u, s: _online_softmax_step(s, values(u), m, l, acc, u))

    @pl.when(j == last)
    def _():
        def consume(u, s):
            _online_softmax_step(jnp.maximum(s, NEG_INF) + bprev_ref[u // 2], values(u),
                                 m, l, acc, u)
        _pipelined(units, scores, consume)
        kn = kn_ref[...]
        vn = vn_ref[...]
        lam = _diff_lambda(lam_ref[...], lambda_init)
        for h in range(n_heads):
            both = slice(2 * h * HEAD_DIM, (2 * h + 2) * HEAD_DIM)
            for u in (2 * h, 2 * h + 1):
                s = _nt_dot(q[:, lanes(u)], kn[:, lanes(u)])
                _online_softmax_step(jnp.maximum(s, NEG_INF) + bdiag_ref[h], vn[:, both],
                                     m, l, acc, u)
            o = acc[2 * h] / l[2 * h] - lam * (acc[2 * h + 1] / l[2 * h + 1])
            o = o * _rms_scale(o, SUBLN_EPS) * g_ref[...] * (1.0 - lambda_init)
            o_ref[:, both] = o.astype(o_ref.dtype)


def _diff_cached_attention(q, k_new, v_new, bias_prev, bias_diag, diff_lambda, subln_g,
                           k_past, v_past, *, tkp, lambda_init):
    b, t, w = q.shape
    plen, n_heads = k_past.shape[1:3]
    assert plen % tkp == 0 and tkp >= MAX_DISTANCE and t <= CHUNK and plen % CHUNK == 0
    assert k_past.shape[3:] == (2, HEAD_DIM) and n_heads * 2 * HEAD_DIM == w
    seq = pl.BlockSpec((None, t, w), lambda bi, j: (bi, 0, 0))
    full = lambda arr: pl.BlockSpec(arr.shape, lambda bi, j: (0,) * arr.ndim)
    return pl.pallas_call(
        functools.partial(_diff_cached_kernel, tkp=tkp, n_heads=n_heads, lambda_init=lambda_init),
        out_shape=jax.ShapeDtypeStruct((b, t, w), BF16),
        grid=(b, plen // tkp),
        in_specs=[seq, seq, seq, full(bias_prev), full(bias_diag), full(diff_lambda),
                  full(subln_g),
                  pl.BlockSpec((None, tkp * n_heads * 2, HEAD_DIM), lambda bi, j: (bi, j, 0)),
                  pl.BlockSpec((None, tkp * n_heads, 2 * HEAD_DIM), lambda bi, j: (bi, j, 0))],
        out_specs=seq,
        scratch_shapes=[pltpu.VMEM((2 * n_heads, t, 2 * HEAD_DIM), F32),
                        pltpu.VMEM((2 * n_heads, t, 1), F32), pltpu.VMEM((2 * n_heads, t, 1), F32)],
        compiler_params=_cparams(2),
        name="diff_cached_attn",
    )(q, k_new, v_new, bias_prev, bias_diag, diff_lambda, subln_g,
      k_past.reshape(b, plen * n_heads * 2, HEAD_DIM),
      v_past.reshape(b, plen * n_heads, 2 * HEAD_DIM))


def _fox_prompt_kernel(q_ref, k_ref, kx_ref, vt_ref, o_ref, acc_a, acc_b, m_a, m_b, *, tq, rs):
    i = pl.program_id(2)
    n_qs = tq // rs
    lane = lax.broadcasted_iota(I32, (1, LANES), 1)
    lo = lane < HEAD_DIM
    top = lax.broadcasted_iota(I32, (LANES, rs), 0) < HEAD_DIM

    q = q_ref[...].astype(F32)
    ones_a = jnp.where(lane < HEAD_DIM + N_DECAY_TERMS, 1.0, 0.0)
    ones_b = jnp.where(lane < N_DECAY_TERMS, 1.0, 0.0)
    q_heads = (jnp.where(lo, q, ones_a).astype(BF16), jnp.where(lo, ones_b, q).astype(BF16))
    accs = (acc_a, acc_b)
    ms = (m_a, m_b)
    for acc, m in zip(accs, ms):
        acc[...] = jnp.zeros_like(acc)
        m[...] = jnp.full_like(m, NEG_INF)

    def chunk_group(first, units):
        @_memo
        def operands(kc):
            off = pl.multiple_of((first + kc) * rs, rs)
            k = k_ref[pl.ds(off, rs), :]
            kx = kx_ref[pl.ds(off, rs), :]
            vt = vt_ref[:, pl.ds(off, rs)]
            ones_v = jnp.ones_like(vt)
            return ((jnp.where(lo, k, kx), jnp.where(lo, kx, k)),
                    (jnp.where(top, vt, ones_v), jnp.where(top, ones_v, vt)))

        def scores(unit):
            kc, hd, qs, _ = unit
            return _nt_dot(operands(kc)[0][hd], q_heads[hd][qs * rs:(qs + 1) * rs])

        def consume(unit, s):
            kc, hd, qs, mask = unit
            cols = pl.ds(qs * rs, rs)
            if mask is not None:
                s = jnp.where(mask, s, NEG_INF)
            m_old = ms[hd][:, cols]
            m_new = jnp.maximum(m_old, jnp.max(s, axis=0, keepdims=True))
            p = jnp.exp2(s - m_new).astype(BF16)
            accs[hd][:, cols] = jnp.exp2(m_old - m_new) * accs[hd][:, cols] + jnp.dot(
                operands(kc)[1][hd], p, preferred_element_type=F32)
            ms[hd][:, cols] = m_new

        _pipelined(units, scores, consume)

    def below(j, carry):
        chunk_group(n_qs * j, [(kc, hd, qs, None) for kc in range(n_qs) for qs in range(n_qs)
                               for hd in range(2)])
        return carry
    lax.fori_loop(0, i, below, 0)
    causal = (lax.broadcasted_iota(I32, (rs, rs), 0) <= lax.broadcasted_iota(I32, (rs, rs), 1))
    chunk_group(n_qs * i, [(kc, hd, qs, causal if qs == kc else None) for kc in range(n_qs)
                           for qs in range(kc, n_qs) for hd in range(2)])

    a = acc_a[...]
    b = acc_b[...]
    out_t = jnp.concatenate([a[:HEAD_DIM] / a[HEAD_DIM:HEAD_DIM + 1], b[HEAD_DIM:] / b[0:1]],
                            axis=0)
    o_ref[...] = out_t.T.astype(o_ref.dtype)


def _fox_prompt_attention(q, k, kx, vt, *, tq, rs):
    b, t, w = q.shape
    qspec = pl.BlockSpec((None, tq, LANES), lambda bi, p, i: (bi, i, p))
    kspec = pl.BlockSpec((None, t, LANES), lambda bi, p, i: (bi, 0, p))
    return pl.pallas_call(
        functools.partial(_fox_prompt_kernel, tq=tq, rs=rs),
        out_shape=jax.ShapeDtypeStruct((b, t, w), BF16),
        grid=(b, w // LANES, t // tq),
        in_specs=[qspec, kspec, kspec, pl.BlockSpec((LANES, t), lambda bi, p, i: (p, bi))],
        out_specs=qspec,
        scratch_shapes=[pltpu.VMEM((LANES, tq), F32), pltpu.VMEM((LANES, tq), F32),
                        pltpu.VMEM((1, tq), F32), pltpu.VMEM((1, tq), F32)],
        compiler_params=_cparams(3),
        name="fox_prompt_attn",
    )(q, k, kx, vt)


ONES_ROWS = 16


def _diff_prompt_kernel(q_ref, k_ref, vt_ref, bprev_ref, bdiag_ref, lam_ref, g_ref, o_ref,
                        acc0, acc1, m0, m1, *, tq, rs, lambda_init):
    i = pl.program_id(2)
    n_qs = tq // rs
    lo = lax.broadcasted_iota(I32, (1, LANES), 1) < HEAD_DIM

    q = q_ref[...]
    zero = jnp.zeros_like(q)
    q_maps = (jnp.where(lo, q, zero), jnp.where(lo, zero, q))
    accs, ms = (acc0, acc1), (m0, m1)
    for acc, m in zip(accs, ms):
        acc[...] = jnp.zeros_like(acc)
        m[...] = jnp.full_like(m, NEG_INF)

    def chunk_group(first, blocks_ahead):
        @_memo
        def operands(kc):
            off = pl.multiple_of((first + kc) * rs, rs)
            return (k_ref[pl.ds(off, rs), :],
                    jnp.concatenate([vt_ref[:, pl.ds(off, rs)], jnp.ones((ONES_ROWS, rs), BF16)],
                                    axis=0))

        def scores(unit):
            kc, mp, qs = unit
            return _nt_dot(operands(kc)[0], q_maps[mp][qs * rs:(qs + 1) * rs])

        def consume(unit, s):
            kc, mp, qs = unit
            cols = pl.ds(qs * rs, rs)
            distance = blocks_ahead + qs - kc
            if distance == 1:
                s = jnp.maximum(s, NEG_INF) + bprev_ref[...]
            elif distance == 0:
                s = jnp.maximum(s, NEG_INF) + bdiag_ref[...]
            m_old = ms[mp][:, cols]
            m_new = jnp.maximum(m_old, jnp.max(s, axis=0, keepdims=True))
            p = jnp.exp2(s - m_new).astype(BF16)
            accs[mp][:, cols] = jnp.exp2(m_old - m_new) * accs[mp][:, cols] + jnp.dot(
                operands(kc)[1], p, preferred_element_type=F32)
            ms[mp][:, cols] = m_new

        _pipelined([(kc, mp, qs) for kc in range(n_qs) for qs in range(n_qs) for mp in range(2)
                    if blocks_ahead + qs - kc >= 0], scores, consume)

    def far_below(j, carry):
        chunk_group(n_qs * j, 2 * n_qs)
        return carry
    lax.fori_loop(0, jnp.maximum(i - 1, 0), far_below, 0)

    @pl.when(i >= 1)
    def _():
        chunk_group(n_qs * (i - 1), n_qs)

    chunk_group(n_qs * i, 0)

    lam = _diff_lambda(lam_ref[...], lambda_init)
    a0 = acc0[...]
    a1 = acc1[...]
    o = a0[:LANES] / a0[LANES:LANES + 1] - lam * (a1[:LANES] / a1[LANES:LANES + 1])
    scale = lax.rsqrt(jnp.mean(o * o, axis=0, keepdims=True) + SUBLN_EPS)
    o = o * scale * g_ref[...] * (1.0 - lambda_init)
    o_ref[...] = o.T.astype(o_ref.dtype)


def _diff_prompt_attention(q, k, vt, bias_prev, bias_diag, diff_lambda, subln_g, *,
                           tq, rs, lambda_init):
    b, t, w = q.shape
    assert rs >= MAX_DISTANCE
    qspec = pl.BlockSpec((None, tq, LANES), lambda bi, h, i: (bi, i, h))
    head_tile = lambda arr: pl.BlockSpec((None,) + arr.shape[1:], lambda bi, h, i: (h, 0, 0))
    const2 = lambda arr: pl.BlockSpec(arr.shape, lambda bi, h, i: (0, 0))
    return pl.pallas_call(
        functools.partial(_diff_prompt_kernel, tq=tq, rs=rs, lambda_init=lambda_init),
        out_shape=jax.ShapeDtypeStruct((b, t, w), BF16),
        grid=(b, w // LANES, t // tq),
        in_specs=[qspec, pl.BlockSpec((None, t, LANES), lambda bi, h, i: (bi, 0, h)),
                  pl.BlockSpec((LANES, t), lambda bi, h, i: (h, bi)),
                  head_tile(bias_prev), head_tile(bias_diag), const2(diff_lambda),
                  const2(subln_g)],
        out_specs=qspec,
        scratch_shapes=[pltpu.VMEM((LANES + ONES_ROWS, tq), F32)] * 2
        + [pltpu.VMEM((1, tq), F32)] * 2,
        compiler_params=_cparams(3),
        name="diff_prompt_attn",
    )(q, k, vt, bias_prev, bias_diag, diff_lambda, subln_g)


def _t5_bucket(rel):
    nb = NUM_BUCKETS // 2
    max_exact = nb // 2
    ret = jnp.where(rel > 0, nb, 0)
    n = jnp.abs(rel)
    nf = jnp.maximum(n, max_exact).astype(F32)
    large = max_exact + (jnp.log(nf / max_exact) / math.log(MAX_DISTANCE / max_exact)
                         * (nb - max_exact)).astype(I32)
    large = jnp.minimum(large, nb - 1)
    return ret + jnp.where(n < max_exact, n, large)


def _bias_tiles(rel_table, tq, tk_prev):
    buckets = jnp.arange(NUM_BUCKETS, dtype=I32)

    def lookup(rel):
        hit = _t5_bucket(rel)[None, :] == buckets[:, None]
        return jnp.sum(jnp.where(hit[:, None, :], rel_table[:, :, None], 0.0), axis=0)

    far = lookup(jnp.full((1,), -4 * MAX_DISTANCE, I32))

    def tile(rel0, tk):
        period = tq + tk
        j = jnp.arange(period, dtype=I32)
        strip = lookup(rel0 + jnp.where(j < tk, j, j - period)) - far
        flat = jnp.tile(strip, (1, tq))[:, :tq * (period - 1)]
        return (flat.reshape(-1, tq, period - 1)[:, :, :tk] * LOG2E).astype(F32)

    return tile(-tk_prev, tk_prev), tile(0, tq)


def _router_kernel(x_ref, of_ref, od_ref, wof_ref, wod_ref, g_ref, wr_ref, br_ref,
                   x1_ref, hn_ref, te_ref, tw_ref, rank_ref, cnt_ref, *, tm):
    step = pl.program_id(0)

    @pl.when(step == 0)
    def _():
        cnt_ref[...] = jnp.zeros_like(cnt_ref)

    x1 = (x_ref[...] + jnp.dot(of_ref[...], wof_ref[...], preferred_element_type=F32)
          + jnp.dot(od_ref[...], wod_ref[...], preferred_element_type=F32))
    x1_ref[...] = x1
    hn = x1 * _rms_scale(x1, NORM_EPS) * g_ref[...]
    half = hn.shape[1] // 2
    hn_ref[...] = _pack_bf16_pairs(hn[:, :half], hn[:, half:])

    hn_hi = hn.astype(BF16)
    hn_lo = (hn - hn_hi.astype(F32)).astype(BF16)
    both = jnp.dot(hn_hi, wr_ref[...], preferred_element_type=F32)
    logits = (both[:, :LANES] + both[:, LANES:] + br_ref[...]
              + jnp.dot(hn_lo, wr_ref[:, :LANES], preferred_element_type=F32))
    lane = lax.broadcasted_iota(I32, logits.shape, 1)
    lane_f = lane.astype(F32)
    sel = jnp.zeros(logits.shape, F32)
    vals, idxs = [], []
    for _ in range(TOP_K):
        mx = jnp.max(logits, axis=1, keepdims=True)
        idx = jnp.min(jnp.where(logits == mx, lane_f, float(LANES)), axis=1, keepdims=True)
        hit = lane_f == idx
        sel = jnp.where(hit, 1.0, sel)
        logits = jnp.where(hit, -jnp.inf, logits)
        vals.append(mx)
        idxs.append(idx)
    exps = [jnp.exp(v - vals[0]) for v in vals]
    denom = exps[0] + exps[1] + exps[2] + exps[3]

    earlier = (lax.broadcasted_iota(I32, (tm, tm), 1)
               < lax.broadcasted_iota(I32, (tm, tm), 0)).astype(BF16)
    before = cnt_ref[...] + jnp.dot(earlier, sel.astype(BF16), preferred_element_type=F32)
    for k in range(TOP_K):
        te_ref[:, k:k + 1] = idxs[k].astype(I32)
        tw_ref[:, k:k + 1] = exps[k] / denom
        rank_ref[:, k:k + 1] = jnp.sum(
            jnp.where(lane_f == idxs[k], before, 0.0), axis=1, keepdims=True).astype(I32)
    cnt_ref[...] += jnp.sum(sel, axis=0, keepdims=True)


def _router(x2d, o_fox, o_diff, wo_f, wo_d, g_ffn, w_r, b_r, *, tm):
    n, d = x2d.shape
    row = lambda i: (i, 0)
    const = lambda i: (0, 0)
    full = lambda arr: pl.BlockSpec(arr.shape, const)
    return pl.pallas_call(
        functools.partial(_router_kernel, tm=tm),
        out_shape=[jax.ShapeDtypeStruct((n, d), F32), jax.ShapeDtypeStruct((n, d // 2), U32),
                   jax.ShapeDtypeStruct((n, TOP_K), I32), jax.ShapeDtypeStruct((n, TOP_K), F32),
                   jax.ShapeDtypeStruct((n, TOP_K), I32), jax.ShapeDtypeStruct((1, LANES), F32)],
        grid=(n // tm,),
        in_specs=[pl.BlockSpec((tm, d), row), pl.BlockSpec((tm, o_fox.shape[1]), row),
                  pl.BlockSpec((tm, o_diff.shape[1]), row), full(wo_f), full(wo_d),
                  full(g_ffn), full(w_r), full(b_r)],
        out_specs=[pl.BlockSpec((tm, d), row), pl.BlockSpec((tm, d // 2), row),
                   pl.BlockSpec((tm, TOP_K), row), pl.BlockSpec((tm, TOP_K), row),
                   pl.BlockSpec((tm, TOP_K), row), pl.BlockSpec((1, LANES), const)],
        compiler_params=_cparams(1),
        name="router",
    )(x2d, o_fox, o_diff, wo_f, wo_d, g_ffn, w_r, b_r)


def _row_copy(src, src_row, dst, dst_row, sem):
    return pltpu.make_async_copy(src.at[pl.ds(src_row, 1), :], dst.at[pl.ds(dst_row, 1), :], sem)


def _dispatch_kernel(tail_start_ref, tail_valid_ref, pos_ref, hn_ref, xs_hbm, zeros_ref, sem,
                     *, tm, tme, n_experts):
    def tail_copy(e):
        return pltpu.make_async_copy(
            zeros_ref, xs_hbm.at[pl.ds(pl.multiple_of(tail_start_ref[e], tme), tme), :], sem.at[0])

    @pl.when(pl.program_id(0) == 0)
    def _():
        zeros_ref[...] = jnp.zeros_like(zeros_ref)
        for e in range(n_experts):
            @pl.when(tail_valid_ref[e] > 0)
            def _():
                tail_copy(e).start()
        for e in range(n_experts):
            @pl.when(tail_valid_ref[e] > 0)
            def _():
                tail_copy(e).wait()

    def start_row(r, carry):
        for k in range(TOP_K):
            _row_copy(hn_ref, r, xs_hbm, pos_ref[r, k], sem.at[1]).start(priority=k % 2)
        return carry

    def wait_row(r, carry):
        for k in range(TOP_K):
            _row_copy(hn_ref, r, xs_hbm, pos_ref[r, k], sem.at[1]).wait()
        return carry

    lax.fori_loop(0, tm, start_row, 0, unroll=8)
    lax.fori_loop(0, tm, wait_row, 0, unroll=8)


def _dispatch(hn_packed, pos, tail_start, tail_valid, *, m_pad, tm, tme):
    n, half = hn_packed.shape
    n_experts = tail_start.shape[0]
    grid_spec = pltpu.PrefetchScalarGridSpec(
        num_scalar_prefetch=2,
        grid=(n // tm,),
        in_specs=[pl.BlockSpec((tm, TOP_K), lambda i, *_: (i, 0), memory_space=pltpu.SMEM),
                  pl.BlockSpec((tm, half), lambda i, *_: (i, 0))],
        out_specs=pl.BlockSpec(memory_space=pl.ANY),
        scratch_shapes=[pltpu.VMEM((tme, half), U32), pltpu.SemaphoreType.DMA((2,))],
    )
    return pl.pallas_call(
        functools.partial(_dispatch_kernel, tm=tm, tme=tme, n_experts=n_experts),
        out_shape=jax.ShapeDtypeStruct((m_pad, half), U32),
        grid_spec=grid_spec,
        compiler_params=pltpu.CompilerParams(
            dimension_semantics=("arbitrary",), vmem_limit_bytes=VMEM_LIMIT_BYTES,
            has_side_effects=True),
        name="dispatch",
    )(tail_start, tail_valid, pos, hn_packed)


def _experts_kernel(te_ref, nact_ref, xs_ref, wg_ref, bg_ref, wu_ref, bu_ref, wd_ref, bd_ref,
                    ys_ref):
    @pl.when(pl.program_id(0) < nact_ref[0])
    def _():
        half = xs_ref.shape[1]
        x_lo, x_hi = _unpack_bf16_pairs(xs_ref[...])
        x_lo = x_lo.astype(BF16)
        x_hi = x_hi.astype(BF16)

        def proj(w_ref, b_ref):
            return (jnp.dot(x_lo, w_ref[:half, :], preferred_element_type=F32)
                    + jnp.dot(x_hi, w_ref[half:, :], preferred_element_type=F32) + b_ref[...])

        g = jnp.minimum(proj(wg_ref, bg_ref), SWIGLU_LIMIT)
        u = jnp.clip(proj(wu_ref, bu_ref), -SWIGLU_LIMIT, SWIGLU_LIMIT)
        a = ((u + 1.0) * (g * jax.nn.sigmoid(SWIGLU_ALPHA * g))).astype(BF16)
        y = jnp.dot(a, wd_ref[...], preferred_element_type=F32) + bd_ref[...]
        ys_ref[...] = _pack_bf16_pairs(y[:, :half], y[:, half:])


def _experts(xs, tile_expert, n_active, w_gate, b_gate, w_up, b_up, w_down, b_down, *, tme):
    m_pad, half = xs.shape
    n_tiles = m_pad // tme
    d, d_ff = w_gate.shape[1:]
    row = lambda t, te, na: (jnp.minimum(t, na[0] - 1), 0)
    wspec = lambda shape: pl.BlockSpec((None,) + shape, lambda t, te, na: (te[t], 0, 0))
    grid_spec = pltpu.PrefetchScalarGridSpec(
        num_scalar_prefetch=2,
        grid=(n_tiles,),
        in_specs=[pl.BlockSpec((tme, half), row),
                  wspec((d, d_ff)), wspec((1, d_ff)), wspec((d, d_ff)), wspec((1, d_ff)),
                  wspec((d_ff, d)), wspec((1, d))],
        out_specs=pl.BlockSpec((tme, half), row),
    )
    return pl.pallas_call(
        _experts_kernel,
        out_shape=jax.ShapeDtypeStruct((m_pad, half), U32),
        grid_spec=grid_spec,
        compiler_params=_cparams(1),
        name="experts",
    )(tile_expert, n_active, xs, w_gate, b_gate, w_up, b_up, w_down, b_down)


def _combine_kernel(pos_ref, pos_next_ref, x1_ref, tw_ref, p_ref, wg_ref, wp_ref, gple_ref,
                    gfin_ref, ys_hbm, y_ref, gbuf, sem, *, tm):
    step = pl.program_id(0)
    slot = step % 2

    def gather(tile_pos_ref, tile_slot, wait):
        def row(r, carry):
            for k in range(TOP_K):
                copy = _row_copy(ys_hbm, tile_pos_ref[r, k], gbuf.at[tile_slot, k], r,
                                 sem.at[tile_slot])
                if wait:
                    copy.wait()
                else:
                    copy.start(priority=k % 2)
            return carry
        lax.fori_loop(0, tm, row, 0, unroll=8)

    @pl.when(step == 0)
    def _():
        gather(pos_ref, 0, wait=False)

    @pl.when(step + 1 < pl.num_programs(0))
    def _():
        gather(pos_next_ref, 1 - slot, wait=False)

    gather(pos_ref, slot, wait=True)

    x1 = x1_ref[...]
    half = x1.shape[1] // 2
    tw = tw_ref[...]
    moe_lo = jnp.zeros((tm, half), F32)
    moe_hi = jnp.zeros((tm, half), F32)
    for k in range(TOP_K):
        lo, hi = _unpack_bf16_pairs(gbuf[slot, k])
        moe_lo = moe_lo + tw[:, k:k + 1] * lo
        moe_hi = moe_hi + tw[:, k:k + 1] * hi
    x2 = x1 + jnp.concatenate([moe_lo, moe_hi], axis=1)
    hp = (x2 * _rms_scale(x2, NORM_EPS) * gple_ref[...]).astype(BF16)
    gate = jax.nn.sigmoid(jnp.dot(hp, wg_ref[...], preferred_element_type=F32))
    proj = jnp.dot(p_ref[...].astype(BF16), wp_ref[...], preferred_element_type=F32)
    x3 = x2 + proj * gate
    y_ref[...] = x3 * _rms_scale(x3, NORM_EPS) * gfin_ref[...]


def _combine(x1, top_w, pos, ys, p2d, w_ple_gate, w_ple_proj, g_ple, g_final, *, tm):
    n, d = x1.shape
    half = ys.shape[1]
    row = lambda i: (i, 0)
    const = lambda i: (0, 0)
    full = lambda arr: pl.BlockSpec(arr.shape, const)
    return pl.pallas_call(
        functools.partial(_combine_kernel, tm=tm),
        out_shape=jax.ShapeDtypeStruct((n, d), F32),
        grid=(n // tm,),
        in_specs=[pl.BlockSpec((tm, TOP_K), row, memory_space=pltpu.SMEM),
                  pl.BlockSpec((tm, TOP_K), lambda i: (jnp.minimum(i + 1, n // tm - 1), 0),
                               memory_space=pltpu.SMEM),
                  pl.BlockSpec((tm, d), row), pl.BlockSpec((tm, TOP_K), row),
                  pl.BlockSpec((tm, p2d.shape[1]), row), full(w_ple_gate), full(w_ple_proj),
                  full(g_ple), full(g_final), pl.BlockSpec(memory_space=pl.ANY)],
        out_specs=pl.BlockSpec((tm, d), row),
        scratch_shapes=[pltpu.VMEM((2, TOP_K, tm, half), U32), pltpu.SemaphoreType.DMA((2,))],
        compiler_params=_cparams(1),
        name="combine",
    )(pos, pos, x1, top_w, p2d, w_ple_gate, w_ple_proj, g_ple, g_final, ys)


def _pick_tile(n, want):
    t = min(n, want)
    assert n % t == 0, (n, t)
    return t


def _layer_group(x, p, past, wts, lambda_init, g_final):
    (g_mix, w_main, w_vt, w_f, b_f, diff_lambda, subln_g, rel_table, wo_f, wo_d, g_ffn, w_r, b_r,
     w_gate, b_gate, w_up, b_up, w_down, b_down, g_ple, w_ple_gate, w_ple_proj) = wts
    b, t, d = x.shape
    n = b * t
    n_fox = (w_main.shape[1] // 6) // HEAD_DIM
    n_experts = w_gate.shape[0]
    x2d = x.reshape(n, d)

    tm = _pick_tile(n, 512)
    (qf, qd, kf, vf, kd, vd, kf16, kd16, vf16, vd16, logf_lanes) = _inproj(
        x2d, g_mix, w_main, w_vt, w_f, b_f, tm=tm, values_transposed=past is None)
    logf = logf_lanes[:, :n_fox]
    width = qf.shape[1]
    as_seq = lambda a: a.reshape(b, t, a.shape[-1])

    tq = _pick_tile(t, 1024)
    rs = _pick_tile(tq, 256)
    if past is None:
        kx = _key_decay(as_seq(logf_lanes), n_fox, width)
        o_fox = _fox_prompt_attention(as_seq(qf), as_seq(kf16), kx, vf16, tq=tq, rs=rs)
        bias_prev, bias_diag = _bias_tiles(rel_table, rs, rs)
        block = jnp.arange(rs, dtype=I32) // CHUNK
        bias_diag = jnp.where(block[:, None] <= block[None, :], jnp.swapaxes(bias_diag, 1, 2),
                              NEG_INF)
        o_diff = _diff_prompt_attention(
            as_seq(qd), as_seq(kd16), vd16, jnp.swapaxes(bias_prev, 1, 2), bias_diag,
            diff_lambda, subln_g.reshape(-1, 1), tq=tq, rs=rs, lambda_init=lambda_init)
    else:
        pk_f, pv_f, plogf, pk_d, pv_d = past
        plen = pk_f.shape[1]
        tkp = 512
        total = plen + t
        padded = -(-total // LANES) * LANES
        seq = jnp.concatenate([jnp.transpose(plogf.astype(F32), (0, 2, 1)),
                               jnp.transpose(as_seq(logf), (0, 2, 1))], axis=2)
        c_all = _cumsum_lanes(jnp.pad(seq, ((0, 0), (0, 0), (0, padded - total))))
        o_fox = _fox_cached_attention(as_seq(qf), as_seq(kf16), as_seq(vf16),
                                      c_all[:, :, plen:total], pk_f, pv_f, c_all[:, :, :plen],
                                      tkp=tkp)
        bias_prev, bias_diag = _bias_tiles(rel_table, t, tkp)
        o_diff = _diff_cached_attention(as_seq(qd), as_seq(kd16), as_seq(vd16), bias_prev,
                                        bias_diag, diff_lambda, subln_g, pk_d, pv_d, tkp=tkp,
                                        lambda_init=lambda_init)

    tmr = _pick_tile(n, 256)
    x1, hn_packed, top_e, top_w, rank, counts = _router(
        x2d, o_fox.reshape(n, width), o_diff.reshape(n, width), wo_f, wo_d, g_ffn, w_r, b_r, tm=tmr)

    tme = 512 if n * TOP_K >= 64 * 512 else 256
    n_tiles = -(-(n * TOP_K) // tme) + n_experts
    m_pad = n_tiles * tme
    cnt = counts[0, :n_experts].astype(I32)
    group = -(-cnt // tme) * tme
    ends = jnp.cumsum(group)
    starts = ends - group
    experts = jnp.arange(n_experts, dtype=I32)
    pos = rank + jnp.sum(jnp.where(top_e[:, :, None] == experts, starts, 0), axis=2)
    n_active = (ends[-1] // tme).astype(I32)
    tile_ids = jnp.arange(n_tiles, dtype=I32)
    tile_expert = jnp.sum(jnp.minimum(tile_ids, n_active - 1)[:, None] >= (ends // tme)[None, :],
                          axis=1).astype(I32)
    tile_expert = jnp.minimum(tile_expert, n_experts - 1)
    tail_start = jnp.maximum(ends - tme, 0).astype(I32)
    tail_valid = (group > 0).astype(I32)

    tmd = _pick_tile(n, 256)
    xs = _dispatch(hn_packed, pos, tail_start, tail_valid, m_pad=m_pad, tm=tmd, tme=tme)
    ys = _experts(xs, tile_expert, n_active.reshape(1), w_gate, b_gate, w_up, b_up, w_down, b_down,
                  tme=tme)
    y = _combine(x1, top_w, pos, ys, p.reshape(n, p.shape[-1]), w_ple_gate, w_ple_proj, g_ple,
                 g_final, tm=tmd)
    return y.reshape(b, t, d), (kf, vf, logf, kd, vd)


def kernel(x_prompt, x_sample, p_prompt, p_sample, cache_fox_k, cache_fox_v, cache_fox_logf, cache_diff_k, cache_diff_v, g_mix, w_in, b_forget, diff_lambda, diff_subln_g, rel_bias_table, w_o, g_ffn, w_router, b_router, w_gate, b_gate, w_up, b_up, w_down, b_down, g_ple, w_ple_gate, w_ple_proj, g_final):
    depth = g_mix.shape[0]
    assert depth == 1, "the final norm is fused into the last layer; one layer supported"
    d = x_prompt.shape[-1]
    n_fox = cache_fox_k.shape[3]
    fox_w = n_fox * cache_fox_k.shape[4]
    n_diff = cache_diff_k.shape[3]
    diff_w = n_diff * 2 * cache_diff_k.shape[5]
    assert fox_w == diff_w and fox_w % LANES == 0
    n_experts = w_router.shape[-1]
    assert TOP_K <= n_experts <= LANES
    row2 = lambda a: a.reshape(1, -1).astype(F32)

    i = 0
    lambda_init = 0.8 - 0.6 * math.exp(-0.3 * i)
    off_fk, off_fv, off_ff = fox_w, 2 * fox_w, 3 * fox_w
    off_dq = off_ff + n_fox
    off_dk, off_dv = off_dq + diff_w, off_dq + 2 * diff_w
    w = w_in[i]
    cols = lambda o, wd: w[:, o:o + wd]
    w_main = jnp.concatenate([cols(0, fox_w), cols(off_dq, diff_w), cols(off_fk, fox_w),
                              cols(off_fv, fox_w), cols(off_dk, diff_w), cols(off_dv, diff_w)],
                             axis=1).astype(BF16)
    w_vt = jnp.concatenate([cols(off_fv, fox_w), cols(off_dv, diff_w)], axis=1).T.astype(BF16)
    w_f = jnp.pad(cols(off_ff, n_fox), ((0, 0), (0, LANES - n_fox))).astype(BF16)
    b_f = jnp.pad(row2(b_forget[i]), ((0, 0), (0, LANES - n_fox)))
    w_r = jnp.pad(w_router[i].astype(F32), ((0, 0), (0, LANES - n_experts)))
    w_r_hi = w_r.astype(BF16)
    w_r = jnp.concatenate([w_r_hi, (w_r - w_r_hi.astype(F32)).astype(BF16)], axis=1)
    b_r = jnp.pad(row2(b_router[i]), ((0, 0), (0, LANES - n_experts)), constant_values=NEG_INF)
    wo = w_o[i].astype(BF16)
    bias3 = lambda a: a.reshape(n_experts, 1, -1).astype(F32)
    wts = (row2(g_mix[i]), w_main, w_vt, w_f, b_f, diff_lambda[i].astype(F32), row2(diff_subln_g[i]),
           rel_bias_table.astype(F32), wo[:fox_w], wo[fox_w:], row2(g_ffn[i]), w_r, b_r,
           w_gate[i].astype(BF16), bias3(b_gate[i]), w_up[i].astype(BF16), bias3(b_up[i]),
           w_down[i].astype(BF16), bias3(b_down[i]), row2(g_ple[i]),
           w_ple_gate[i].astype(BF16), w_ple_proj[i].astype(BF16))
    gf = row2(g_final)

    yp, new_p = _layer_group(x_prompt, p_prompt[i], None, wts, lambda_init, gf)
    past_s = (cache_fox_k[i], cache_fox_v[i], cache_fox_logf[i], cache_diff_k[i], cache_diff_v[i])
    ys, new_s = _layer_group(x_sample, p_sample[i], past_s, wts, lambda_init, gf)

    def rows(new, b, t):
        kf, vf, logf, kd, vd = new
        return (kf.reshape(1, b, t, n_fox, -1), vf.reshape(1, b, t, n_fox, -1),
                logf.reshape(1, b, t, n_fox), kd.reshape(1, b, t, n_diff, 2, -1),
                vd.reshape(1, b, t, n_diff, -1))

    bp, tp = x_prompt.shape[:2]
    bs, ts = x_sample.shape[:2]
    return (yp, ys) + rows(new_p, bp, tp) + rows(new_s, bs, ts)
```

```python
import functools
import math

import jax
import jax.numpy as jnp
import numpy as np
from jax import lax
from jax.experimental import pallas as pl
from jax.experimental.pallas import tpu as pltpu

HEAD_DIM = 64
CHUNK = 64
NUM_BUCKETS = 32
MAX_DISTANCE = 128
TOP_K = 4
SWIGLU_LIMIT = 7.0
SWIGLU_ALPHA = 1.702
NORM_EPS = 1e-6
SUBLN_EPS = 1e-5
NEG_INF = -1e30
LOG2E = math.log2(math.e)

LANES = 128
VMEM_LIMIT_BYTES = 56 * 1024 * 1024

F32 = jnp.float32
BF16 = jnp.bfloat16
I32 = jnp.int32
U32 = jnp.uint32


def _cparams(n_axes):
    return pltpu.CompilerParams(
        dimension_semantics=("arbitrary",) * n_axes, vmem_limit_bytes=VMEM_LIMIT_BYTES)


def _rms_scale(x, eps):
    return lax.rsqrt(jnp.mean(x * x, axis=-1, keepdims=True) + eps)


def _pack_bf16_pairs(lo, hi):
    lo_bits = pltpu.bitcast(lo.astype(BF16).astype(F32), U32)
    hi_bits = pltpu.bitcast(hi.astype(BF16).astype(F32), U32)
    return lo_bits | (hi_bits >> 16)


def _unpack_bf16_pairs(packed):
    lo = pltpu.bitcast(packed & jnp.uint32(0xFFFF0000), F32)
    hi = pltpu.bitcast(packed << 16, F32)
    return lo, hi


def _inproj_kernel(x_ref, g_ref, w_ref, wvt_ref, wf_ref, bf_ref,
                   qf_ref, qd_ref, kf_ref, vf_ref, kd_ref, vd_ref,
                   kf16_ref, kd16_ref, vf16_ref, vd16_ref, logf_ref, *, width, values_transposed):
    x = x_ref[...]
    h = (x * _rms_scale(x, NORM_EPS) * g_ref[...]).astype(BF16)
    q_scale = LOG2E / math.sqrt(HEAD_DIM)

    def proj(c):
        return jnp.dot(h, w_ref[:, c * width:(c + 1) * width], preferred_element_type=F32)

    qf_ref[...] = (proj(0) * q_scale).astype(BF16)
    qd_ref[...] = (proj(1) * q_scale).astype(BF16)
    for c, (o32, o16) in enumerate(((kf_ref, kf16_ref), (vf_ref, vf16_ref),
                                    (kd_ref, kd16_ref), (vd_ref, vd16_ref))):
        u = proj(2 + c)
        if len(o32.shape) == 3:
            o32[...] = pltpu.einshape("m(hd)->mhd", u, h=o32.shape[1])
        else:
            o32[...] = pltpu.einshape("m(hed)->mhed", u, h=o32.shape[1], e=o32.shape[2])
        if not (values_transposed and c in (1, 3)):
            o16[...] = u.astype(BF16)
    if values_transposed:
        vf16_ref[...] = _nt_dot(wvt_ref[:width, :], h).astype(BF16)
        vd16_ref[...] = _nt_dot(wvt_ref[width:, :], h).astype(BF16)
    f = jnp.dot(h, wf_ref[...], preferred_element_type=F32) + bf_ref[...]
    logf_ref[...] = jax.nn.log_sigmoid(f)


def _inproj(x2d, g_mix, w_main, w_vt, w_f, b_f, *, tm, values_transposed):
    n, d = x2d.shape
    width = w_main.shape[1] // 6
    row = lambda i: (i, 0)
    const = lambda i: (0, 0)
    if values_transposed:
        v16_shape, v16_spec = (width, n), pl.BlockSpec((width, tm), lambda i: (0, i))
    else:
        v16_shape, v16_spec = (n, width), pl.BlockSpec((tm, width), row)
    head_shapes = [(width // HEAD_DIM, HEAD_DIM)] * 2 + [
        (width // (2 * HEAD_DIM), 2, HEAD_DIM), (width // (2 * HEAD_DIM), 2 * HEAD_DIM)]
    out_shape = ([jax.ShapeDtypeStruct((n, width), BF16)] * 2
                 + [jax.ShapeDtypeStruct((n,) + s, F32) for s in head_shapes]
                 + [jax.ShapeDtypeStruct((n, width), BF16)] * 2
                 + [jax.ShapeDtypeStruct(v16_shape, BF16)] * 2
                 + [jax.ShapeDtypeStruct((n, LANES), F32)])
    out_specs = ([pl.BlockSpec((tm, width), row)] * 2
                 + [pl.BlockSpec((tm,) + s, lambda i, s=s: (i,) + (0,) * len(s))
                    for s in head_shapes]
                 + [pl.BlockSpec((tm, width), row)] * 2 + [v16_spec] * 2
                 + [pl.BlockSpec((tm, LANES), row)])
    return pl.pallas_call(
        functools.partial(_inproj_kernel, width=width, values_transposed=values_transposed),
        out_shape=out_shape,
        grid=(n // tm,),
        in_specs=[pl.BlockSpec((tm, d), row), pl.BlockSpec((1, d), const),
                  pl.BlockSpec(w_main.shape, const), pl.BlockSpec(w_vt.shape, const),
                  pl.BlockSpec(w_f.shape, const), pl.BlockSpec(b_f.shape, const)],
        out_specs=out_specs,
        compiler_params=_cparams(1),
        name="inproj",
    )(x2d, g_mix, w_main, w_vt, w_f, b_f)


def _cumsum_kernel(x_ref, o_ref):
    x = x_ref[...]
    length = x.shape[1]
    lane = lax.broadcasted_iota(I32, x.shape, 1)
    shift = 1
    while shift < length:
        x = x + jnp.where(lane >= shift, pltpu.roll(x, shift, axis=1), 0.0)
        shift *= 2
    o_ref[...] = x


def _cumsum_lanes(x):
    b, h, length = x.shape
    spec = pl.BlockSpec((None, h, length), lambda i: (i, 0, 0))
    return pl.pallas_call(
        _cumsum_kernel, out_shape=jax.ShapeDtypeStruct(x.shape, F32), grid=(b,),
        in_specs=[spec], out_specs=spec, compiler_params=_cparams(1), name="cumsum",
    )(x)


N_DECAY_TERMS = 3


def _key_decay_kernel(x_ref, sel_ref, o_ref):
    x = x_ref[...]
    length = x.shape[0]
    row = lax.broadcasted_iota(I32, x.shape, 0)
    shift = 1
    while shift < length:
        x = x + jnp.where(row >= shift, pltpu.roll(x, shift, axis=0), 0.0)
        shift *= 2
    rest = x * (-LOG2E)
    out = jnp.zeros(o_ref.shape, F32)
    for j in range(N_DECAY_TERMS):
        term = rest.astype(BF16)
        rest = rest - term.astype(F32)
        out = out + jnp.dot(term, sel_ref[j], preferred_element_type=F32)
    o_ref[...] = out.astype(BF16)


def _decay_selectors(n_heads, width):
    sel = np.zeros((N_DECAY_TERMS, LANES, width), np.float32)
    for h in range(n_heads):
        base = (h // 2) * LANES + (HEAD_DIM if h % 2 == 0 else 0)
        for j in range(N_DECAY_TERMS):
            sel[j, h, base + j] = 1.0
    return jnp.asarray(sel, BF16)


def _key_decay(logf, n_heads, width):
    b, length, lanes = logf.shape
    sel = _decay_selectors(n_heads, width)
    return pl.pallas_call(
        _key_decay_kernel, out_shape=jax.ShapeDtypeStruct((b, length, width), BF16), grid=(b,),
        in_specs=[pl.BlockSpec((None, length, lanes), lambda i: (i, 0, 0)),
                  pl.BlockSpec(sel.shape, lambda i: (0, 0, 0))],
        out_specs=pl.BlockSpec((None, length, width), lambda i: (i, 0, 0)),
        compiler_params=_cparams(1), name="key_decay",
    )(logf, sel)


def _nt_dot(a, b):
    return lax.dot_general(a, b, (((1,), (1,)), ((), ())), preferred_element_type=F32)


SCORE_LOOKAHEAD = 4


def _pipelined(units, scores, consume):
    ahead = [scores(unit) for unit in units[:SCORE_LOOKAHEAD]]
    for n, unit in enumerate(units):
        s = ahead.pop(0)
        if n + SCORE_LOOKAHEAD < len(units):
            ahead.append(scores(units[n + SCORE_LOOKAHEAD]))
        consume(unit, s)


def _memo(fn):
    cache = {}

    def get(key):
        if key not in cache:
            cache[key] = fn(key)
        return cache[key]
    return get


CACHED_CHUNK_GROUP = 2


def _fox_cached_kernel(q_ref, kn_ref, vn_ref, cn_ref, cp_ref, kp_ref, vp_ref,
                       o_ref, acc_a, acc_b, m_a, m_b, *, tkp, n_past):
    t = q_ref.shape[0]
    lo = lax.broadcasted_iota(I32, (1, LANES), 1) < HEAD_DIM
    q = q_ref[...]
    zero = jnp.zeros_like(q)
    q_heads = (jnp.where(lo, q, zero), jnp.where(lo, zero, q))
    accs = (acc_a, acc_b)
    ms = (m_a, m_b)
    for acc, m in zip(accs, ms):
        acc[...] = jnp.zeros_like(acc)
        m[...] = jnp.full_like(m, NEG_INF)

    def run(chunks):
        @_memo
        def operands(n):
            k, v, c_rows, _ = chunks[n]
            v = v.astype(BF16)
            one = jnp.ones_like(v)
            return k.astype(BF16), (jnp.where(lo, v, one), jnp.where(lo, one, v)), c_rows * LOG2E

        def scores(unit):
            n, hd = unit
            return _nt_dot(q_heads[hd], operands(n)[0])

        def consume(unit, s):
            n, hd = unit
            _, v_heads, c2 = operands(n)
            s = s - c2[hd:hd + 1, :]
            if chunks[n][3] is not None:
                s = jnp.where(chunks[n][3], s, NEG_INF)
            m_old = ms[hd][...]
            m_new = jnp.maximum(m_old, jnp.max(s, axis=1, keepdims=True))
            p = jnp.exp2(s - m_new).astype(BF16)
            accs[hd][...] = jnp.exp2(m_old - m_new) * accs[hd][...] + jnp.dot(
                p, v_heads[hd], preferred_element_type=F32)
            ms[hd][...] = m_new

        _pipelined([(n, hd) for n in range(len(chunks)) for hd in range(2)], scores, consume)

    def cached(c):
        off = pl.multiple_of(c * tkp, tkp)
        return (kp_ref[pl.ds(off, tkp), :], vp_ref[pl.ds(off, tkp), :],
                cp_ref[:, pl.ds(off, tkp)], None)

    def past_body(j, carry):
        run([cached(j * CACHED_CHUNK_GROUP + g) for g in range(CACHED_CHUNK_GROUP)])
        return carry
    lax.fori_loop(0, n_past // CACHED_CHUNK_GROUP, past_body, 0)
    causal = lax.broadcasted_iota(I32, (t, t), 1) <= lax.broadcasted_iota(I32, (t, t), 0)
    run([cached(c) for c in range(n_past - n_past % CACHED_CHUNK_GROUP, n_past)]
        + [(kn_ref[...], vn_ref[...], cn_ref[...], causal)])

    a = acc_a[...]
    b = acc_b[...]
    out = jnp.where(lo, a / pltpu.roll(a, HEAD_DIM, axis=1), b / pltpu.roll(b, HEAD_DIM, axis=1))
    o_ref[...] = out.astype(o_ref.dtype)


def _fox_cached_attention(q, k_new, v_new, c_new, k_past, v_past, c_past, *, tkp):
    b, t, w = q.shape
    n_pairs = w // LANES
    plen = k_past.shape[1]
    n_past = plen // tkp
    assert n_past * tkp == plen
    pair_rows = lambda c: c.reshape(b, n_pairs, 2, c.shape[-1])
    seqspec = lambda n: pl.BlockSpec((None, n, LANES), lambda bi, p: (bi, 0, p))
    cspec = lambda n: pl.BlockSpec((None, None, 2, n), lambda bi, p: (bi, p, 0, 0))
    return pl.pallas_call(
        functools.partial(_fox_cached_kernel, tkp=tkp, n_past=n_past),
        out_shape=jax.ShapeDtypeStruct((b, t, w), BF16),
        grid=(b, n_pairs),
        in_specs=[seqspec(t), seqspec(t), seqspec(t), cspec(t), cspec(plen),
                  seqspec(plen), seqspec(plen)],
        out_specs=seqspec(t),
        scratch_shapes=[pltpu.VMEM((t, LANES), F32)] * 2 + [pltpu.VMEM((t, 1), F32)] * 2,
        compiler_params=_cparams(2),
        name="fox_cached_attn",
    )(q, k_new, v_new, pair_rows(c_new), pair_rows(c_past), k_past, v_past)


def _diff_lambda(dl, lambda_init):
    return (jnp.exp(jnp.sum(dl[0:1] * dl[1:2], axis=1, keepdims=True))
            - jnp.exp(jnp.sum(dl[2:3] * dl[3:4], axis=1, keepdims=True)) + lambda_init)


def _diff_cached_kernel(q_ref, kn_ref, vn_ref, bprev_ref, bdiag_ref, lam_ref, g_ref,
                        kp_ref, vp_ref, o_ref, acc0, acc1, m0, m1, l0, l1,
                        *, tkp, n_past, lambda_init):
    lo = lax.broadcasted_iota(I32, (1, LANES), 1) < HEAD_DIM
    q = q_ref[...]
    zero = jnp.zeros_like(q)
    q_maps = (jnp.where(lo, q, zero), jnp.where(lo, zero, q))
    accs, ms, ls = (acc0, acc1), (m0, m1), (l0, l1)
    for acc, m, l in zip(accs, ms, ls):
        acc[...] = jnp.zeros_like(acc)
        l[...] = jnp.zeros_like(l)
        m[...] = jnp.full_like(m, NEG_INF)

    def run(chunks):
        @_memo
        def operands(n):
            return chunks[n][0].astype(BF16), chunks[n][1].astype(BF16)

        def scores(unit):
            n, mp = unit
            return _nt_dot(q_maps[mp], operands(n)[0])

        def consume(unit, s):
            n, mp = unit
            if chunks[n][2] is not None:
                s = jnp.maximum(s, NEG_INF) + chunks[n][2]
            m_old = ms[mp][...]
            m_new = jnp.maximum(m_old, jnp.max(s, axis=1, keepdims=True))
            p = jnp.exp2(s - m_new)
            alpha = jnp.exp2(m_old - m_new)
            ls[mp][...] = alpha * ls[mp][...] + jnp.sum(p, axis=1, keepdims=True)
            accs[mp][...] = alpha * accs[mp][...] + jnp.dot(
                p.astype(BF16), operands(n)[1], preferred_element_type=F32)
            ms[mp][...] = m_new

        _pipelined([(n, mp) for n in range(len(chunks)) for mp in range(2)], scores, consume)

    def cached(c, bias=None):
        off = pl.multiple_of(c * tkp, tkp)
        return kp_ref[pl.ds(off, tkp), :], vp_ref[pl.ds(off, tkp), :], bias

    def past_body(j, carry):
        run([cached(j * CACHED_CHUNK_GROUP + g) for g in range(CACHED_CHUNK_GROUP)])
        return carry
    n_far = n_past - 1
    lax.fori_loop(0, n_far // CACHED_CHUNK_GROUP, past_body, 0)
    run([cached(c) for c in range(n_far - n_far % CACHED_CHUNK_GROUP, n_far)]
        + [cached(n_far, bprev_ref[...]), (kn_ref[...], vn_ref[...], bdiag_ref[...])])

    lam = _diff_lambda(lam_ref[...], lambda_init)
    o = acc0[...] / l0[...] - lam * (acc1[...] / l1[...])
    o = o * _rms_scale(o, SUBLN_EPS) * g_ref[...] * (1.0 - lambda_init)
    o_ref[...] = o.astype(o_ref.dtype)


def _diff_cached_attention(q, k_new, v_new, bias_prev, bias_diag, diff_lambda, subln_g,
                           k_past, v_past, *, tkp, lambda_init):
    b, t, w = q.shape
    plen = k_past.shape[1]
    n_past = plen // tkp
    assert n_past * tkp == plen and tkp >= MAX_DISTANCE and t <= CHUNK and plen % CHUNK == 0
    seqspec = lambda n: pl.BlockSpec((None, n, LANES), lambda bi, h: (bi, 0, h))
    head_tile = lambda arr: pl.BlockSpec((None,) + arr.shape[1:], lambda bi, h: (h, 0, 0))
    const2 = lambda arr: pl.BlockSpec(arr.shape, lambda bi, h: (0, 0))
    return pl.pallas_call(
        functools.partial(_diff_cached_kernel, tkp=tkp, n_past=n_past, lambda_init=lambda_init),
        out_shape=jax.ShapeDtypeStruct((b, t, w), BF16),
        grid=(b, w // LANES),
        in_specs=[seqspec(t), seqspec(t), seqspec(t), head_tile(bias_prev), head_tile(bias_diag),
                  const2(diff_lambda), const2(subln_g), seqspec(plen), seqspec(plen)],
        out_specs=seqspec(t),
        scratch_shapes=[pltpu.VMEM((t, LANES), F32)] * 2 + [pltpu.VMEM((t, 1), F32)] * 4,
        compiler_params=_cparams(2),
        name="diff_cached_attn",
    )(q, k_new, v_new, bias_prev, bias_diag, diff_lambda, subln_g, k_past, v_past)


def _fox_prompt_kernel(q_ref, k_ref, kx_ref, vt_ref, o_ref, acc_a, acc_b, m_a, m_b, *, tq, rs):
    i = pl.program_id(2)
    n_qs = tq // rs
    lane = lax.broadcasted_iota(I32, (1, LANES), 1)
    lo = lane < HEAD_DIM
    top = lax.broadcasted_iota(I32, (LANES, rs), 0) < HEAD_DIM

    q = q_ref[...].astype(F32)
    ones_a = jnp.where(lane < HEAD_DIM + N_DECAY_TERMS, 1.0, 0.0)
    ones_b = jnp.where(lane < N_DECAY_TERMS, 1.0, 0.0)
    q_heads = (jnp.where(lo, q, ones_a).astype(BF16), jnp.where(lo, ones_b, q).astype(BF16))
    accs = (acc_a, acc_b)
    ms = (m_a, m_b)
    for acc, m in zip(accs, ms):
        acc[...] = jnp.zeros_like(acc)
        m[...] = jnp.full_like(m, NEG_INF)

    def chunk_group(first, units):
        @_memo
        def operands(kc):
            off = pl.multiple_of((first + kc) * rs, rs)
            k = k_ref[pl.ds(off, rs), :]
            kx = kx_ref[pl.ds(off, rs), :]
            vt = vt_ref[:, pl.ds(off, rs)]
            ones_v = jnp.ones_like(vt)
            return ((jnp.where(lo, k, kx), jnp.where(lo, kx, k)),
                    (jnp.where(top, vt, ones_v), jnp.where(top, ones_v, vt)))

        def scores(unit):
            kc, hd, qs, _ = unit
            return _nt_dot(operands(kc)[0][hd], q_heads[hd][qs * rs:(qs + 1) * rs])

        def consume(unit, s):
            kc, hd, qs, mask = unit
            cols = pl.ds(qs * rs, rs)
            if mask is not None:
                s = jnp.where(mask, s, NEG_INF)
            m_old = ms[hd][:, cols]
            m_new = jnp.maximum(m_old, jnp.max(s, axis=0, keepdims=True))
            p = jnp.exp2(s - m_new).astype(BF16)
            accs[hd][:, cols] = jnp.exp2(m_old - m_new) * accs[hd][:, cols] + jnp.dot(
                operands(kc)[1][hd], p, preferred_element_type=F32)
            ms[hd][:, cols] = m_new

        _pipelined(units, scores, consume)

    def below(j, carry):
        chunk_group(n_qs * j, [(kc, hd, qs, None) for kc in range(n_qs) for qs in range(n_qs)
                               for hd in range(2)])
        return carry
    lax.fori_loop(0, i, below, 0)
    causal = (lax.broadcasted_iota(I32, (rs, rs), 0) <= lax.broadcasted_iota(I32, (rs, rs), 1))
    chunk_group(n_qs * i, [(kc, hd, qs, causal if qs == kc else None) for kc in range(n_qs)
                           for qs in range(kc, n_qs) for hd in range(2)])

    a = acc_a[...]
    b = acc_b[...]
    out_t = jnp.concatenate([a[:HEAD_DIM] / a[HEAD_DIM:HEAD_DIM + 1], b[HEAD_DIM:] / b[0:1]],
                            axis=0)
    o_ref[...] = out_t.T.astype(o_ref.dtype)


def _fox_prompt_attention(q, k, kx, vt, *, tq, rs):
    b, t, w = q.shape
    qspec = pl.BlockSpec((None, tq, LANES), lambda bi, p, i: (bi, i, p))
    kspec = pl.BlockSpec((None, t, LANES), lambda bi, p, i: (bi, 0, p))
    return pl.pallas_call(
        functools.partial(_fox_prompt_kernel, tq=tq, rs=rs),
        out_shape=jax.ShapeDtypeStruct((b, t, w), BF16),
        grid=(b, w // LANES, t // tq),
        in_specs=[qspec, kspec, kspec, pl.BlockSpec((LANES, t), lambda bi, p, i: (p, bi))],
        out_specs=qspec,
        scratch_shapes=[pltpu.VMEM((LANES, tq), F32), pltpu.VMEM((LANES, tq), F32),
                        pltpu.VMEM((1, tq), F32), pltpu.VMEM((1, tq), F32)],
        compiler_params=_cparams(3),
        name="fox_prompt_attn",
    )(q, k, kx, vt)


ONES_ROWS = 16


def _diff_prompt_kernel(q_ref, k_ref, vt_ref, bprev_ref, bdiag_ref, lam_ref, g_ref, o_ref,
                        acc0, acc1, m0, m1, *, tq, rs, lambda_init):
    i = pl.program_id(2)
    n_qs = tq // rs
    lo = lax.broadcasted_iota(I32, (1, LANES), 1) < HEAD_DIM

    q = q_ref[...]
    zero = jnp.zeros_like(q)
    q_maps = (jnp.where(lo, q, zero), jnp.where(lo, zero, q))
    accs, ms = (acc0, acc1), (m0, m1)
    for acc, m in zip(accs, ms):
        acc[...] = jnp.zeros_like(acc)
        m[...] = jnp.full_like(m, NEG_INF)

    def chunk_group(first, blocks_ahead):
        @_memo
        def operands(kc):
            off = pl.multiple_of((first + kc) * rs, rs)
            return (k_ref[pl.ds(off, rs), :],
                    jnp.concatenate([vt_ref[:, pl.ds(off, rs)], jnp.ones((ONES_ROWS, rs), BF16)],
                                    axis=0))

        def scores(unit):
            kc, mp, qs = unit
            return _nt_dot(operands(kc)[0], q_maps[mp][qs * rs:(qs + 1) * rs])

        def consume(unit, s):
            kc, mp, qs = unit
            cols = pl.ds(qs * rs, rs)
            distance = blocks_ahead + qs - kc
            if distance == 1:
                s = jnp.maximum(s, NEG_INF) + bprev_ref[...]
            elif distance == 0:
                s = jnp.maximum(s, NEG_INF) + bdiag_ref[...]
            m_old = ms[mp][:, cols]
            m_new = jnp.maximum(m_old, jnp.max(s, axis=0, keepdims=True))
            p = jnp.exp2(s - m_new).astype(BF16)
            accs[mp][:, cols] = jnp.exp2(m_old - m_new) * accs[mp][:, cols] + jnp.dot(
                operands(kc)[1], p, preferred_element_type=F32)
            ms[mp][:, cols] = m_new

        _pipelined([(kc, mp, qs) for kc in range(n_qs) for qs in range(n_qs) for mp in range(2)
                    if blocks_ahead + qs - kc >= 0], scores, consume)

    def far_below(j, carry):
        chunk_group(n_qs * j, 2 * n_qs)
        return carry
    lax.fori_loop(0, jnp.maximum(i - 1, 0), far_below, 0)

    @pl.when(i >= 1)
    def _():
        chunk_group(n_qs * (i - 1), n_qs)

    chunk_group(n_qs * i, 0)

    lam = _diff_lambda(lam_ref[...], lambda_init)
    a0 = acc0[...]
    a1 = acc1[...]
    o = a0[:LANES] / a0[LANES:LANES + 1] - lam * (a1[:LANES] / a1[LANES:LANES + 1])
    scale = lax.rsqrt(jnp.mean(o * o, axis=0, keepdims=True) + SUBLN_EPS)
    o = o * scale * g_ref[...] * (1.0 - lambda_init)
    o_ref[...] = o.T.astype(o_ref.dtype)


def _diff_prompt_attention(q, k, vt, bias_prev, bias_diag, diff_lambda, subln_g, *,
                           tq, rs, lambda_init):
    b, t, w = q.shape
    assert rs >= MAX_DISTANCE
    qspec = pl.BlockSpec((None, tq, LANES), lambda bi, h, i: (bi, i, h))
    head_tile = lambda arr: pl.BlockSpec((None,) + arr.shape[1:], lambda bi, h, i: (h, 0, 0))
    const2 = lambda arr: pl.BlockSpec(arr.shape, lambda bi, h, i: (0, 0))
    return pl.pallas_call(
        functools.partial(_diff_prompt_kernel, tq=tq, rs=rs, lambda_init=lambda_init),
        out_shape=jax.ShapeDtypeStruct((b, t, w), BF16),
        grid=(b, w // LANES, t // tq),
        in_specs=[qspec, pl.BlockSpec((None, t, LANES), lambda bi, h, i: (bi, 0, h)),
                  pl.BlockSpec((LANES, t), lambda bi, h, i: (h, bi)),
                  head_tile(bias_prev), head_tile(bias_diag), const2(diff_lambda),
                  const2(subln_g)],
        out_specs=qspec,
        scratch_shapes=[pltpu.VMEM((LANES + ONES_ROWS, tq), F32)] * 2
        + [pltpu.VMEM((1, tq), F32)] * 2,
        compiler_params=_cparams(3),
        name="diff_prompt_attn",
    )(q, k, vt, bias_prev, bias_diag, diff_lambda, subln_g)


def _t5_bucket(rel):
    nb = NUM_BUCKETS // 2
    max_exact = nb // 2
    ret = jnp.where(rel > 0, nb, 0)
    n = jnp.abs(rel)
    nf = jnp.maximum(n, max_exact).astype(F32)
    large = max_exact + (jnp.log(nf / max_exact) / math.log(MAX_DISTANCE / max_exact)
                         * (nb - max_exact)).astype(I32)
    large = jnp.minimum(large, nb - 1)
    return ret + jnp.where(n < max_exact, n, large)


def _bias_tiles(rel_table, tq, tk_prev):
    buckets = jnp.arange(NUM_BUCKETS, dtype=I32)

    def lookup(rel):
        hit = _t5_bucket(rel)[None, :] == buckets[:, None]
        return jnp.sum(jnp.where(hit[:, None, :], rel_table[:, :, None], 0.0), axis=0)

    far = lookup(jnp.full((1,), -4 * MAX_DISTANCE, I32))

    def tile(rel0, tk):
        period = tq + tk
        j = jnp.arange(period, dtype=I32)
        strip = lookup(rel0 + jnp.where(j < tk, j, j - period)) - far
        flat = jnp.tile(strip, (1, tq))[:, :tq * (period - 1)]
        return (flat.reshape(-1, tq, period - 1)[:, :, :tk] * LOG2E).astype(F32)

    return tile(-tk_prev, tk_prev), tile(0, tq)


def _router_kernel(x_ref, of_ref, od_ref, wof_ref, wod_ref, g_ref, wr_ref, br_ref,
                   x1_ref, hn_ref, te_ref, tw_ref, rank_ref, cnt_ref, *, tm):
    step = pl.program_id(0)

    @pl.when(step == 0)
    def _():
        cnt_ref[...] = jnp.zeros_like(cnt_ref)

    x1 = (x_ref[...] + jnp.dot(of_ref[...], wof_ref[...], preferred_element_type=F32)
          + jnp.dot(od_ref[...], wod_ref[...], preferred_element_type=F32))
    x1_ref[...] = x1
    hn = x1 * _rms_scale(x1, NORM_EPS) * g_ref[...]
    half = hn.shape[1] // 2
    hn_ref[...] = _pack_bf16_pairs(hn[:, :half], hn[:, half:])

    hn_hi = hn.astype(BF16)
    hn_lo = (hn - hn_hi.astype(F32)).astype(BF16)
    both = jnp.dot(hn_hi, wr_ref[...], preferred_element_type=F32)
    logits = (both[:, :LANES] + both[:, LANES:] + br_ref[...]
              + jnp.dot(hn_lo, wr_ref[:, :LANES], preferred_element_type=F32))
    lane = lax.broadcasted_iota(I32, logits.shape, 1)
    lane_f = lane.astype(F32)
    sel = jnp.zeros(logits.shape, F32)
    vals, idxs = [], []
    for _ in range(TOP_K):
        mx = jnp.max(logits, axis=1, keepdims=True)
        idx = jnp.min(jnp.where(logits == mx, lane_f, float(LANES)), axis=1, keepdims=True)
        hit = lane_f == idx
        sel = jnp.where(hit, 1.0, sel)
        logits = jnp.where(hit, -jnp.inf, logits)
        vals.append(mx)
        idxs.append(idx)
    exps = [jnp.exp(v - vals[0]) for v in vals]
    denom = exps[0] + exps[1] + exps[2] + exps[3]

    earlier = (lax.broadcasted_iota(I32, (tm, tm), 1)
               < lax.broadcasted_iota(I32, (tm, tm), 0)).astype(BF16)
    before = cnt_ref[...] + jnp.dot(earlier, sel.astype(BF16), preferred_element_type=F32)
    for k in range(TOP_K):
        te_ref[:, k:k + 1] = idxs[k].astype(I32)
        tw_ref[:, k:k + 1] = exps[k] / denom
        rank_ref[:, k:k + 1] = jnp.sum(
            jnp.where(lane_f == idxs[k], before, 0.0), axis=1, keepdims=True).astype(I32)
    cnt_ref[...] += jnp.sum(sel, axis=0, keepdims=True)


def _router(x2d, o_fox, o_diff, wo_f, wo_d, g_ffn, w_r, b_r, *, tm):
    n, d = x2d.shape
    row = lambda i: (i, 0)
    const = lambda i: (0, 0)
    full = lambda arr: pl.BlockSpec(arr.shape, const)
    return pl.pallas_call(
        functools.partial(_router_kernel, tm=tm),
        out_shape=[jax.ShapeDtypeStruct((n, d), F32), jax.ShapeDtypeStruct((n, d // 2), U32),
                   jax.ShapeDtypeStruct((n, TOP_K), I32), jax.ShapeDtypeStruct((n, TOP_K), F32),
                   jax.ShapeDtypeStruct((n, TOP_K), I32), jax.ShapeDtypeStruct((1, LANES), F32)],
        grid=(n // tm,),
        in_specs=[pl.BlockSpec((tm, d), row), pl.BlockSpec((tm, o_fox.shape[1]), row),
                  pl.BlockSpec((tm, o_diff.shape[1]), row), full(wo_f), full(wo_d),
                  full(g_ffn), full(w_r), full(b_r)],
        out_specs=[pl.BlockSpec((tm, d), row), pl.BlockSpec((tm, d // 2), row),
                   pl.BlockSpec((tm, TOP_K), row), pl.BlockSpec((tm, TOP_K), row),
                   pl.BlockSpec((tm, TOP_K), row), pl.BlockSpec((1, LANES), const)],
        compiler_params=_cparams(1),
        name="router",
    )(x2d, o_fox, o_diff, wo_f, wo_d, g_ffn, w_r, b_r)


def _row_copy(src, src_row, dst, dst_row, sem):
    return pltpu.make_async_copy(src.at[pl.ds(src_row, 1), :], dst.at[pl.ds(dst_row, 1), :], sem)


def _dispatch_kernel(tail_start_ref, tail_valid_ref, pos_ref, hn_ref, xs_hbm, zeros_ref, sem,
                     *, tm, tme, n_experts):
    def tail_copy(e):
        return pltpu.make_async_copy(
            zeros_ref, xs_hbm.at[pl.ds(pl.multiple_of(tail_start_ref[e], tme), tme), :], sem.at[0])

    @pl.when(pl.program_id(0) == 0)
    def _():
        zeros_ref[...] = jnp.zeros_like(zeros_ref)
        for e in range(n_experts):
            @pl.when(tail_valid_ref[e] > 0)
            def _():
                tail_copy(e).start()
        for e in range(n_experts):
            @pl.when(tail_valid_ref[e] > 0)
            def _():
                tail_copy(e).wait()

    def start_row(r, carry):
        for k in range(TOP_K):
            _row_copy(hn_ref, r, xs_hbm, pos_ref[r, k], sem.at[1]).start(priority=k % 2)
        return carry

    def wait_row(r, carry):
        for k in range(TOP_K):
            _row_copy(hn_ref, r, xs_hbm, pos_ref[r, k], sem.at[1]).wait()
        return carry

    lax.fori_loop(0, tm, start_row, 0, unroll=True)
    lax.fori_loop(0, tm, wait_row, 0, unroll=8)


def _dispatch(hn_packed, pos, tail_start, tail_valid, *, m_pad, tm, tme):
    n, half = hn_packed.shape
    n_experts = tail_start.shape[0]
    grid_spec = pltpu.PrefetchScalarGridSpec(
        num_scalar_prefetch=2,
        grid=(n // tm,),
        in_specs=[pl.BlockSpec((tm, TOP_K), lambda i, *_: (i, 0), memory_space=pltpu.SMEM),
                  pl.BlockSpec((tm, half), lambda i, *_: (i, 0))],
        out_specs=pl.BlockSpec(memory_space=pl.ANY),
        scratch_shapes=[pltpu.VMEM((tme, half), U32), pltpu.SemaphoreType.DMA((2,))],
    )
    return pl.pallas_call(
        functools.partial(_dispatch_kernel, tm=tm, tme=tme, n_experts=n_experts),
        out_shape=jax.ShapeDtypeStruct((m_pad, half), U32),
        grid_spec=grid_spec,
        compiler_params=pltpu.CompilerParams(
            dimension_semantics=("arbitrary",), vmem_limit_bytes=VMEM_LIMIT_BYTES,
            has_side_effects=True),
        name="dispatch",
    )(tail_start, tail_valid, pos, hn_packed)


def _experts_kernel(te_ref, nact_ref, xs_ref, wg_ref, bg_ref, wu_ref, bu_ref, wd_ref, bd_ref,
                    ys_ref):
    @pl.when(pl.program_id(0) < nact_ref[0])
    def _():
        half = xs_ref.shape[1]
        x_lo, x_hi = _unpack_bf16_pairs(xs_ref[...])
        x_lo = x_lo.astype(BF16)
        x_hi = x_hi.astype(BF16)

        def proj(w_ref, b_ref):
            return (jnp.dot(x_lo, w_ref[:half, :], preferred_element_type=F32)
                    + jnp.dot(x_hi, w_ref[half:, :], preferred_element_type=F32) + b_ref[...])

        g = jnp.minimum(proj(wg_ref, bg_ref), SWIGLU_LIMIT)
        u = jnp.clip(proj(wu_ref, bu_ref), -SWIGLU_LIMIT, SWIGLU_LIMIT)
        a = ((u + 1.0) * (g * jax.nn.sigmoid(SWIGLU_ALPHA * g))).astype(BF16)
        y = jnp.dot(a, wd_ref[...], preferred_element_type=F32) + bd_ref[...]
        ys_ref[...] = _pack_bf16_pairs(y[:, :half], y[:, half:])


def _experts(xs, tile_expert, n_active, w_gate, b_gate, w_up, b_up, w_down, b_down, *, tme):
    m_pad, half = xs.shape
    n_tiles = m_pad // tme
    d, d_ff = w_gate.shape[1:]
    row = lambda t, te, na: (jnp.minimum(t, na[0] - 1), 0)
    wspec = lambda shape: pl.BlockSpec((None,) + shape, lambda t, te, na: (te[t], 0, 0))
    grid_spec = pltpu.PrefetchScalarGridSpec(
        num_scalar_prefetch=2,
        grid=(n_tiles,),
        in_specs=[pl.BlockSpec((tme, half), row),
                  wspec((d, d_ff)), wspec((1, d_ff)), wspec((d, d_ff)), wspec((1, d_ff)),
                  wspec((d_ff, d)), wspec((1, d))],
        out_specs=pl.BlockSpec((tme, half), row),
    )
    return pl.pallas_call(
        _experts_kernel,
        out_shape=jax.ShapeDtypeStruct((m_pad, half), U32),
        grid_spec=grid_spec,
        compiler_params=_cparams(1),
        name="experts",
    )(tile_expert, n_active, xs, w_gate, b_gate, w_up, b_up, w_down, b_down)


def _combine_kernel(pos_ref, pos_next_ref, x1_ref, tw_ref, p_ref, wg_ref, wp_ref, gple_ref,
                    gfin_ref, ys_hbm, y_ref, gbuf, sem, *, tm):
    step = pl.program_id(0)
    slot = step % 2

    def request(tile_pos_ref, tile_slot, unroll):
        def row(r, carry):
            for k in range(TOP_K):
                _row_copy(ys_hbm, tile_pos_ref[r, k], gbuf.at[tile_slot, k], r,
                          sem.at[tile_slot]).start(priority=k % 2)
            return carry
        lax.fori_loop(0, tm, row, 0, unroll=unroll)

    def await_rows(tile_slot):
        one_row = _row_copy(ys_hbm, 0, gbuf.at[tile_slot, 0], 0, sem.at[tile_slot])
        lax.fori_loop(0, tm * TOP_K, lambda r, carry: (one_row.wait(), carry)[1], 0, unroll=8)

    @pl.when(step == 0)
    def _():
        request(pos_ref, 0, 8)

    await_rows(slot)
    for other in range(2):
        @pl.when(slot == 1 - other)
        def _():
            request(pos_next_ref, other, True)

    x1 = x1_ref[...]
    half = x1.shape[1] // 2
    tw = tw_ref[...]
    moe_lo = jnp.zeros((tm, half), F32)
    moe_hi = jnp.zeros((tm, half), F32)
    for k in range(TOP_K):
        lo, hi = _unpack_bf16_pairs(gbuf[slot, k])
        moe_lo = moe_lo + tw[:, k:k + 1] * lo
        moe_hi = moe_hi + tw[:, k:k + 1] * hi
    x2 = x1 + jnp.concatenate([moe_lo, moe_hi], axis=1)
    hp = (x2 * _rms_scale(x2, NORM_EPS) * gple_ref[...]).astype(BF16)
    gate = jax.nn.sigmoid(jnp.dot(hp, wg_ref[...], preferred_element_type=F32))
    proj = jnp.dot(p_ref[...].astype(BF16), wp_ref[...], preferred_element_type=F32)
    x3 = x2 + proj * gate
    y_ref[...] = x3 * _rms_scale(x3, NORM_EPS) * gfin_ref[...]

    @pl.when(step == pl.num_programs(0) - 1)
    def _():
        await_rows(1 - slot)


def _combine(x1, top_w, pos, ys, p2d, w_ple_gate, w_ple_proj, g_ple, g_final, *, tm):
    n, d = x1.shape
    half = ys.shape[1]
    row = lambda i: (i, 0)
    const = lambda i: (0, 0)
    full = lambda arr: pl.BlockSpec(arr.shape, const)
    return pl.pallas_call(
        functools.partial(_combine_kernel, tm=tm),
        out_shape=jax.ShapeDtypeStruct((n, d), F32),
        grid=(n // tm,),
        in_specs=[pl.BlockSpec((tm, TOP_K), row, memory_space=pltpu.SMEM),
                  pl.BlockSpec((tm, TOP_K), lambda i: (jnp.minimum(i + 1, n // tm - 1), 0),
                               memory_space=pltpu.SMEM),
                  pl.BlockSpec((tm, d), row), pl.BlockSpec((tm, TOP_K), row),
                  pl.BlockSpec((tm, p2d.shape[1]), row), full(w_ple_gate), full(w_ple_proj),
                  full(g_ple), full(g_final), pl.BlockSpec(memory_space=pl.ANY)],
        out_specs=pl.BlockSpec((tm, d), row),
        scratch_shapes=[pltpu.VMEM((2, TOP_K, tm, half), U32), pltpu.SemaphoreType.DMA((2,))],
        compiler_params=_cparams(1),
        name="combine",
    )(pos, pos, x1, top_w, p2d, w_ple_gate, w_ple_proj, g_ple, g_final, ys)


def _pick_tile(n, want):
    t = min(n, want)
    assert n % t == 0, (n, t)
    return t


def _layer_group(x, p, past, wts, lambda_init, g_final):
    (g_mix, w_main, w_vt, w_f, b_f, diff_lambda, subln_g, rel_table, wo_f, wo_d, g_ffn, w_r, b_r,
     w_gate, b_gate, w_up, b_up, w_down, b_down, g_ple, w_ple_gate, w_ple_proj) = wts
    b, t, d = x.shape
    n = b * t
    n_fox = (w_main.shape[1] // 6) // HEAD_DIM
    n_experts = w_gate.shape[0]
    x2d = x.reshape(n, d)

    tm = _pick_tile(n, 512)
    (qf, qd, kf, vf, kd, vd, kf16, kd16, vf16, vd16, logf_lanes) = _inproj(
        x2d, g_mix, w_main, w_vt, w_f, b_f, tm=tm, values_transposed=past is None)
    logf = logf_lanes[:, :n_fox]
    width = qf.shape[1]
    as_seq = lambda a: a.reshape(b, t, a.shape[-1])

    tq = _pick_tile(t, 1024)
    rs = _pick_tile(tq, 256)
    if past is None:
        kx = _key_decay(as_seq(logf_lanes), n_fox, width)
        o_fox = _fox_prompt_attention(as_seq(qf), as_seq(kf16), kx, vf16, tq=tq, rs=rs)
        bias_prev, bias_diag = _bias_tiles(rel_table, rs, rs)
        block = jnp.arange(rs, dtype=I32) // CHUNK
        bias_diag = jnp.where(block[:, None] <= block[None, :], jnp.swapaxes(bias_diag, 1, 2),
                              NEG_INF)
        o_diff = _diff_prompt_attention(
            as_seq(qd), as_seq(kd16), vd16, jnp.swapaxes(bias_prev, 1, 2), bias_diag,
            diff_lambda, subln_g.reshape(-1, 1), tq=tq, rs=rs, lambda_init=lambda_init)
    else:
        pk_f, pv_f, plogf, pk_d, pv_d = past
        plen = pk_f.shape[1]
        tkp = 512
        total = plen + t
        padded = -(-total // LANES) * LANES
        seq = jnp.concatenate([jnp.transpose(plogf.astype(F32), (0, 2, 1)),
                               jnp.transpose(as_seq(logf), (0, 2, 1))], axis=2)
        c_all = _cumsum_lanes(jnp.pad(seq, ((0, 0), (0, 0), (0, padded - total))))
        flat = lambda a: a.reshape(b, plen, width).astype(BF16)
        o_fox = _fox_cached_attention(as_seq(qf), as_seq(kf16), as_seq(vf16),
                                      c_all[:, :, plen:total], flat(pk_f), flat(pv_f),
                                      c_all[:, :, :plen], tkp=tkp)
        bias_prev, bias_diag = _bias_tiles(rel_table, t, tkp)
        o_diff = _diff_cached_attention(as_seq(qd), as_seq(kd16), as_seq(vd16), bias_prev,
                                        bias_diag, diff_lambda, subln_g, flat(pk_d), flat(pv_d),
                                        tkp=tkp, lambda_init=lambda_init)

    tmr = _pick_tile(n, 256)
    x1, hn_packed, top_e, top_w, rank, counts = _router(
        x2d, o_fox.reshape(n, width), o_diff.reshape(n, width), wo_f, wo_d, g_ffn, w_r, b_r, tm=tmr)

    tme = 512 if n * TOP_K >= 64 * 512 else 256
    n_tiles = -(-(n * TOP_K) // tme) + n_experts
    m_pad = n_tiles * tme
    cnt = counts[0, :n_experts].astype(I32)
    group = -(-cnt // tme) * tme
    ends = jnp.cumsum(group)
    starts = ends - group
    experts = jnp.arange(n_experts, dtype=I32)
    pos = rank + jnp.sum(jnp.where(top_e[:, :, None] == experts, starts, 0), axis=2)
    n_active = (ends[-1] // tme).astype(I32)
    tile_ids = jnp.arange(n_tiles, dtype=I32)
    tile_expert = jnp.sum(jnp.minimum(tile_ids, n_active - 1)[:, None] >= (ends // tme)[None, :],
                          axis=1).astype(I32)
    tile_expert = jnp.minimum(tile_expert, n_experts - 1)
    tail_start = jnp.maximum(ends - tme, 0).astype(I32)
    tail_valid = (group > 0).astype(I32)

    tmd = _pick_tile(n, 256)
    xs = _dispatch(hn_packed, pos, tail_start, tail_valid, m_pad=m_pad, tm=tmd, tme=tme)
    ys = _experts(xs, tile_expert, n_active.reshape(1), w_gate, b_gate, w_up, b_up, w_down, b_down,
                  tme=tme)
    y = _combine(x1, top_w, pos, ys, p.reshape(n, p.shape[-1]), w_ple_gate, w_ple_proj, g_ple,
                 g_final, tm=tmd)
    return y.reshape(b, t, d), (kf, vf, logf, kd, vd)


def kernel(x_prompt, x_sample, p_prompt, p_sample, cache_fox_k, cache_fox_v, cache_fox_logf, cache_diff_k, cache_diff_v, g_mix, w_in, b_forget, diff_lambda, diff_subln_g, rel_bias_table, w_o, g_ffn, w_router, b_router, w_gate, b_gate, w_up, b_up, w_down, b_down, g_ple, w_ple_gate, w_ple_proj, g_final):
    depth = g_mix.shape[0]
    assert depth == 1, "the final norm is fused into the last layer; one layer supported"
    d = x_prompt.shape[-1]
    n_fox = cache_fox_k.shape[3]
    fox_w = n_fox * cache_fox_k.shape[4]
    n_diff = cache_diff_k.shape[3]
    diff_w = n_diff * 2 * cache_diff_k.shape[5]
    assert fox_w == diff_w and fox_w % LANES == 0
    n_experts = w_router.shape[-1]
    assert TOP_K <= n_experts <= LANES
    row2 = lambda a: a.reshape(1, -1).astype(F32)

    i = 0
    lambda_init = 0.8 - 0.6 * math.exp(-0.3 * i)
    off_fk, off_fv, off_ff = fox_w, 2 * fox_w, 3 * fox_w
    off_dq = off_ff + n_fox
    off_dk, off_dv = off_dq + diff_w, off_dq + 2 * diff_w
    w = w_in[i]
    cols = lambda o, wd: w[:, o:o + wd]
    w_main = jnp.concatenate([cols(0, fox_w), cols(off_dq, diff_w), cols(off_fk, fox_w),
                              cols(off_fv, fox_w), cols(off_dk, diff_w), cols(off_dv, diff_w)],
                             axis=1).astype(BF16)
    w_vt = jnp.concatenate([cols(off_fv, fox_w), cols(off_dv, diff_w)], axis=1).T.astype(BF16)
    w_f = jnp.pad(cols(off_ff, n_fox), ((0, 0), (0, LANES - n_fox))).astype(BF16)
    b_f = jnp.pad(row2(b_forget[i]), ((0, 0), (0, LANES - n_fox)))
    w_r = jnp.pad(w_router[i].astype(F32), ((0, 0), (0, LANES - n_experts)))
    w_r_hi = w_r.astype(BF16)
    w_r = jnp.concatenate([w_r_hi, (w_r - w_r_hi.astype(F32)).astype(BF16)], axis=1)
    b_r = jnp.pad(row2(b_router[i]), ((0, 0), (0, LANES - n_experts)), constant_values=NEG_INF)
    wo = w_o[i].astype(BF16)
    bias3 = lambda a: a.reshape(n_experts, 1, -1).astype(F32)
    wts = (row2(g_mix[i]), w_main, w_vt, w_f, b_f, diff_lambda[i].astype(F32), row2(diff_subln_g[i]),
           rel_bias_table.astype(F32), wo[:fox_w], wo[fox_w:], row2(g_ffn[i]), w_r, b_r,
           w_gate[i].astype(BF16), bias3(b_gate[i]), w_up[i].astype(BF16), bias3(b_up[i]),
           w_down[i].astype(BF16), bias3(b_down[i]), row2(g_ple[i]),
           w_ple_gate[i].astype(BF16), w_ple_proj[i].astype(BF16))
    gf = row2(g_final)

    yp, new_p = _layer_group(x_prompt, p_prompt[i], None, wts, lambda_init, gf)
    past_s = (cache_fox_k[i], cache_fox_v[i], cache_fox_logf[i], cache_diff_k[i], cache_diff_v[i])
    ys, new_s = _layer_group(x_sample, p_sample[i], past_s, wts, lambda_init, gf)

    def rows(new, b, t):
        kf, vf, logf, kd, vd = new
        return (kf.reshape(1, b, t, n_fox, -1), vf.reshape(1, b, t, n_fox, -1),
                logf.reshape(1, b, t, n_fox), kd.reshape(1, b, t, n_diff, 2, -1),
                vd.reshape(1, b, t, n_diff, -1))

    bp, tp = x_prompt.shape[:2]
    bs, ts = x_sample.shape[:2]
    return (yp, ys) + rows(new_p, bp, tp) + rows(new_s, bs, ts)
```

```python
import functools
import math

import jax
import jax.numpy as jnp
import numpy as np
from jax import lax
from jax.experimental import pallas as pl
from jax.experimental.pallas import tpu as pltpu

HEAD_DIM = 64
CHUNK = 64
NUM_BUCKETS = 32
MAX_DISTANCE = 128
TOP_K = 4
SWIGLU_LIMIT = 7.0
SWIGLU_ALPHA = 1.702
NORM_EPS = 1e-6
SUBLN_EPS = 1e-5
NEG_INF = -1e30
LOG2E = math.log2(math.e)

LANES = 128
VMEM_LIMIT_BYTES = 56 * 1024 * 1024

F32 = jnp.float32
BF16 = jnp.bfloat16
I32 = jnp.int32
U32 = jnp.uint32


def _cparams(n_axes):
    return pltpu.CompilerParams(
        dimension_semantics=("arbitrary",) * n_axes, vmem_limit_bytes=VMEM_LIMIT_BYTES)


def _rms_scale(x, eps):
    return lax.rsqrt(jnp.mean(x * x, axis=-1, keepdims=True) + eps)


def _pack_bf16_pairs(lo, hi):
    lo_bits = pltpu.bitcast(lo.astype(BF16).astype(F32), U32)
    hi_bits = pltpu.bitcast(hi.astype(BF16).astype(F32), U32)
    return lo_bits | (hi_bits >> 16)


def _unpack_bf16_pairs(packed):
    lo = pltpu.bitcast(packed & jnp.uint32(0xFFFF0000), F32)
    hi = pltpu.bitcast(packed << 16, F32)
    return lo, hi


def _inproj_kernel(x_ref, g_ref, w_ref, wvt_ref, wf_ref, bf_ref,
                   qf_ref, qd_ref, kf_ref, vf_ref, kd_ref, vd_ref,
                   kf16_ref, kd16_ref, vf16_ref, vd16_ref, logf_ref, *, width, values_transposed):
    x = x_ref[...]
    h = (x * _rms_scale(x, NORM_EPS) * g_ref[...]).astype(BF16)
    q_scale = LOG2E / math.sqrt(HEAD_DIM)

    def proj(c):
        return jnp.dot(h, w_ref[:, c * width:(c + 1) * width], preferred_element_type=F32)

    qf_ref[...] = (proj(0) * q_scale).astype(BF16)
    qd_ref[...] = (proj(1) * q_scale).astype(BF16)
    for c, (o32, o16) in enumerate(((kf_ref, kf16_ref), (vf_ref, vf16_ref),
                                    (kd_ref, kd16_ref), (vd_ref, vd16_ref))):
        u = proj(2 + c)
        if len(o32.shape) == 3:
            o32[...] = pltpu.einshape("m(hd)->mhd", u, h=o32.shape[1])
        else:
            o32[...] = pltpu.einshape("m(hed)->mhed", u, h=o32.shape[1], e=o32.shape[2])
        if not (values_transposed and c in (1, 3)):
            o16[...] = u.astype(BF16)
    if values_transposed:
        vf16_ref[...] = _nt_dot(wvt_ref[:width, :], h).astype(BF16)
        vd16_ref[...] = _nt_dot(wvt_ref[width:, :], h).astype(BF16)
    f = jnp.dot(h, wf_ref[...], preferred_element_type=F32) + bf_ref[...]
    logf_ref[...] = jax.nn.log_sigmoid(f)


def _inproj(x2d, g_mix, w_main, w_vt, w_f, b_f, *, tm, values_transposed):
    n, d = x2d.shape
    width = w_main.shape[1] // 6
    row = lambda i: (i, 0)
    const = lambda i: (0, 0)
    if values_transposed:
        v16_shape, v16_spec = (width, n), pl.BlockSpec((width, tm), lambda i: (0, i))
    else:
        v16_shape, v16_spec = (n, width), pl.BlockSpec((tm, width), row)
    head_shapes = [(width // HEAD_DIM, HEAD_DIM)] * 2 + [
        (width // (2 * HEAD_DIM), 2, HEAD_DIM), (width // (2 * HEAD_DIM), 2 * HEAD_DIM)]
    out_shape = ([jax.ShapeDtypeStruct((n, width), BF16)] * 2
                 + [jax.ShapeDtypeStruct((n,) + s, F32) for s in head_shapes]
                 + [jax.ShapeDtypeStruct((n, width), BF16)] * 2
                 + [jax.ShapeDtypeStruct(v16_shape, BF16)] * 2
                 + [jax.ShapeDtypeStruct((n, LANES), F32)])
    out_specs = ([pl.BlockSpec((tm, width), row)] * 2
                 + [pl.BlockSpec((tm,) + s, lambda i, s=s: (i,) + (0,) * len(s))
                    for s in head_shapes]
                 + [pl.BlockSpec((tm, width), row)] * 2 + [v16_spec] * 2
                 + [pl.BlockSpec((tm, LANES), row)])
    return pl.pallas_call(
        functools.partial(_inproj_kernel, width=width, values_transposed=values_transposed),
        out_shape=out_shape,
        grid=(n // tm,),
        in_specs=[pl.BlockSpec((tm, d), row), pl.BlockSpec((1, d), const),
                  pl.BlockSpec(w_main.shape, const), pl.BlockSpec(w_vt.shape, const),
                  pl.BlockSpec(w_f.shape, const), pl.BlockSpec(b_f.shape, const)],
        out_specs=out_specs,
        compiler_params=_cparams(1),
        name="inproj",
    )(x2d, g_mix, w_main, w_vt, w_f, b_f)


def _cumsum_kernel(x_ref, o_ref):
    x = x_ref[...]
    length = x.shape[1]
    lane = lax.broadcasted_iota(I32, x.shape, 1)
    shift = 1
    while shift < length:
        x = x + jnp.where(lane >= shift, pltpu.roll(x, shift, axis=1), 0.0)
        shift *= 2
    o_ref[...] = x


def _cumsum_lanes(x):
    b, h, length = x.shape
    spec = pl.BlockSpec((None, h, length), lambda i: (i, 0, 0))
    return pl.pallas_call(
        _cumsum_kernel, out_shape=jax.ShapeDtypeStruct(x.shape, F32), grid=(b,),
        in_specs=[spec], out_specs=spec, compiler_params=_cparams(1), name="cumsum",
    )(x)


N_DECAY_TERMS = 3


def _key_decay_kernel(x_ref, sel_ref, o_ref):
    x = x_ref[...]
    length = x.shape[0]
    row = lax.broadcasted_iota(I32, x.shape, 0)
    shift = 1
    while shift < length:
        x = x + jnp.where(row >= shift, pltpu.roll(x, shift, axis=0), 0.0)
        shift *= 2
    rest = x * (-LOG2E)
    out = jnp.zeros(o_ref.shape, F32)
    for j in range(N_DECAY_TERMS):
        term = rest.astype(BF16)
        rest = rest - term.astype(F32)
        out = out + jnp.dot(term, sel_ref[j], preferred_element_type=F32)
    o_ref[...] = out.astype(BF16)


def _decay_selectors(n_heads, width):
    sel = np.zeros((N_DECAY_TERMS, LANES, width), np.float32)
    for h in range(n_heads):
        base = (h // 2) * LANES + (HEAD_DIM if h % 2 == 0 else 0)
        for j in range(N_DECAY_TERMS):
            sel[j, h, base + j] = 1.0
    return jnp.asarray(sel, BF16)


def _key_decay(logf, n_heads, width):
    b, length, lanes = logf.shape
    sel = _decay_selectors(n_heads, width)
    return pl.pallas_call(
        _key_decay_kernel, out_shape=jax.ShapeDtypeStruct((b, length, width), BF16), grid=(b,),
        in_specs=[pl.BlockSpec((None, length, lanes), lambda i: (i, 0, 0)),
                  pl.BlockSpec(sel.shape, lambda i: (0, 0, 0))],
        out_specs=pl.BlockSpec((None, length, width), lambda i: (i, 0, 0)),
        compiler_params=_cparams(1), name="key_decay",
    )(logf, sel)


def _nt_dot(a, b):
    return lax.dot_general(a, b, (((1,), (1,)), ((), ())), preferred_element_type=F32)


SCORE_LOOKAHEAD = 4


def _pipelined(units, scores, consume):
    ahead = [scores(unit) for unit in units[:SCORE_LOOKAHEAD]]
    for n, unit in enumerate(units):
        s = ahead.pop(0)
        if n + SCORE_LOOKAHEAD < len(units):
            ahead.append(scores(units[n + SCORE_LOOKAHEAD]))
        consume(unit, s)


def _memo(fn):
    cache = {}

    def get(key):
        if key not in cache:
            cache[key] = fn(key)
        return cache[key]
    return get


CACHED_CHUNK_GROUP = 2


def _fox_cached_kernel(q_ref, kn_ref, vn_ref, cn_ref, cp_ref, kp_ref, vp_ref,
                       o_ref, acc_a, acc_b, m_a, m_b, *, tkp, n_past):
    t = q_ref.shape[0]
    lo = lax.broadcasted_iota(I32, (1, LANES), 1) < HEAD_DIM
    q = q_ref[...]
    zero = jnp.zeros_like(q)
    q_heads = (jnp.where(lo, q, zero), jnp.where(lo, zero, q))
    accs = (acc_a, acc_b)
    ms = (m_a, m_b)
    for acc, m in zip(accs, ms):
        acc[...] = jnp.zeros_like(acc)
        m[...] = jnp.full_like(m, NEG_INF)

    def run(chunks):
        @_memo
        def operands(n):
            k, v, c_rows, _ = chunks[n]
            v = v.astype(BF16)
            one = jnp.ones_like(v)
            return k.astype(BF16), (jnp.where(lo, v, one), jnp.where(lo, one, v)), c_rows * LOG2E

        def scores(unit):
            n, hd = unit
            return _nt_dot(q_heads[hd], operands(n)[0])

        def consume(unit, s):
            n, hd = unit
            _, v_heads, c2 = operands(n)
            s = s - c2[hd:hd + 1, :]
            if chunks[n][3] is not None:
                s = jnp.where(chunks[n][3], s, NEG_INF)
            m_old = ms[hd][...]
            m_new = jnp.maximum(m_old, jnp.max(s, axis=1, keepdims=True))
            p = jnp.exp2(s - m_new).astype(BF16)
            accs[hd][...] = jnp.exp2(m_old - m_new) * accs[hd][...] + jnp.dot(
                p, v_heads[hd], preferred_element_type=F32)
            ms[hd][...] = m_new

        _pipelined([(n, hd) for n in range(len(chunks)) for hd in range(2)], scores, consume)

    def cached(c):
        off = pl.multiple_of(c * tkp, tkp)
        return (kp_ref[pl.ds(off, tkp), :], vp_ref[pl.ds(off, tkp), :],
                cp_ref[:, pl.ds(off, tkp)], None)

    def past_body(j, carry):
        run([cached(j * CACHED_CHUNK_GROUP + g) for g in range(CACHED_CHUNK_GROUP)])
        return carry
    lax.fori_loop(0, n_past // CACHED_CHUNK_GROUP, past_body, 0)
    causal = lax.broadcasted_iota(I32, (t, t), 1) <= lax.broadcasted_iota(I32, (t, t), 0)
    run([cached(c) for c in range(n_past - n_past % CACHED_CHUNK_GROUP, n_past)]
        + [(kn_ref[...], vn_ref[...], cn_ref[...], causal)])

    a = acc_a[...]
    b = acc_b[...]
    out = jnp.where(lo, a / pltpu.roll(a, HEAD_DIM, axis=1), b / pltpu.roll(b, HEAD_DIM, axis=1))
    o_ref[...] = out.astype(o_ref.dtype)


def _fox_cached_attention(q, k_new, v_new, c_new, k_past, v_past, c_past, *, tkp):
    b, t, w = q.shape
    n_pairs = w // LANES
    plen = k_past.shape[1]
    n_past = plen // tkp
    assert n_past * tkp == plen
    pair_rows = lambda c: c.reshape(b, n_pairs, 2, c.shape[-1])
    seqspec = lambda n: pl.BlockSpec((None, n, LANES), lambda bi, p: (bi, 0, p))
    cspec = lambda n: pl.BlockSpec((None, None, 2, n), lambda bi, p: (bi, p, 0, 0))
    return pl.pallas_call(
        functools.partial(_fox_cached_kernel, tkp=tkp, n_past=n_past),
        out_shape=jax.ShapeDtypeStruct((b, t, w), BF16),
        grid=(b, n_pairs),
        in_specs=[seqspec(t), seqspec(t), seqspec(t), cspec(t), cspec(plen),
                  seqspec(plen), seqspec(plen)],
        out_specs=seqspec(t),
        scratch_shapes=[pltpu.VMEM((t, LANES), F32)] * 2 + [pltpu.VMEM((t, 1), F32)] * 2,
        compiler_params=_cparams(2),
        name="fox_cached_attn",
    )(q, k_new, v_new, pair_rows(c_new), pair_rows(c_past), k_past, v_past)


def _diff_lambda(dl, lambda_init):
    return (jnp.exp(jnp.sum(dl[0:1] * dl[1:2], axis=1, keepdims=True))
            - jnp.exp(jnp.sum(dl[2:3] * dl[3:4], axis=1, keepdims=True)) + lambda_init)


def _diff_cached_kernel(q_ref, kn_ref, vn_ref, bprev_ref, bdiag_ref, lam_ref, g_ref,
                        kp_ref, vp_ref, o_ref, acc0, acc1, m0, m1, l0, l1,
                        *, tkp, n_past, lambda_init):
    lo = lax.broadcasted_iota(I32, (1, LANES), 1) < HEAD_DIM
    q = q_ref[...]
    zero = jnp.zeros_like(q)
    q_maps = (jnp.where(lo, q, zero), jnp.where(lo, zero, q))
    accs, ms, ls = (acc0, acc1), (m0, m1), (l0, l1)
    for acc, m, l in zip(accs, ms, ls):
        acc[...] = jnp.zeros_like(acc)
        l[...] = jnp.zeros_like(l)
        m[...] = jnp.full_like(m, NEG_INF)

    def run(chunks):
        @_memo
        def operands(n):
            return chunks[n][0].astype(BF16), chunks[n][1].astype(BF16)

        def scores(unit):
            n, mp = unit
            return _nt_dot(q_maps[mp], operands(n)[0])

        def consume(unit, s):
            n, mp = unit
            if chunks[n][2] is not None:
                s = jnp.maximum(s, NEG_INF) + chunks[n][2]
            m_old = ms[mp][...]
            m_new = jnp.maximum(m_old, jnp.max(s, axis=1, keepdims=True))
            p = jnp.exp2(s - m_new)
            alpha = jnp.exp2(m_old - m_new)
            ls[mp][...] = alpha * ls[mp][...] + jnp.sum(p, axis=1, keepdims=True)
            accs[mp][...] = alpha * accs[mp][...] + jnp.dot(
                p.astype(BF16), operands(n)[1], preferred_element_type=F32)
            ms[mp][...] = m_new

        _pipelined([(n, mp) for n in range(len(chunks)) for mp in range(2)], scores, consume)

    def cached(c, bias=None):
        off = pl.multiple_of(c * tkp, tkp)
        return kp_ref[pl.ds(off, tkp), :], vp_ref[pl.ds(off, tkp), :], bias

    def past_body(j, carry):
        run([cached(j * CACHED_CHUNK_GROUP + g) for g in range(CACHED_CHUNK_GROUP)])
        return carry
    n_far = n_past - 1
    lax.fori_loop(0, n_far // CACHED_CHUNK_GROUP, past_body, 0)
    run([cached(c) for c in range(n_far - n_far % CACHED_CHUNK_GROUP, n_far)]
        + [cached(n_far, bprev_ref[...]), (kn_ref[...], vn_ref[...], bdiag_ref[...])])

    lam = _diff_lambda(lam_ref[...], lambda_init)
    o = acc0[...] / l0[...] - lam * (acc1[...] / l1[...])
    o = o * _rms_scale(o, SUBLN_EPS) * g_ref[...] * (1.0 - lambda_init)
    o_ref[...] = o.astype(o_ref.dtype)


def _diff_cached_attention(q, k_new, v_new, bias_prev, bias_diag, diff_lambda, subln_g,
                           k_past, v_past, *, tkp, lambda_init):
    b, t, w = q.shape
    plen = k_past.shape[1]
    n_past = plen // tkp
    assert n_past * tkp == plen and tkp >= MAX_DISTANCE and t <= CHUNK and plen % CHUNK == 0
    seqspec = lambda n: pl.BlockSpec((None, n, LANES), lambda bi, h: (bi, 0, h))
    head_tile = lambda arr: pl.BlockSpec((None,) + arr.shape[1:], lambda bi, h: (h, 0, 0))
    const2 = lambda arr: pl.BlockSpec(arr.shape, lambda bi, h: (0, 0))
    return pl.pallas_call(
        functools.partial(_diff_cached_kernel, tkp=tkp, n_past=n_past, lambda_init=lambda_init),
        out_shape=jax.ShapeDtypeStruct((b, t, w), BF16),
        grid=(b, w // LANES),
        in_specs=[seqspec(t), seqspec(t), seqspec(t), head_tile(bias_prev), head_tile(bias_diag),
                  const2(diff_lambda), const2(subln_g), seqspec(plen), seqspec(plen)],
        out_specs=seqspec(t),
        scratch_shapes=[pltpu.VMEM((t, LANES), F32)] * 2 + [pltpu.VMEM((t, 1), F32)] * 4,
        compiler_params=_cparams(2),
        name="diff_cached_attn",
    )(q, k_new, v_new, bias_prev, bias_diag, diff_lambda, subln_g, k_past, v_past)


def _fox_prompt_kernel(q_ref, k_ref, kx_ref, vt_ref, o_ref, acc_a, acc_b, m_a, m_b, *, tq, rs):
    i = pl.program_id(2)
    n_qs = tq // rs
    lane = lax.broadcasted_iota(I32, (1, LANES), 1)
    lo = lane < HEAD_DIM
    top = lax.broadcasted_iota(I32, (LANES, rs), 0) < HEAD_DIM

    q = q_ref[...].astype(F32)
    ones_a = jnp.where(lane < HEAD_DIM + N_DECAY_TERMS, 1.0, 0.0)
    ones_b = jnp.where(lane < N_DECAY_TERMS, 1.0, 0.0)
    q_heads = (jnp.where(lo, q, ones_a).astype(BF16), jnp.where(lo, ones_b, q).astype(BF16))
    accs = (acc_a, acc_b)
    ms = (m_a, m_b)
    for acc, m in zip(accs, ms):
        acc[...] = jnp.zeros_like(acc)
        m[...] = jnp.full_like(m, NEG_INF)

    def chunk_group(first, units):
        @_memo
        def operands(kc):
            off = pl.multiple_of((first + kc) * rs, rs)
            k = k_ref[pl.ds(off, rs), :]
            kx = kx_ref[pl.ds(off, rs), :]
            vt = vt_ref[:, pl.ds(off, rs)]
            ones_v = jnp.ones_like(vt)
            return ((jnp.where(lo, k, kx), jnp.where(lo, kx, k)),
                    (jnp.where(top, vt, ones_v), jnp.where(top, ones_v, vt)))

        def scores(unit):
            kc, hd, qs, _ = unit
            return _nt_dot(operands(kc)[0][hd], q_heads[hd][qs * rs:(qs + 1) * rs])

        def consume(unit, s):
            kc, hd, qs, mask = unit
            cols = pl.ds(qs * rs, rs)
            if mask is not None:
                s = jnp.where(mask, s, NEG_INF)
            m_old = ms[hd][:, cols]
            m_new = jnp.maximum(m_old, jnp.max(s, axis=0, keepdims=True))
            p = jnp.exp2(s - m_new).astype(BF16)
            accs[hd][:, cols] = jnp.exp2(m_old - m_new) * accs[hd][:, cols] + jnp.dot(
                operands(kc)[1][hd], p, preferred_element_type=F32)
            ms[hd][:, cols] = m_new

        _pipelined(units, scores, consume)

    def below(j, carry):
        chunk_group(n_qs * j, [(kc, hd, qs, None) for kc in range(n_qs) for qs in range(n_qs)
                               for hd in range(2)])
        return carry
    lax.fori_loop(0, i, below, 0)
    causal = (lax.broadcasted_iota(I32, (rs, rs), 0) <= lax.broadcasted_iota(I32, (rs, rs), 1))
    chunk_group(n_qs * i, [(kc, hd, qs, causal if qs == kc else None) for kc in range(n_qs)
                           for qs in range(kc, n_qs) for hd in range(2)])

    a = acc_a[...]
    b = acc_b[...]
    out_t = jnp.concatenate([a[:HEAD_DIM] / a[HEAD_DIM:HEAD_DIM + 1], b[HEAD_DIM:] / b[0:1]],
                            axis=0)
    o_ref[...] = out_t.T.astype(o_ref.dtype)


def _fox_prompt_attention(q, k, kx, vt, *, tq, rs):
    b, t, w = q.shape
    qspec = pl.BlockSpec((None, tq, LANES), lambda bi, p, i: (bi, i, p))
    kspec = pl.BlockSpec((None, t, LANES), lambda bi, p, i: (bi, 0, p))
    return pl.pallas_call(
        functools.partial(_fox_prompt_kernel, tq=tq, rs=rs),
        out_shape=jax.ShapeDtypeStruct((b, t, w), BF16),
        grid=(b, w // LANES, t // tq),
        in_specs=[qspec, kspec, kspec, pl.BlockSpec((LANES, t), lambda bi, p, i: (p, bi))],
        out_specs=qspec,
        scratch_shapes=[pltpu.VMEM((LANES, tq), F32), pltpu.VMEM((LANES, tq), F32),
                        pltpu.VMEM((1, tq), F32), pltpu.VMEM((1, tq), F32)],
        compiler_params=_cparams(3),
        name="fox_prompt_attn",
    )(q, k, kx, vt)


ONES_ROWS = 16


def _diff_prompt_kernel(q_ref, k_ref, vt_ref, bprev_ref, bdiag_ref, lam_ref, g_ref, o_ref,
                        acc0, acc1, m0, m1, *, tq, rs, lambda_init):
    i = pl.program_id(2)
    n_qs = tq // rs
    lo = lax.broadcasted_iota(I32, (1, LANES), 1) < HEAD_DIM

    q = q_ref[...]
    zero = jnp.zeros_like(q)
    q_maps = (jnp.where(lo, q, zero), jnp.where(lo, zero, q))
    accs, ms = (acc0, acc1), (m0, m1)
    for acc, m in zip(accs, ms):
        acc[...] = jnp.zeros_like(acc)
        m[...] = jnp.full_like(m, NEG_INF)

    def chunk_group(first, blocks_ahead):
        @_memo
        def operands(kc):
            off = pl.multiple_of((first + kc) * rs, rs)
            return (k_ref[pl.ds(off, rs), :],
                    jnp.concatenate([vt_ref[:, pl.ds(off, rs)], jnp.ones((ONES_ROWS, rs), BF16)],
                                    axis=0))

        def scores(unit):
            kc, mp, qs = unit
            return _nt_dot(operands(kc)[0], q_maps[mp][qs * rs:(qs + 1) * rs])

        def consume(unit, s):
            kc, mp, qs = unit
            cols = pl.ds(qs * rs, rs)
            distance = blocks_ahead + qs - kc
            if distance == 1:
                s = jnp.maximum(s, NEG_INF) + bprev_ref[...]
            elif distance == 0:
                s = jnp.maximum(s, NEG_INF) + bdiag_ref[...]
            m_old = ms[mp][:, cols]
            m_new = jnp.maximum(m_old, jnp.max(s, axis=0, keepdims=True))
            p = jnp.exp2(s - m_new).astype(BF16)
            accs[mp][:, cols] = jnp.exp2(m_old - m_new) * accs[mp][:, cols] + jnp.dot(
                operands(kc)[1], p, preferred_element_type=F32)
            ms[mp][:, cols] = m_new

        _pipelined([(kc, mp, qs) for kc in range(n_qs) for qs in range(n_qs) for mp in range(2)
                    if blocks_ahead + qs - kc >= 0], scores, consume)

    def far_below(j, carry):
        chunk_group(n_qs * j, 2 * n_qs)
        return carry
    lax.fori_loop(0, jnp.maximum(i - 1, 0), far_below, 0)

    @pl.when(i >= 1)
    def _():
        chunk_group(n_qs * (i - 1), n_qs)

    chunk_group(n_qs * i, 0)

    lam = _diff_lambda(lam_ref[...], lambda_init)
    a0 = acc0[...]
    a1 = acc1[...]
    o = a0[:LANES] / a0[LANES:LANES + 1] - lam * (a1[:LANES] / a1[LANES:LANES + 1])
    scale = lax.rsqrt(jnp.mean(o * o, axis=0, keepdims=True) + SUBLN_EPS)
    o = o * scale * g_ref[...] * (1.0 - lambda_init)
    o_ref[...] = o.T.astype(o_ref.dtype)


def _diff_prompt_attention(q, k, vt, bias_prev, bias_diag, diff_lambda, subln_g, *,
                           tq, rs, lambda_init):
    b, t, w = q.shape
    assert rs >= MAX_DISTANCE
    qspec = pl.BlockSpec((None, tq, LANES), lambda bi, h, i: (bi, i, h))
    head_tile = lambda arr: pl.BlockSpec((None,) + arr.shape[1:], lambda bi, h, i: (h, 0, 0))
    const2 = lambda arr: pl.BlockSpec(arr.shape, lambda bi, h, i: (0, 0))
    return pl.pallas_call(
        functools.partial(_diff_prompt_kernel, tq=tq, rs=rs, lambda_init=lambda_init),
        out_shape=jax.ShapeDtypeStruct((b, t, w), BF16),
        grid=(b, w // LANES, t // tq),
        in_specs=[qspec, pl.BlockSpec((None, t, LANES), lambda bi, h, i: (bi, 0, h)),
                  pl.BlockSpec((LANES, t), lambda bi, h, i: (h, bi)),
                  head_tile(bias_prev), head_tile(bias_diag), const2(diff_lambda),
                  const2(subln_g)],
        out_specs=qspec,
        scratch_shapes=[pltpu.VMEM((LANES + ONES_ROWS, tq), F32)] * 2
        + [pltpu.VMEM((1, tq), F32)] * 2,
        compiler_params=_cparams(3),
        name="diff_prompt_attn",
    )(q, k, vt, bias_prev, bias_diag, diff_lambda, subln_g)


def _t5_bucket(rel):
    nb = NUM_BUCKETS // 2
    max_exact = nb // 2
    ret = jnp.where(rel > 0, nb, 0)
    n = jnp.abs(rel)
    nf = jnp.maximum(n, max_exact).astype(F32)
    large = max_exact + (jnp.log(nf / max_exact) / math.log(MAX_DISTANCE / max_exact)
                         * (nb - max_exact)).astype(I32)
    large = jnp.minimum(large, nb - 1)
    return ret + jnp.where(n < max_exact, n, large)


def _bias_tiles(rel_table, tq, tk_prev):
    buckets = jnp.arange(NUM_BUCKETS, dtype=I32)

    def lookup(rel):
        hit = _t5_bucket(rel)[None, :] == buckets[:, None]
        return jnp.sum(jnp.where(hit[:, None, :], rel_table[:, :, None], 0.0), axis=0)

    far = lookup(jnp.full((1,), -4 * MAX_DISTANCE, I32))

    def tile(rel0, tk):
        period = tq + tk
        j = jnp.arange(period, dtype=I32)
        strip = lookup(rel0 + jnp.where(j < tk, j, j - period)) - far
        flat = jnp.tile(strip, (1, tq))[:, :tq * (period - 1)]
        return (flat.reshape(-1, tq, period - 1)[:, :, :tk] * LOG2E).astype(F32)

    return tile(-tk_prev, tk_prev), tile(0, tq)


def _router_kernel(x_ref, of_ref, od_ref, wof_ref, wod_ref, g_ref, wr_ref, br_ref,
                   x1_ref, hn_ref, te_ref, tw_ref, rank_ref, cnt_ref, *, tm):
    step = pl.program_id(0)

    @pl.when(step == 0)
    def _():
        cnt_ref[...] = jnp.zeros_like(cnt_ref)

    x1 = (x_ref[...] + jnp.dot(of_ref[...], wof_ref[...], preferred_element_type=F32)
          + jnp.dot(od_ref[...], wod_ref[...], preferred_element_type=F32))
    x1_ref[...] = x1
    hn = x1 * _rms_scale(x1, NORM_EPS) * g_ref[...]
    half = hn.shape[1] // 2
    hn_ref[...] = _pack_bf16_pairs(hn[:, :half], hn[:, half:])

    hn_hi = hn.astype(BF16)
    hn_lo = (hn - hn_hi.astype(F32)).astype(BF16)
    both = jnp.dot(hn_hi, wr_ref[...], preferred_element_type=F32)
    logits = (both[:, :LANES] + both[:, LANES:] + br_ref[...]
              + jnp.dot(hn_lo, wr_ref[:, :LANES], preferred_element_type=F32))
    lane = lax.broadcasted_iota(I32, logits.shape, 1)
    lane_f = lane.astype(F32)
    sel = jnp.zeros(logits.shape, F32)
    vals, idxs = [], []
    for _ in range(TOP_K):
        mx = jnp.max(logits, axis=1, keepdims=True)
        idx = jnp.min(jnp.where(logits == mx, lane_f, float(LANES)), axis=1, keepdims=True)
        hit = lane_f == idx
        sel = jnp.where(hit, 1.0, sel)
        logits = jnp.where(hit, -jnp.inf, logits)
        vals.append(mx)
        idxs.append(idx)
    exps = [jnp.exp(v - vals[0]) for v in vals]
    denom = exps[0] + exps[1] + exps[2] + exps[3]

    earlier = (lax.broadcasted_iota(I32, (tm, tm), 1)
               < lax.broadcasted_iota(I32, (tm, tm), 0)).astype(BF16)
    before = cnt_ref[...] + jnp.dot(earlier, sel.astype(BF16), preferred_element_type=F32)
    for k in range(TOP_K):
        te_ref[:, k:k + 1] = idxs[k].astype(I32)
        tw_ref[:, k:k + 1] = exps[k] / denom
        rank_ref[:, k:k + 1] = jnp.sum(
            jnp.where(lane_f == idxs[k], before, 0.0), axis=1, keepdims=True).astype(I32)
    cnt_ref[...] += jnp.sum(sel, axis=0, keepdims=True)


def _router(x2d, o_fox, o_diff, wo_f, wo_d, g_ffn, w_r, b_r, *, tm):
    n, d = x2d.shape
    row = lambda i: (i, 0)
    const = lambda i: (0, 0)
    full = lambda arr: pl.BlockSpec(arr.shape, const)
    return pl.pallas_call(
        functools.partial(_router_kernel, tm=tm),
        out_shape=[jax.ShapeDtypeStruct((n, d), F32), jax.ShapeDtypeStruct((n, d // 2), U32),
                   jax.ShapeDtypeStruct((n, TOP_K), I32), jax.ShapeDtypeStruct((n, TOP_K), F32),
                   jax.ShapeDtypeStruct((n, TOP_K), I32), jax.ShapeDtypeStruct((1, LANES), F32)],
        grid=(n // tm,),
        in_specs=[pl.BlockSpec((tm, d), row), pl.BlockSpec((tm, o_fox.shape[1]), row),
                  pl.BlockSpec((tm, o_diff.shape[1]), row), full(wo_f), full(wo_d),
                  full(g_ffn), full(w_r), full(b_r)],
        out_specs=[pl.BlockSpec((tm, d), row), pl.BlockSpec((tm, d // 2), row),
                   pl.BlockSpec((tm, TOP_K), row), pl.BlockSpec((tm, TOP_K), row),
                   pl.BlockSpec((tm, TOP_K), row), pl.BlockSpec((1, LANES), const)],
        compiler_params=_cparams(1),
        name="router",
    )(x2d, o_fox, o_diff, wo_f, wo_d, g_ffn, w_r, b_r)


def _row_copy(src, src_row, dst, dst_row, sem):
    return pltpu.make_async_copy(src.at[pl.ds(src_row, 1), :], dst.at[pl.ds(dst_row, 1), :], sem)


def _dispatch_kernel(tail_start_ref, tail_valid_ref, pos_ref, hn_ref, xs_hbm, zeros_ref, sem,
                     *, tm, tme, n_experts):
    def tail_copy(e):
        return pltpu.make_async_copy(
            zeros_ref, xs_hbm.at[pl.ds(pl.multiple_of(tail_start_ref[e], tme), tme), :], sem.at[0])

    @pl.when(pl.program_id(0) == 0)
    def _():
        zeros_ref[...] = jnp.zeros_like(zeros_ref)
        for e in range(n_experts):
            @pl.when(tail_valid_ref[e] > 0)
            def _():
                tail_copy(e).start()
        for e in range(n_experts):
            @pl.when(tail_valid_ref[e] > 0)
            def _():
                tail_copy(e).wait()

    def start_row(r, carry):
        for k in range(TOP_K):
            _row_copy(hn_ref, r, xs_hbm, pos_ref[r, k], sem.at[1]).start(priority=k % 2)
        return carry

    def wait_row(r, carry):
        for k in range(TOP_K):
            _row_copy(hn_ref, r, xs_hbm, pos_ref[r, k], sem.at[1]).wait()
        return carry

    lax.fori_loop(0, tm, start_row, 0, unroll=True)
    lax.fori_loop(0, tm, wait_row, 0, unroll=8)


def _dispatch(hn_packed, pos, tail_start, tail_valid, *, m_pad, tm, tme):
    n, half = hn_packed.shape
    n_experts = tail_start.shape[0]
    grid_spec = pltpu.PrefetchScalarGridSpec(
        num_scalar_prefetch=2,
        grid=(n // tm,),
        in_specs=[pl.BlockSpec((tm, TOP_K), lambda i, *_: (i, 0), memory_space=pltpu.SMEM),
                  pl.BlockSpec((tm, half), lambda i, *_: (i, 0))],
        out_specs=pl.BlockSpec(memory_space=pl.ANY),
        scratch_shapes=[pltpu.VMEM((tme, half), U32), pltpu.SemaphoreType.DMA((2,))],
    )
    return pl.pallas_call(
        functools.partial(_dispatch_kernel, tm=tm, tme=tme, n_experts=n_experts),
        out_shape=jax.ShapeDtypeStruct((m_pad, half), U32),
        grid_spec=grid_spec,
        compiler_params=pltpu.CompilerParams(
            dimension_semantics=("arbitrary",), vmem_limit_bytes=VMEM_LIMIT_BYTES,
            has_side_effects=True),
        name="dispatch",
    )(tail_start, tail_valid, pos, hn_packed)


def _experts_kernel(te_ref, nact_ref, xs_ref, wg_ref, bg_ref, wu_ref, bu_ref, wd_ref, bd_ref,
                    ys_ref):
    @pl.when(pl.program_id(0) < nact_ref[0])
    def _():
        half = xs_ref.shape[1]
        x_lo, x_hi = _unpack_bf16_pairs(xs_ref[...])
        x_lo = x_lo.astype(BF16)
        x_hi = x_hi.astype(BF16)

        def proj(w_ref, b_ref):
            return (jnp.dot(x_lo, w_ref[:half, :], preferred_element_type=F32)
                    + jnp.dot(x_hi, w_ref[half:, :], preferred_element_type=F32) + b_ref[...])

        g = jnp.minimum(proj(wg_ref, bg_ref), SWIGLU_LIMIT)
        u = jnp.clip(proj(wu_ref, bu_ref), -SWIGLU_LIMIT, SWIGLU_LIMIT)
        a = ((u + 1.0) * (g * jax.nn.sigmoid(SWIGLU_ALPHA * g))).astype(BF16)
        y = jnp.dot(a, wd_ref[...], preferred_element_type=F32) + bd_ref[...]
        ys_ref[...] = _pack_bf16_pairs(y[:, :half], y[:, half:])


def _experts(xs, tile_expert, n_active, w_gate, b_gate, w_up, b_up, w_down, b_down, *, tme):
    m_pad, half = xs.shape
    n_tiles = m_pad // tme
    d, d_ff = w_gate.shape[1:]
    row = lambda t, te, na: (jnp.minimum(t, na[0] - 1), 0)
    wspec = lambda shape: pl.BlockSpec((None,) + shape, lambda t, te, na: (te[t], 0, 0))
    grid_spec = pltpu.PrefetchScalarGridSpec(
        num_scalar_prefetch=2,
        grid=(n_tiles,),
        in_specs=[pl.BlockSpec((tme, half), row),
                  wspec((d, d_ff)), wspec((1, d_ff)), wspec((d, d_ff)), wspec((1, d_ff)),
                  wspec((d_ff, d)), wspec((1, d))],
        out_specs=pl.BlockSpec((tme, half), row),
    )
    return pl.pallas_call(
        _experts_kernel,
        out_shape=jax.ShapeDtypeStruct((m_pad, half), U32),
        grid_spec=grid_spec,
        compiler_params=_cparams(1),
        name="experts",
    )(tile_expert, n_active, xs, w_gate, b_gate, w_up, b_up, w_down, b_down)


def _combine_kernel(pos_ref, pos_next_ref, x1_ref, tw_ref, p_ref, wg_ref, wp_ref, gple_ref,
                    gfin_ref, ys_hbm, y_ref, gbuf, sem, *, tm):
    step = pl.program_id(0)
    slot = step % 2

    def gather(tile_pos_ref, tile_slot, wait):
        def row(r, carry):
            for k in range(TOP_K):
                copy = _row_copy(ys_hbm, tile_pos_ref[r, k], gbuf.at[tile_slot, k], r,
                                 sem.at[tile_slot])
                if wait:
                    copy.wait()
                else:
                    copy.start(priority=k % 2)
            return carry
        lax.fori_loop(0, tm, row, 0, unroll=8)

    @pl.when(step == 0)
    def _():
        gather(pos_ref, 0, wait=False)

    @pl.when(step + 1 < pl.num_programs(0))
    def _():
        gather(pos_next_ref, 1 - slot, wait=False)

    gather(pos_ref, slot, wait=True)

    x1 = x1_ref[...]
    half = x1.shape[1] // 2
    tw = tw_ref[...]
    moe_lo = jnp.zeros((tm, half), F32)
    moe_hi = jnp.zeros((tm, half), F32)
    for k in range(TOP_K):
        lo, hi = _unpack_bf16_pairs(gbuf[slot, k])
        moe_lo = moe_lo + tw[:, k:k + 1] * lo
        moe_hi = moe_hi + tw[:, k:k + 1] * hi
    x2 = x1 + jnp.concatenate([moe_lo, moe_hi], axis=1)
    hp = (x2 * _rms_scale(x2, NORM_EPS) * gple_ref[...]).astype(BF16)
    gate = jax.nn.sigmoid(jnp.dot(hp, wg_ref[...], preferred_element_type=F32))
    proj = jnp.dot(p_ref[...].astype(BF16), wp_ref[...], preferred_element_type=F32)
    x3 = x2 + proj * gate
    y_ref[...] = x3 * _rms_scale(x3, NORM_EPS) * gfin_ref[...]


def _combine(x1, top_w, pos, ys, p2d, w_ple_gate, w_ple_proj, g_ple, g_final, *, tm):
    n, d = x1.shape
    half = ys.shape[1]
    row = lambda i: (i, 0)
    const = lambda i: (0, 0)
    full = lambda arr: pl.BlockSpec(arr.shape, const)
    return pl.pallas_call(
        functools.partial(_combine_kernel, tm=tm),
        out_shape=jax.ShapeDtypeStruct((n, d), F32),
        grid=(n // tm,),
        in_specs=[pl.BlockSpec((tm, TOP_K), row, memory_space=pltpu.SMEM),
                  pl.BlockSpec((tm, TOP_K), lambda i: (jnp.minimum(i + 1, n // tm - 1), 0),
                               memory_space=pltpu.SMEM),
                  pl.BlockSpec((tm, d), row), pl.BlockSpec((tm, TOP_K), row),
                  pl.BlockSpec((tm, p2d.shape[1]), row), full(w_ple_gate), full(w_ple_proj),
                  full(g_ple), full(g_final), pl.BlockSpec(memory_space=pl.ANY)],
        out_specs=pl.BlockSpec((tm, d), row),
        scratch_shapes=[pltpu.VMEM((2, TOP_K, tm, half), U32), pltpu.SemaphoreType.DMA((2,))],
        compiler_params=_cparams(1),
        name="combine",
    )(pos, pos, x1, top_w, p2d, w_ple_gate, w_ple_proj, g_ple, g_final, ys)


def _pick_tile(n, want):
    t = min(n, want)
    assert n % t == 0, (n, t)
    return t


def _layer_group(x, p, past, wts, lambda_init, g_final):
    (g_mix, w_main, w_vt, w_f, b_f, diff_lambda, subln_g, rel_table, wo_f, wo_d, g_ffn, w_r, b_r,
     w_gate, b_gate, w_up, b_up, w_down, b_down, g_ple, w_ple_gate, w_ple_proj) = wts
    b, t, d = x.shape
    n = b * t
    n_fox = (w_main.shape[1] // 6) // HEAD_DIM
    n_experts = w_gate.shape[0]
    x2d = x.reshape(n, d)

    tm = _pick_tile(n, 512)
    (qf, qd, kf, vf, kd, vd, kf16, kd16, vf16, vd16, logf_lanes) = _inproj(
        x2d, g_mix, w_main, w_vt, w_f, b_f, tm=tm, values_transposed=past is None)
    logf = logf_lanes[:, :n_fox]
    width = qf.shape[1]
    as_seq = lambda a: a.reshape(b, t, a.shape[-1])

    tq = _pick_tile(t, 2048)
    rs = _pick_tile(tq, 256)
    if past is None:
        kx = _key_decay(as_seq(logf_lanes), n_fox, width)
        o_fox = _fox_prompt_attention(as_seq(qf), as_seq(kf16), kx, vf16, tq=tq, rs=rs)
        bias_prev, bias_diag = _bias_tiles(rel_table, rs, rs)
        block = jnp.arange(rs, dtype=I32) // CHUNK
        bias_diag = jnp.where(block[:, None] <= block[None, :], jnp.swapaxes(bias_diag, 1, 2),
                              NEG_INF)
        o_diff = _diff_prompt_attention(
            as_seq(qd), as_seq(kd16), vd16, jnp.swapaxes(bias_prev, 1, 2), bias_diag,
            diff_lambda, subln_g.reshape(-1, 1), tq=tq, rs=rs, lambda_init=lambda_init)
    else:
        pk_f, pv_f, plogf, pk_d, pv_d = past
        plen = pk_f.shape[1]
        tkp = 512
        total = plen + t
        padded = -(-total // LANES) * LANES
        seq = jnp.concatenate([jnp.transpose(plogf.astype(F32), (0, 2, 1)),
                               jnp.transpose(as_seq(logf), (0, 2, 1))], axis=2)
        c_all = _cumsum_lanes(jnp.pad(seq, ((0, 0), (0, 0), (0, padded - total))))
        flat = lambda a: a.reshape(b, plen, width)
        o_fox = _fox_cached_attention(as_seq(qf), as_seq(kf16), as_seq(vf16),
                                      c_all[:, :, plen:total], flat(pk_f), flat(pv_f),
                                      c_all[:, :, :plen], tkp=tkp)
        bias_prev, bias_diag = _bias_tiles(rel_table, t, tkp)
        o_diff = _diff_cached_attention(as_seq(qd), as_seq(kd16), as_seq(vd16), bias_prev,
                                        bias_diag, diff_lambda, subln_g, flat(pk_d), flat(pv_d),
                                        tkp=tkp, lambda_init=lambda_init)

    tmr = _pick_tile(n, 256)
    x1, hn_packed, top_e, top_w, rank, counts = _router(
        x2d, o_fox.reshape(n, width), o_diff.reshape(n, width), wo_f, wo_d, g_ffn, w_r, b_r, tm=tmr)

    tme = 512 if n * TOP_K >= 64 * 512 else 256
    n_tiles = -(-(n * TOP_K) // tme) + n_experts
    m_pad = n_tiles * tme
    cnt = counts[0, :n_experts].astype(I32)
    group = -(-cnt // tme) * tme
    ends = jnp.cumsum(group)
    starts = ends - group
    experts = jnp.arange(n_experts, dtype=I32)
    pos = rank + jnp.sum(jnp.where(top_e[:, :, None] == experts, starts, 0), axis=2)
    n_active = (ends[-1] // tme).astype(I32)
    tile_ids = jnp.arange(n_tiles, dtype=I32)
    tile_expert = jnp.sum(jnp.minimum(tile_ids, n_active - 1)[:, None] >= (ends // tme)[None, :],
                          axis=1).astype(I32)
    tile_expert = jnp.minimum(tile_expert, n_experts - 1)
    tail_start = jnp.maximum(ends - tme, 0).astype(I32)
    tail_valid = (group > 0).astype(I32)

    tmd = _pick_tile(n, 256)
    xs = _dispatch(hn_packed, pos, tail_start, tail_valid, m_pad=m_pad, tm=tmd, tme=tme)
    ys = _experts(xs, tile_expert, n_active.reshape(1), w_gate, b_gate, w_up, b_up, w_down, b_down,
                  tme=tme)
    y = _combine(x1, top_w, pos, ys, p.reshape(n, p.shape[-1]), w_ple_gate, w_ple_proj, g_ple,
                 g_final, tm=tmd)
    return y.reshape(b, t, d), (kf, vf, logf, kd, vd)


def kernel(x_prompt, x_sample, p_prompt, p_sample, cache_fox_k, cache_fox_v, cache_fox_logf, cache_diff_k, cache_diff_v, g_mix, w_in, b_forget, diff_lambda, diff_subln_g, rel_bias_table, w_o, g_ffn, w_router, b_router, w_gate, b_gate, w_up, b_up, w_down, b_down, g_ple, w_ple_gate, w_ple_proj, g_final):
    depth = g_mix.shape[0]
    assert depth == 1, "the final norm is fused into the last layer; one layer supported"
    d = x_prompt.shape[-1]
    n_fox = cache_fox_k.shape[3]
    fox_w = n_fox * cache_fox_k.shape[4]
    n_diff = cache_diff_k.shape[3]
    diff_w = n_diff * 2 * cache_diff_k.shape[5]
    assert fox_w == diff_w and fox_w % LANES == 0
    n_experts = w_router.shape[-1]
    assert TOP_K <= n_experts <= LANES
    row2 = lambda a: a.reshape(1, -1).astype(F32)

    i = 0
    lambda_init = 0.8 - 0.6 * math.exp(-0.3 * i)
    off_fk, off_fv, off_ff = fox_w, 2 * fox_w, 3 * fox_w
    off_dq = off_ff + n_fox
    off_dk, off_dv = off_dq + diff_w, off_dq + 2 * diff_w
    w = w_in[i]
    cols = lambda o, wd: w[:, o:o + wd]
    w_main = jnp.concatenate([cols(0, fox_w), cols(off_dq, diff_w), cols(off_fk, fox_w),
                              cols(off_fv, fox_w), cols(off_dk, diff_w), cols(off_dv, diff_w)],
                             axis=1).astype(BF16)
    w_vt = jnp.concatenate([cols(off_fv, fox_w), cols(off_dv, diff_w)], axis=1).T.astype(BF16)
    w_f = jnp.pad(cols(off_ff, n_fox), ((0, 0), (0, LANES - n_fox))).astype(BF16)
    b_f = jnp.pad(row2(b_forget[i]), ((0, 0), (0, LANES - n_fox)))
    w_r = jnp.pad(w_router[i].astype(F32), ((0, 0), (0, LANES - n_experts)))
    w_r_hi = w_r.astype(BF16)
    w_r = jnp.concatenate([w_r_hi, (w_r - w_r_hi.astype(F32)).astype(BF16)], axis=1)
    b_r = jnp.pad(row2(b_router[i]), ((0, 0), (0, LANES - n_experts)), constant_values=NEG_INF)
    wo = w_o[i].astype(BF16)
    bias3 = lambda a: a.reshape(n_experts, 1, -1).astype(F32)
    wts = (row2(g_mix[i]), w_main, w_vt, w_f, b_f, diff_lambda[i].astype(F32), row2(diff_subln_g[i]),
           rel_bias_table.astype(F32), wo[:fox_w], wo[fox_w:], row2(g_ffn[i]), w_r, b_r,
           w_gate[i].astype(BF16), bias3(b_gate[i]), w_up[i].astype(BF16), bias3(b_up[i]),
           w_down[i].astype(BF16), bias3(b_down[i]), row2(g_ple[i]),
           w_ple_gate[i].astype(BF16), w_ple_proj[i].astype(BF16))
    gf = row2(g_final)

    yp, new_p = _layer_group(x_prompt, p_prompt[i], None, wts, lambda_init, gf)
    past_s = (cache_fox_k[i], cache_fox_v[i], cache_fox_logf[i], cache_diff_k[i], cache_diff_v[i])
    ys, new_s = _layer_group(x_sample, p_sample[i], past_s, wts, lambda_init, gf)

    def rows(new, b, t):
        kf, vf, logf, kd, vd = new
        return (kf.reshape(1, b, t, n_fox, -1), vf.reshape(1, b, t, n_fox, -1),
                logf.reshape(1, b, t, n_fox), kd.reshape(1, b, t, n_diff, 2, -1),
                vd.reshape(1, b, t, n_diff, -1))

    bp, tp = x_prompt.shape[:2]
    bs, ts = x_sample.shape[:2]
    return (yp, ys) + rows(new_p, bp, tp) + rows(new_s, bs, ts)
```

```python
import functools
import math

import jax
import jax.numpy as jnp
import numpy as np
from jax import lax
from jax.experimental import pallas as pl
from jax.experimental.pallas import tpu as pltpu

HEAD_DIM = 64
CHUNK = 64
NUM_BUCKETS = 32
MAX_DISTANCE = 128
TOP_K = 4
SWIGLU_LIMIT = 7.0
SWIGLU_ALPHA = 1.702
NORM_EPS = 1e-6
SUBLN_EPS = 1e-5
NEG_INF = -1e30
LOG2E = math.log2(math.e)

LANES = 128
VMEM_LIMIT_BYTES = 56 * 1024 * 1024

F32 = jnp.float32
BF16 = jnp.bfloat16
I32 = jnp.int32
U32 = jnp.uint32


def _cparams(n_axes):
    return pltpu.CompilerParams(
        dimension_semantics=("arbitrary",) * n_axes, vmem_limit_bytes=VMEM_LIMIT_BYTES)


def _rms_scale(x, eps):
    return lax.rsqrt(jnp.mean(x * x, axis=-1, keepdims=True) + eps)


def _pack_bf16_pairs(lo, hi):
    lo_bits = pltpu.bitcast(lo.astype(BF16).astype(F32), U32)
    hi_bits = pltpu.bitcast(hi.astype(BF16).astype(F32), U32)
    return lo_bits | (hi_bits >> 16)


def _unpack_bf16_pairs(packed):
    lo = pltpu.bitcast(packed & jnp.uint32(0xFFFF0000), F32)
    hi = pltpu.bitcast(packed << 16, F32)
    return lo, hi


def _store_token_slabs(ref, rows):
    n, width = rows.shape
    parts = width // LANES
    for j in range(parts):
        ref[pl.ds(j, n, stride=parts), :] = rows[:, j * LANES:(j + 1) * LANES]


def _load_token_slabs(ref, n, parts):
    return jnp.concatenate([ref[pl.ds(j, n, stride=parts), :] for j in range(parts)], axis=1)


def _inproj_kernel(x_ref, g_ref, w_ref, wvt_ref, wf_ref, bf_ref,
                   qf_ref, qd_ref, kf_ref, vf_ref, kd_ref, vd_ref,
                   kf16_ref, kd16_ref, vf16_ref, vd16_ref, logf_ref, *, width, values_transposed):
    x = x_ref[...]
    h = (x * _rms_scale(x, NORM_EPS) * g_ref[...]).astype(BF16)
    q_scale = LOG2E / math.sqrt(HEAD_DIM)

    def proj(c):
        return jnp.dot(h, w_ref[:, c * width:(c + 1) * width], preferred_element_type=F32)

    qf_ref[...] = (proj(0) * q_scale).astype(BF16)
    qd_ref[...] = (proj(1) * q_scale).astype(BF16)
    for c, (o32, o16) in enumerate(((kf_ref, kf16_ref), (vf_ref, vf16_ref),
                                    (kd_ref, kd16_ref), (vd_ref, vd16_ref))):
        u = proj(2 + c)
        if len(o32.shape) == 3:
            o32[...] = pltpu.einshape("m(hd)->mhd", u, h=o32.shape[1])
        else:
            o32[...] = pltpu.einshape("m(hed)->mhed", u, h=o32.shape[1], e=o32.shape[2])
        if not (values_transposed and c in (1, 3)):
            o16[...] = u.astype(BF16)
    if values_transposed:
        vf16_ref[...] = _nt_dot(wvt_ref[:width, :], h).astype(BF16)
        vd16_ref[...] = _nt_dot(wvt_ref[width:, :], h).astype(BF16)
    f = jnp.dot(h, wf_ref[...], preferred_element_type=F32) + bf_ref[...]
    logf_ref[...] = jax.nn.log_sigmoid(f)


def _inproj(x2d, g_mix, w_main, w_vt, w_f, b_f, *, tm, values_transposed):
    n, d = x2d.shape
    width = w_main.shape[1] // 6
    row = lambda i: (i, 0)
    const = lambda i: (0, 0)
    if values_transposed:
        v16_shape, v16_spec = (width, n), pl.BlockSpec((width, tm), lambda i: (0, i))
    else:
        v16_shape, v16_spec = (n, width), pl.BlockSpec((tm, width), row)
    head_shapes = [(width // HEAD_DIM, HEAD_DIM)] * 2 + [
        (width // (2 * HEAD_DIM), 2, HEAD_DIM), (width // (2 * HEAD_DIM), 2 * HEAD_DIM)]
    out_shape = ([jax.ShapeDtypeStruct((n, width), BF16)] * 2
                 + [jax.ShapeDtypeStruct((n,) + s, F32) for s in head_shapes]
                 + [jax.ShapeDtypeStruct((n, width), BF16)] * 2
                 + [jax.ShapeDtypeStruct(v16_shape, BF16)] * 2
                 + [jax.ShapeDtypeStruct((n, LANES), F32)])
    out_specs = ([pl.BlockSpec((tm, width), row)] * 2
                 + [pl.BlockSpec((tm,) + s, lambda i, s=s: (i,) + (0,) * len(s))
                    for s in head_shapes]
                 + [pl.BlockSpec((tm, width), row)] * 2 + [v16_spec] * 2
                 + [pl.BlockSpec((tm, LANES), row)])
    return pl.pallas_call(
        functools.partial(_inproj_kernel, width=width, values_transposed=values_transposed),
        out_shape=out_shape,
        grid=(n // tm,),
        in_specs=[pl.BlockSpec((tm, d), row), pl.BlockSpec((1, d), const),
                  pl.BlockSpec(w_main.shape, const), pl.BlockSpec(w_vt.shape, const),
                  pl.BlockSpec(w_f.shape, const), pl.BlockSpec(b_f.shape, const)],
        out_specs=out_specs,
        compiler_params=_cparams(1),
        name="inproj",
    )(x2d, g_mix, w_main, w_vt, w_f, b_f)


def _cumsum_kernel(x_ref, o_ref):
    x = x_ref[...]
    length = x.shape[1]
    lane = lax.broadcasted_iota(I32, x.shape, 1)
    shift = 1
    while shift < length:
        x = x + jnp.where(lane >= shift, pltpu.roll(x, shift, axis=1), 0.0)
        shift *= 2
    o_ref[...] = x


def _cumsum_lanes(x):
    b, h, length = x.shape
    spec = pl.BlockSpec((None, h, length), lambda i: (i, 0, 0))
    return pl.pallas_call(
        _cumsum_kernel, out_shape=jax.ShapeDtypeStruct(x.shape, F32), grid=(b,),
        in_specs=[spec], out_specs=spec, compiler_params=_cparams(1), name="cumsum",
    )(x)


N_DECAY_TERMS = 3


def _key_decay_kernel(x_ref, sel_ref, o_ref):
    x = x_ref[...]
    length = x.shape[0]
    row = lax.broadcasted_iota(I32, x.shape, 0)
    shift = 1
    while shift < length:
        x = x + jnp.where(row >= shift, pltpu.roll(x, shift, axis=0), 0.0)
        shift *= 2
    rest = x * (-LOG2E)
    out = jnp.zeros(o_ref.shape, F32)
    for j in range(N_DECAY_TERMS):
        term = rest.astype(BF16)
        rest = rest - term.astype(F32)
        out = out + jnp.dot(term, sel_ref[j], preferred_element_type=F32)
    o_ref[...] = out.astype(BF16)


def _decay_selectors(n_heads, width):
    sel = np.zeros((N_DECAY_TERMS, LANES, width), np.float32)
    for h in range(n_heads):
        base = (h // 2) * LANES + (HEAD_DIM if h % 2 == 0 else 0)
        for j in range(N_DECAY_TERMS):
            sel[j, h, base + j] = 1.0
    return jnp.asarray(sel, BF16)


def _key_decay(logf, n_heads, width):
    b, length, lanes = logf.shape
    sel = _decay_selectors(n_heads, width)
    return pl.pallas_call(
        _key_decay_kernel, out_shape=jax.ShapeDtypeStruct((b, length, width), BF16), grid=(b,),
        in_specs=[pl.BlockSpec((None, length, lanes), lambda i: (i, 0, 0)),
                  pl.BlockSpec(sel.shape, lambda i: (0, 0, 0))],
        out_specs=pl.BlockSpec((None, length, width), lambda i: (i, 0, 0)),
        compiler_params=_cparams(1), name="key_decay",
    )(logf, sel)


def _nt_dot(a, b):
    return lax.dot_general(a, b, (((1,), (1,)), ((), ())), preferred_element_type=F32)


SCORE_LOOKAHEAD = 4


def _pipelined(units, scores, consume):
    ahead = [scores(unit) for unit in units[:SCORE_LOOKAHEAD]]
    for n, unit in enumerate(units):
        s = ahead.pop(0)
        if n + SCORE_LOOKAHEAD < len(units):
            ahead.append(scores(units[n + SCORE_LOOKAHEAD]))
        consume(unit, s)


def _memo(fn):
    cache = {}

    def get(key):
        if key not in cache:
            cache[key] = fn(key)
        return cache[key]
    return get


CACHED_CHUNK_GROUP = 2


def _fox_cached_kernel(q_ref, kn_ref, vn_ref, cn_ref, cp_ref, kp_ref, vp_ref,
                       o_ref, acc_a, acc_b, m_a, m_b, *, tkp, n_past):
    t = q_ref.shape[0]
    lo = lax.broadcasted_iota(I32, (1, LANES), 1) < HEAD_DIM
    q = q_ref[...]
    zero = jnp.zeros_like(q)
    q_heads = (jnp.where(lo, q, zero), jnp.where(lo, zero, q))
    accs = (acc_a, acc_b)
    ms = (m_a, m_b)
    for acc, m in zip(accs, ms):
        acc[...] = jnp.zeros_like(acc)
        m[...] = jnp.full_like(m, NEG_INF)

    def run(chunks):
        @_memo
        def operands(n):
            k, v, c_rows, _ = chunks[n]
            v = v.astype(BF16)
            one = jnp.ones_like(v)
            return k.astype(BF16), (jnp.where(lo, v, one), jnp.where(lo, one, v)), c_rows * LOG2E

        def scores(unit):
            n, hd = unit
            return _nt_dot(q_heads[hd], operands(n)[0])

        def consume(unit, s):
            n, hd = unit
            _, v_heads, c2 = operands(n)
            s = s - c2[hd:hd + 1, :]
            if chunks[n][3] is not None:
                s = jnp.where(chunks[n][3], s, NEG_INF)
            m_old = ms[hd][...]
            m_new = jnp.maximum(m_old, jnp.max(s, axis=1, keepdims=True))
            p = jnp.exp2(s - m_new).astype(BF16)
            accs[hd][...] = jnp.exp2(m_old - m_new) * accs[hd][...] + jnp.dot(
                p, v_heads[hd], preferred_element_type=F32)
            ms[hd][...] = m_new

        _pipelined([(n, hd) for n in range(len(chunks)) for hd in range(2)], scores, consume)

    def cached(c):
        off = pl.multiple_of(c * tkp, tkp)
        return (kp_ref[pl.ds(off, tkp), :], vp_ref[pl.ds(off, tkp), :],
                cp_ref[:, pl.ds(off, tkp)], None)

    def past_body(j, carry):
        run([cached(j * CACHED_CHUNK_GROUP + g) for g in range(CACHED_CHUNK_GROUP)])
        return carry
    lax.fori_loop(0, n_past // CACHED_CHUNK_GROUP, past_body, 0)
    causal = lax.broadcasted_iota(I32, (t, t), 1) <= lax.broadcasted_iota(I32, (t, t), 0)
    run([cached(c) for c in range(n_past - n_past % CACHED_CHUNK_GROUP, n_past)]
        + [(kn_ref[...], vn_ref[...], cn_ref[...], causal)])

    a = acc_a[...]
    b = acc_b[...]
    out = jnp.where(lo, a / pltpu.roll(a, HEAD_DIM, axis=1), b / pltpu.roll(b, HEAD_DIM, axis=1))
    o_ref[...] = out.astype(o_ref.dtype)


def _fox_cached_attention(q, k_new, v_new, c_new, k_past, v_past, c_past, *, tkp):
    b, t, w = q.shape
    n_pairs = w // LANES
    plen = k_past.shape[1]
    n_past = plen // tkp
    assert n_past * tkp == plen
    pair_rows = lambda c: c.reshape(b, n_pairs, 2, c.shape[-1])
    seqspec = lambda n: pl.BlockSpec((None, n, LANES), lambda bi, p: (bi, 0, p))
    cspec = lambda n: pl.BlockSpec((None, None, 2, n), lambda bi, p: (bi, p, 0, 0))
    return pl.pallas_call(
        functools.partial(_fox_cached_kernel, tkp=tkp, n_past=n_past),
        out_shape=jax.ShapeDtypeStruct((b, t, w), BF16),
        grid=(b, n_pairs),
        in_specs=[seqspec(t), seqspec(t), seqspec(t), cspec(t), cspec(plen),
                  seqspec(plen), seqspec(plen)],
        out_specs=seqspec(t),
        scratch_shapes=[pltpu.VMEM((t, LANES), F32)] * 2 + [pltpu.VMEM((t, 1), F32)] * 2,
        compiler_params=_cparams(2),
        name="fox_cached_attn",
    )(q, k_new, v_new, pair_rows(c_new), pair_rows(c_past), k_past, v_past)


def _diff_lambda(dl, lambda_init):
    return (jnp.exp(jnp.sum(dl[0:1] * dl[1:2], axis=1, keepdims=True))
            - jnp.exp(jnp.sum(dl[2:3] * dl[3:4], axis=1, keepdims=True)) + lambda_init)


def _diff_cached_kernel(q_ref, kn_ref, vn_ref, bprev_ref, bdiag_ref, lam_ref, g_ref,
                        kp_ref, vp_ref, o_ref, acc0, acc1, m0, m1, l0, l1,
                        *, tkp, n_past, lambda_init):
    lo = lax.broadcasted_iota(I32, (1, LANES), 1) < HEAD_DIM
    q = q_ref[...]
    zero = jnp.zeros_like(q)
    q_maps = (jnp.where(lo, q, zero), jnp.where(lo, zero, q))
    accs, ms, ls = (acc0, acc1), (m0, m1), (l0, l1)
    for acc, m, l in zip(accs, ms, ls):
        acc[...] = jnp.zeros_like(acc)
        l[...] = jnp.zeros_like(l)
        m[...] = jnp.full_like(m, NEG_INF)

    def run(chunks):
        @_memo
        def operands(n):
            return chunks[n][0].astype(BF16), chunks[n][1].astype(BF16)

        def scores(unit):
            n, mp = unit
            return _nt_dot(q_maps[mp], operands(n)[0])

        def consume(unit, s):
            n, mp = unit
            if chunks[n][2] is not None:
                s = jnp.maximum(s, NEG_INF) + chunks[n][2]
            m_old = ms[mp][...]
            m_new = jnp.maximum(m_old, jnp.max(s, axis=1, keepdims=True))
            p = jnp.exp2(s - m_new)
            alpha = jnp.exp2(m_old - m_new)
            ls[mp][...] = alpha * ls[mp][...] + jnp.sum(p, axis=1, keepdims=True)
            accs[mp][...] = alpha * accs[mp][...] + jnp.dot(
                p.astype(BF16), operands(n)[1], preferred_element_type=F32)
            ms[mp][...] = m_new

        _pipelined([(n, mp) for n in range(len(chunks)) for mp in range(2)], scores, consume)

    def cached(c, bias=None):
        off = pl.multiple_of(c * tkp, tkp)
        return kp_ref[pl.ds(off, tkp), :], vp_ref[pl.ds(off, tkp), :], bias

    def past_body(j, carry):
        run([cached(j * CACHED_CHUNK_GROUP + g) for g in range(CACHED_CHUNK_GROUP)])
        return carry
    n_far = n_past - 1
    lax.fori_loop(0, n_far // CACHED_CHUNK_GROUP, past_body, 0)
    run([cached(c) for c in range(n_far - n_far % CACHED_CHUNK_GROUP, n_far)]
        + [cached(n_far, bprev_ref[...]), (kn_ref[...], vn_ref[...], bdiag_ref[...])])

    lam = _diff_lambda(lam_ref[...], lambda_init)
    o = acc0[...] / l0[...] - lam * (acc1[...] / l1[...])
    o = o * _rms_scale(o, SUBLN_EPS) * g_ref[...] * (1.0 - lambda_init)
    o_ref[...] = o.astype(o_ref.dtype)


def _diff_cached_attention(q, k_new, v_new, bias_prev, bias_diag, diff_lambda, subln_g,
                           k_past, v_past, *, tkp, lambda_init):
    b, t, w = q.shape
    plen = k_past.shape[1]
    n_past = plen // tkp
    assert n_past * tkp == plen and tkp >= MAX_DISTANCE and t <= CHUNK and plen % CHUNK == 0
    seqspec = lambda n: pl.BlockSpec((None, n, LANES), lambda bi, h: (bi, 0, h))
    head_tile = lambda arr: pl.BlockSpec((None,) + arr.shape[1:], lambda bi, h: (h, 0, 0))
    const2 = lambda arr: pl.BlockSpec(arr.shape, lambda bi, h: (0, 0))
    return pl.pallas_call(
        functools.partial(_diff_cached_kernel, tkp=tkp, n_past=n_past, lambda_init=lambda_init),
        out_shape=jax.ShapeDtypeStruct((b, t, w), BF16),
        grid=(b, w // LANES),
        in_specs=[seqspec(t), seqspec(t), seqspec(t), head_tile(bias_prev), head_tile(bias_diag),
                  const2(diff_lambda), const2(subln_g), seqspec(plen), seqspec(plen)],
        out_specs=seqspec(t),
        scratch_shapes=[pltpu.VMEM((t, LANES), F32)] * 2 + [pltpu.VMEM((t, 1), F32)] * 4,
        compiler_params=_cparams(2),
        name="diff_cached_attn",
    )(q, k_new, v_new, bias_prev, bias_diag, diff_lambda, subln_g, k_past, v_past)


def _fox_prompt_kernel(q_ref, k_ref, kx_ref, vt_ref, o_ref, acc_a, acc_b, m_a, m_b, *, tq, rs):
    i = pl.program_id(2)
    n_qs = tq // rs
    lane = lax.broadcasted_iota(I32, (1, LANES), 1)
    lo = lane < HEAD_DIM
    top = lax.broadcasted_iota(I32, (LANES, rs), 0) < HEAD_DIM

    q = q_ref[...].astype(F32)
    ones_a = jnp.where(lane < HEAD_DIM + N_DECAY_TERMS, 1.0, 0.0)
    ones_b = jnp.where(lane < N_DECAY_TERMS, 1.0, 0.0)
    q_heads = (jnp.where(lo, q, ones_a).astype(BF16), jnp.where(lo, ones_b, q).astype(BF16))
    accs = (acc_a, acc_b)
    ms = (m_a, m_b)
    for acc, m in zip(accs, ms):
        acc[...] = jnp.zeros_like(acc)
        m[...] = jnp.full_like(m, NEG_INF)

    def chunk_group(first, units):
        @_memo
        def operands(kc):
            off = pl.multiple_of((first + kc) * rs, rs)
            k = k_ref[pl.ds(off, rs), :]
            kx = kx_ref[pl.ds(off, rs), :]
            vt = vt_ref[:, pl.ds(off, rs)]
            ones_v = jnp.ones_like(vt)
            return ((jnp.where(lo, k, kx), jnp.where(lo, kx, k)),
                    (jnp.where(top, vt, ones_v), jnp.where(top, ones_v, vt)))

        def scores(unit):
            kc, hd, qs, _ = unit
            return _nt_dot(operands(kc)[0][hd], q_heads[hd][qs * rs:(qs + 1) * rs])

        def consume(unit, s):
            kc, hd, qs, mask = unit
            cols = pl.ds(qs * rs, rs)
            if mask is not None:
                s = jnp.where(mask, s, NEG_INF)
            m_old = ms[hd][:, cols]
            m_new = jnp.maximum(m_old, jnp.max(s, axis=0, keepdims=True))
            p = jnp.exp2(s - m_new).astype(BF16)
            accs[hd][:, cols] = jnp.exp2(m_old - m_new) * accs[hd][:, cols] + jnp.dot(
                operands(kc)[1][hd], p, preferred_element_type=F32)
            ms[hd][:, cols] = m_new

        _pipelined(units, scores, consume)

    def below(j, carry):
        chunk_group(n_qs * j, [(kc, hd, qs, None) for kc in range(n_qs) for qs in range(n_qs)
                               for hd in range(2)])
        return carry
    lax.fori_loop(0, i, below, 0)
    causal = (lax.broadcasted_iota(I32, (rs, rs), 0) <= lax.broadcasted_iota(I32, (rs, rs), 1))
    chunk_group(n_qs * i, [(kc, hd, qs, causal if qs == kc else None) for kc in range(n_qs)
                           for qs in range(kc, n_qs) for hd in range(2)])

    a = acc_a[...]
    b = acc_b[...]
    out_t = jnp.concatenate([a[:HEAD_DIM] / a[HEAD_DIM:HEAD_DIM + 1], b[HEAD_DIM:] / b[0:1]],
                            axis=0)
    o_ref[...] = out_t.T.astype(o_ref.dtype)


def _fox_prompt_attention(q, k, kx, vt, *, tq, rs):
    b, t, w = q.shape
    qspec = pl.BlockSpec((None, tq, LANES), lambda bi, p, i: (bi, i, p))
    kspec = pl.BlockSpec((None, t, LANES), lambda bi, p, i: (bi, 0, p))
    return pl.pallas_call(
        functools.partial(_fox_prompt_kernel, tq=tq, rs=rs),
        out_shape=jax.ShapeDtypeStruct((b, t, w), BF16),
        grid=(b, w // LANES, t // tq),
        in_specs=[qspec, kspec, kspec, pl.BlockSpec((LANES, t), lambda bi, p, i: (p, bi))],
        out_specs=qspec,
        scratch_shapes=[pltpu.VMEM((LANES, tq), F32), pltpu.VMEM((LANES, tq), F32),
                        pltpu.VMEM((1, tq), F32), pltpu.VMEM((1, tq), F32)],
        compiler_params=_cparams(3),
        name="fox_prompt_attn",
    )(q, k, kx, vt)


ONES_ROWS = 16


def _diff_prompt_kernel(q_ref, k_ref, vt_ref, bprev_ref, bdiag_ref, lam_ref, g_ref, o_ref,
                        acc0, acc1, m0, m1, *, tq, rs, lambda_init):
    i = pl.program_id(2)
    n_qs = tq // rs
    lo = lax.broadcasted_iota(I32, (1, LANES), 1) < HEAD_DIM

    q = q_ref[...]
    zero = jnp.zeros_like(q)
    q_maps = (jnp.where(lo, q, zero), jnp.where(lo, zero, q))
    accs, ms = (acc0, acc1), (m0, m1)
    for acc, m in zip(accs, ms):
        acc[...] = jnp.zeros_like(acc)
        m[...] = jnp.full_like(m, NEG_INF)

    def chunk_group(first, blocks_ahead):
        @_memo
        def operands(kc):
            off = pl.multiple_of((first + kc) * rs, rs)
            return (k_ref[pl.ds(off, rs), :],
                    jnp.concatenate([vt_ref[:, pl.ds(off, rs)], jnp.ones((ONES_ROWS, rs), BF16)],
                                    axis=0))

        def scores(unit):
            kc, mp, qs = unit
            return _nt_dot(operands(kc)[0], q_maps[mp][qs * rs:(qs + 1) * rs])

        def consume(unit, s):
            kc, mp, qs = unit
            cols = pl.ds(qs * rs, rs)
            distance = blocks_ahead + qs - kc
            if distance == 1:
                s = jnp.maximum(s, NEG_INF) + bprev_ref[...]
            elif distance == 0:
                s = jnp.maximum(s, NEG_INF) + bdiag_ref[...]
            m_old = ms[mp][:, cols]
            m_new = jnp.maximum(m_old, jnp.max(s, axis=0, keepdims=True))
            p = jnp.exp2(s - m_new).astype(BF16)
            accs[mp][:, cols] = jnp.exp2(m_old - m_new) * accs[mp][:, cols] + jnp.dot(
                operands(kc)[1], p, preferred_element_type=F32)
            ms[mp][:, cols] = m_new

        _pipelined([(kc, mp, qs) for kc in range(n_qs) for qs in range(n_qs) for mp in range(2)
                    if blocks_ahead + qs - kc >= 0], scores, consume)

    def far_below(j, carry):
        chunk_group(n_qs * j, 2 * n_qs)
        return carry
    lax.fori_loop(0, jnp.maximum(i - 1, 0), far_below, 0)

    @pl.when(i >= 1)
    def _():
        chunk_group(n_qs * (i - 1), n_qs)

    chunk_group(n_qs * i, 0)

    lam = _diff_lambda(lam_ref[...], lambda_init)
    a0 = acc0[...]
    a1 = acc1[...]
    o = a0[:LANES] / a0[LANES:LANES + 1] - lam * (a1[:LANES] / a1[LANES:LANES + 1])
    scale = lax.rsqrt(jnp.mean(o * o, axis=0, keepdims=True) + SUBLN_EPS)
    o = o * scale * g_ref[...] * (1.0 - lambda_init)
    o_ref[...] = o.T.astype(o_ref.dtype)


def _diff_prompt_attention(q, k, vt, bias_prev, bias_diag, diff_lambda, subln_g, *,
                           tq, rs, lambda_init):
    b, t, w = q.shape
    assert rs >= MAX_DISTANCE
    qspec = pl.BlockSpec((None, tq, LANES), lambda bi, h, i: (bi, i, h))
    head_tile = lambda arr: pl.BlockSpec((None,) + arr.shape[1:], lambda bi, h, i: (h, 0, 0))
    const2 = lambda arr: pl.BlockSpec(arr.shape, lambda bi, h, i: (0, 0))
    return pl.pallas_call(
        functools.partial(_diff_prompt_kernel, tq=tq, rs=rs, lambda_init=lambda_init),
        out_shape=jax.ShapeDtypeStruct((b, t, w), BF16),
        grid=(b, w // LANES, t // tq),
        in_specs=[qspec, pl.BlockSpec((None, t, LANES), lambda bi, h, i: (bi, 0, h)),
                  pl.BlockSpec((LANES, t), lambda bi, h, i: (h, bi)),
                  head_tile(bias_prev), head_tile(bias_diag), const2(diff_lambda),
                  const2(subln_g)],
        out_specs=qspec,
        scratch_shapes=[pltpu.VMEM((LANES + ONES_ROWS, tq), F32)] * 2
        + [pltpu.VMEM((1, tq), F32)] * 2,
        compiler_params=_cparams(3),
        name="diff_prompt_attn",
    )(q, k, vt, bias_prev, bias_diag, diff_lambda, subln_g)


def _t5_bucket(rel):
    nb = NUM_BUCKETS // 2
    max_exact = nb // 2
    ret = jnp.where(rel > 0, nb, 0)
    n = jnp.abs(rel)
    nf = jnp.maximum(n, max_exact).astype(F32)
    large = max_exact + (jnp.log(nf / max_exact) / math.log(MAX_DISTANCE / max_exact)
                         * (nb - max_exact)).astype(I32)
    large = jnp.minimum(large, nb - 1)
    return ret + jnp.where(n < max_exact, n, large)


def _bias_tiles(rel_table, tq, tk_prev):
    buckets = jnp.arange(NUM_BUCKETS, dtype=I32)

    def lookup(rel):
        hit = _t5_bucket(rel)[None, :] == buckets[:, None]
        return jnp.sum(jnp.where(hit[:, None, :], rel_table[:, :, None], 0.0), axis=0)

    far = lookup(jnp.full((1,), -4 * MAX_DISTANCE, I32))

    def tile(rel0, tk):
        period = tq + tk
        j = jnp.arange(period, dtype=I32)
        strip = lookup(rel0 + jnp.where(j < tk, j, j - period)) - far
        flat = jnp.tile(strip, (1, tq))[:, :tq * (period - 1)]
        return (flat.reshape(-1, tq, period - 1)[:, :, :tk] * LOG2E).astype(F32)

    return tile(-tk_prev, tk_prev), tile(0, tq)


def _router_kernel(x_ref, of_ref, od_ref, wof_ref, wod_ref, g_ref, wr_ref, br_ref,
                   x1_ref, hn_ref, te_ref, tw_ref, rank_ref, cnt_ref, *, tm):
    step = pl.program_id(0)

    @pl.when(step == 0)
    def _():
        cnt_ref[...] = jnp.zeros_like(cnt_ref)

    x1 = (x_ref[...] + jnp.dot(of_ref[...], wof_ref[...], preferred_element_type=F32)
          + jnp.dot(od_ref[...], wod_ref[...], preferred_element_type=F32))
    x1_ref[...] = x1
    hn = x1 * _rms_scale(x1, NORM_EPS) * g_ref[...]
    half = hn.shape[1] // 2
    _store_token_slabs(hn_ref, _pack_bf16_pairs(hn[:, :half], hn[:, half:]))

    hn_hi = hn.astype(BF16)
    hn_lo = (hn - hn_hi.astype(F32)).astype(BF16)
    both = jnp.dot(hn_hi, wr_ref[...], preferred_element_type=F32)
    logits = (both[:, :LANES] + both[:, LANES:] + br_ref[...]
              + jnp.dot(hn_lo, wr_ref[:, :LANES], preferred_element_type=F32))
    lane = lax.broadcasted_iota(I32, logits.shape, 1)
    lane_f = lane.astype(F32)
    sel = jnp.zeros(logits.shape, F32)
    vals, idxs = [], []
    for _ in range(TOP_K):
        mx = jnp.max(logits, axis=1, keepdims=True)
        idx = jnp.min(jnp.where(logits == mx, lane_f, float(LANES)), axis=1, keepdims=True)
        hit = lane_f == idx
        sel = jnp.where(hit, 1.0, sel)
        logits = jnp.where(hit, -jnp.inf, logits)
        vals.append(mx)
        idxs.append(idx)
    exps = [jnp.exp(v - vals[0]) for v in vals]
    denom = exps[0] + exps[1] + exps[2] + exps[3]

    earlier = (lax.broadcasted_iota(I32, (tm, tm), 1)
               < lax.broadcasted_iota(I32, (tm, tm), 0)).astype(BF16)
    before = cnt_ref[...] + jnp.dot(earlier, sel.astype(BF16), preferred_element_type=F32)
    for k in range(TOP_K):
        te_ref[:, k:k + 1] = idxs[k].astype(I32)
        tw_ref[:, k:k + 1] = exps[k] / denom
        rank_ref[:, k:k + 1] = jnp.sum(
            jnp.where(lane_f == idxs[k], before, 0.0), axis=1, keepdims=True).astype(I32)
    cnt_ref[...] += jnp.sum(sel, axis=0, keepdims=True)


def _router(x2d, o_fox, o_diff, wo_f, wo_d, g_ffn, w_r, b_r, *, tm):
    n, d = x2d.shape
    row = lambda i: (i, 0)
    const = lambda i: (0, 0)
    full = lambda arr: pl.BlockSpec(arr.shape, const)
    parts = d // 2 // LANES
    return pl.pallas_call(
        functools.partial(_router_kernel, tm=tm),
        out_shape=[jax.ShapeDtypeStruct((n, d), F32), jax.ShapeDtypeStruct((n * parts, LANES), U32),
                   jax.ShapeDtypeStruct((n, TOP_K), I32), jax.ShapeDtypeStruct((n, TOP_K), F32),
                   jax.ShapeDtypeStruct((n, TOP_K), I32), jax.ShapeDtypeStruct((1, LANES), F32)],
        grid=(n // tm,),
        in_specs=[pl.BlockSpec((tm, d), row), pl.BlockSpec((tm, o_fox.shape[1]), row),
                  pl.BlockSpec((tm, o_diff.shape[1]), row), full(wo_f), full(wo_d),
                  full(g_ffn), full(w_r), full(b_r)],
        out_specs=[pl.BlockSpec((tm, d), row), pl.BlockSpec((tm * parts, LANES), row),
                   pl.BlockSpec((tm, TOP_K), row), pl.BlockSpec((tm, TOP_K), row),
                   pl.BlockSpec((tm, TOP_K), row), pl.BlockSpec((1, LANES), const)],
        compiler_params=_cparams(1),
        name="router",
    )(x2d, o_fox, o_diff, wo_f, wo_d, g_ffn, w_r, b_r)


def _row_copy(src, src_row, dst, dst_row, sem, parts):
    return pltpu.make_async_copy(src.at[pl.ds(src_row * parts, parts), :],
                                 dst.at[pl.ds(dst_row * parts, parts), :], sem)


def _dispatch_kernel(tail_start_ref, tail_valid_ref, pos_ref, hn_ref, xs_hbm, zeros_ref, sem,
                     *, tm, tme, n_experts):
    parts = hn_ref.shape[0] // tm
    slab = tme * parts

    def tail_copy(e):
        return pltpu.make_async_copy(
            zeros_ref, xs_hbm.at[pl.ds(pl.multiple_of(tail_start_ref[e] * parts, slab), slab), :],
            sem.at[0])

    @pl.when(pl.program_id(0) == 0)
    def _():
        zeros_ref[...] = jnp.zeros_like(zeros_ref)
        for e in range(n_experts):
            @pl.when(tail_valid_ref[e] > 0)
            def _():
                tail_copy(e).start()
        for e in range(n_experts):
            @pl.when(tail_valid_ref[e] > 0)
            def _():
                tail_copy(e).wait()

    def start_row(r, carry):
        for k in range(TOP_K):
            _row_copy(hn_ref, r, xs_hbm, pos_ref[r, k], sem.at[1], parts).start(priority=k % 2)
        return carry

    def wait_row(r, carry):
        for k in range(TOP_K):
            _row_copy(hn_ref, r, xs_hbm, pos_ref[r, k], sem.at[1], parts).wait()
        return carry

    lax.fori_loop(0, tm, start_row, 0, unroll=True)
    lax.fori_loop(0, tm, wait_row, 0, unroll=8)


def _dispatch(hn_slabs, pos, tail_start, tail_valid, *, m_pad, tm, tme):
    n = pos.shape[0]
    parts = hn_slabs.shape[0] // n
    n_experts = tail_start.shape[0]
    grid_spec = pltpu.PrefetchScalarGridSpec(
        num_scalar_prefetch=2,
        grid=(n // tm,),
        in_specs=[pl.BlockSpec((tm, TOP_K), lambda i, *_: (i, 0), memory_space=pltpu.SMEM),
                  pl.BlockSpec((tm * parts, LANES), lambda i, *_: (i, 0))],
        out_specs=pl.BlockSpec(memory_space=pl.ANY),
        scratch_shapes=[pltpu.VMEM((tme * parts, LANES), U32), pltpu.SemaphoreType.DMA((2,))],
    )
    return pl.pallas_call(
        functools.partial(_dispatch_kernel, tm=tm, tme=tme, n_experts=n_experts),
        out_shape=jax.ShapeDtypeStruct((m_pad * parts, LANES), U32),
        grid_spec=grid_spec,
        compiler_params=pltpu.CompilerParams(
            dimension_semantics=("arbitrary",), vmem_limit_bytes=VMEM_LIMIT_BYTES,
            has_side_effects=True),
        name="dispatch",
    )(tail_start, tail_valid, pos, hn_slabs)


def _experts_kernel(te_ref, nact_ref, xs_ref, wg_ref, bg_ref, wu_ref, bu_ref, wd_ref, bd_ref,
                    ys_ref, *, tme):
    @pl.when(pl.program_id(0) < nact_ref[0])
    def _():
        parts = xs_ref.shape[0] // tme
        half = parts * LANES
        x_lo, x_hi = _unpack_bf16_pairs(_load_token_slabs(xs_ref, tme, parts))
        x_lo = x_lo.astype(BF16)
        x_hi = x_hi.astype(BF16)

        def proj(w_ref, b_ref):
            return (jnp.dot(x_lo, w_ref[:half, :], preferred_element_type=F32)
                    + jnp.dot(x_hi, w_ref[half:, :], preferred_element_type=F32) + b_ref[...])

        g = jnp.minimum(proj(wg_ref, bg_ref), SWIGLU_LIMIT)
        u = jnp.clip(proj(wu_ref, bu_ref), -SWIGLU_LIMIT, SWIGLU_LIMIT)
        a = ((u + 1.0) * (g * jax.nn.sigmoid(SWIGLU_ALPHA * g))).astype(BF16)
        y = jnp.dot(a, wd_ref[...], preferred_element_type=F32) + bd_ref[...]
        _store_token_slabs(ys_ref, _pack_bf16_pairs(y[:, :half], y[:, half:]))


def _experts(xs, tile_expert, n_active, w_gate, b_gate, w_up, b_up, w_down, b_down, *, tme):
    d, d_ff = w_gate.shape[1:]
    parts = d // 2 // LANES
    n_tiles = xs.shape[0] // (tme * parts)
    row = lambda t, te, na: (jnp.minimum(t, na[0] - 1), 0)
    wspec = lambda shape: pl.BlockSpec((None,) + shape, lambda t, te, na: (te[t], 0, 0))
    grid_spec = pltpu.PrefetchScalarGridSpec(
        num_scalar_prefetch=2,
        grid=(n_tiles,),
        in_specs=[pl.BlockSpec((tme * parts, LANES), row),
                  wspec((d, d_ff)), wspec((1, d_ff)), wspec((d, d_ff)), wspec((1, d_ff)),
                  wspec((d_ff, d)), wspec((1, d))],
        out_specs=pl.BlockSpec((tme * parts, LANES), row),
    )
    return pl.pallas_call(
        functools.partial(_experts_kernel, tme=tme),
        out_shape=jax.ShapeDtypeStruct(xs.shape, U32),
        grid_spec=grid_spec,
        compiler_params=_cparams(1),
        name="experts",
    )(tile_expert, n_active, xs, w_gate, b_gate, w_up, b_up, w_down, b_down)


def _combine_kernel(pos_ref, pos_next_ref, x1_ref, tw_ref, p_ref, wg_ref, wp_ref, gple_ref,
                    gfin_ref, ys_hbm, y_ref, gbuf, sem, *, tm):
    step = pl.program_id(0)
    slot = step % 2
    parts = gbuf.shape[2] // tm

    def gather(tile_pos_ref, tile_slot, wait):
        def row(r, carry):
            for k in range(TOP_K):
                copy = _row_copy(ys_hbm, tile_pos_ref[r, k], gbuf.at[tile_slot, k], r,
                                 sem.at[tile_slot], parts)
                if wait:
                    copy.wait()
                else:
                    copy.start(priority=k % 2)
            return carry
        lax.fori_loop(0, tm, row, 0, unroll=8)

    @pl.when(step == 0)
    def _():
        gather(pos_ref, 0, wait=False)

    @pl.when(step + 1 < pl.num_programs(0))
    def _():
        gather(pos_next_ref, 1 - slot, wait=False)

    gather(pos_ref, slot, wait=True)

    x1 = x1_ref[...]
    half = x1.shape[1] // 2
    tw = tw_ref[...]
    moe_lo = jnp.zeros((tm, half), F32)
    moe_hi = jnp.zeros((tm, half), F32)
    for k in range(TOP_K):
        lo, hi = _unpack_bf16_pairs(_load_token_slabs(gbuf.at[slot, k], tm, parts))
        moe_lo = moe_lo + tw[:, k:k + 1] * lo
        moe_hi = moe_hi + tw[:, k:k + 1] * hi
    x2 = x1 + jnp.concatenate([moe_lo, moe_hi], axis=1)
    hp = (x2 * _rms_scale(x2, NORM_EPS) * gple_ref[...]).astype(BF16)
    gate = jax.nn.sigmoid(jnp.dot(hp, wg_ref[...], preferred_element_type=F32))
    proj = jnp.dot(p_ref[...].astype(BF16), wp_ref[...], preferred_element_type=F32)
    x3 = x2 + proj * gate
    y_ref[...] = x3 * _rms_scale(x3, NORM_EPS) * gfin_ref[...]


def _combine(x1, top_w, pos, ys, p2d, w_ple_gate, w_ple_proj, g_ple, g_final, *, tm):
    n, d = x1.shape
    parts = d // 2 // LANES
    row = lambda i: (i, 0)
    const = lambda i: (0, 0)
    full = lambda arr: pl.BlockSpec(arr.shape, const)
    return pl.pallas_call(
        functools.partial(_combine_kernel, tm=tm),
        out_shape=jax.ShapeDtypeStruct((n, d), F32),
        grid=(n // tm,),
        in_specs=[pl.BlockSpec((tm, TOP_K), row, memory_space=pltpu.SMEM),
                  pl.BlockSpec((tm, TOP_K), lambda i: (jnp.minimum(i + 1, n // tm - 1), 0),
                               memory_space=pltpu.SMEM),
                  pl.BlockSpec((tm, d), row), pl.BlockSpec((tm, TOP_K), row),
                  pl.BlockSpec((tm, p2d.shape[1]), row), full(w_ple_gate), full(w_ple_proj),
                  full(g_ple), full(g_final), pl.BlockSpec(memory_space=pl.ANY)],
        out_specs=pl.BlockSpec((tm, d), row),
        scratch_shapes=[pltpu.VMEM((2, TOP_K, tm * parts, LANES), U32),
                        pltpu.SemaphoreType.DMA((2,))],
        compiler_params=_cparams(1),
        name="combine",
    )(pos, pos, x1, top_w, p2d, w_ple_gate, w_ple_proj, g_ple, g_final, ys)


def _pick_tile(n, want):
    t = min(n, want)
    assert n % t == 0, (n, t)
    return t


def _layer_group(x, p, past, wts, lambda_init, g_final):
    (g_mix, w_main, w_vt, w_f, b_f, diff_lambda, subln_g, rel_table, wo_f, wo_d, g_ffn, w_r, b_r,
     w_gate, b_gate, w_up, b_up, w_down, b_down, g_ple, w_ple_gate, w_ple_proj) = wts
    b, t, d = x.shape
    n = b * t
    n_fox = (w_main.shape[1] // 6) // HEAD_DIM
    n_experts = w_gate.shape[0]
    x2d = x.reshape(n, d)

    tm = _pick_tile(n, 512)
    (qf, qd, kf, vf, kd, vd, kf16, kd16, vf16, vd16, logf_lanes) = _inproj(
        x2d, g_mix, w_main, w_vt, w_f, b_f, tm=tm, values_transposed=past is None)
    logf = logf_lanes[:, :n_fox]
    width = qf.shape[1]
    as_seq = lambda a: a.reshape(b, t, a.shape[-1])

    tq = _pick_tile(t, 2048)
    rs = _pick_tile(tq, 256)
    if past is None:
        kx = _key_decay(as_seq(logf_lanes), n_fox, width)
        o_fox = _fox_prompt_attention(as_seq(qf), as_seq(kf16), kx, vf16, tq=tq, rs=rs)
        bias_prev, bias_diag = _bias_tiles(rel_table, rs, rs)
        block = jnp.arange(rs, dtype=I32) // CHUNK
        bias_diag = jnp.where(block[:, None] <= block[None, :], jnp.swapaxes(bias_diag, 1, 2),
                              NEG_INF)
        o_diff = _diff_prompt_attention(
            as_seq(qd), as_seq(kd16), vd16, jnp.swapaxes(bias_prev, 1, 2), bias_diag,
            diff_lambda, subln_g.reshape(-1, 1), tq=tq, rs=rs, lambda_init=lambda_init)
    else:
        pk_f, pv_f, plogf, pk_d, pv_d = past
        plen = pk_f.shape[1]
        tkp = 512
        total = plen + t
        padded = -(-total // LANES) * LANES
        seq = jnp.concatenate([jnp.transpose(plogf.astype(F32), (0, 2, 1)),
                               jnp.transpose(as_seq(logf), (0, 2, 1))], axis=2)
        c_all = _cumsum_lanes(jnp.pad(seq, ((0, 0), (0, 0), (0, padded - total))))
        flat = lambda a: a.reshape(b, plen, width)
        o_fox = _fox_cached_attention(as_seq(qf), as_seq(kf16), as_seq(vf16),
                                      c_all[:, :, plen:total], flat(pk_f), flat(pv_f),
                                      c_all[:, :, :plen], tkp=tkp)
        bias_prev, bias_diag = _bias_tiles(rel_table, t, tkp)
        o_diff = _diff_cached_attention(as_seq(qd), as_seq(kd16), as_seq(vd16), bias_prev,
                                        bias_diag, diff_lambda, subln_g, flat(pk_d), flat(pv_d),
                                        tkp=tkp, lambda_init=lambda_init)

    tmr = _pick_tile(n, 256)
    x1, hn_packed, top_e, top_w, rank, counts = _router(
        x2d, o_fox.reshape(n, width), o_diff.reshape(n, width), wo_f, wo_d, g_ffn, w_r, b_r, tm=tmr)

    tme = 512 if n * TOP_K >= 64 * 512 else 256
    n_tiles = -(-(n * TOP_K) // tme) + n_experts
    m_pad = n_tiles * tme
    cnt = counts[0, :n_experts].astype(I32)
    group = -(-cnt // tme) * tme
    ends = jnp.cumsum(group)
    starts = ends - group
    experts = jnp.arange(n_experts, dtype=I32)
    pos = rank + jnp.sum(jnp.where(top_e[:, :, None] == experts, starts, 0), axis=2)
    n_active = (ends[-1] // tme).astype(I32)
    tile_ids = jnp.arange(n_tiles, dtype=I32)
    tile_expert = jnp.sum(jnp.minimum(tile_ids, n_active - 1)[:, None] >= (ends // tme)[None, :],
                          axis=1).astype(I32)
    tile_expert = jnp.minimum(tile_expert, n_experts - 1)
    tail_start = jnp.maximum(ends - tme, 0).astype(I32)
    tail_valid = (group > 0).astype(I32)

    tmd = _pick_tile(n, 256)
    xs = _dispatch(hn_packed, pos, tail_start, tail_valid, m_pad=m_pad, tm=tmd, tme=tme)
    ys = _experts(xs, tile_expert, n_active.reshape(1), w_gate, b_gate, w_up, b_up, w_down, b_down,
                  tme=tme)
    y = _combine(x1, top_w, pos, ys, p.reshape(n, p.shape[-1]), w_ple_gate, w_ple_proj, g_ple,
                 g_final, tm=tmd)
    return y.reshape(b, t, d), (kf, vf, logf, kd, vd)


def kernel(x_prompt, x_sample, p_prompt, p_sample, cache_fox_k, cache_fox_v, cache_fox_logf, cache_diff_k, cache_diff_v, g_mix, w_in, b_forget, diff_lambda, diff_subln_g, rel_bias_table, w_o, g_ffn, w_router, b_router, w_gate, b_gate, w_up, b_up, w_down, b_down, g_ple, w_ple_gate, w_ple_proj, g_final):
    depth = g_mix.shape[0]
    assert depth == 1, "the final norm is fused into the last layer; one layer supported"
    d = x_prompt.shape[-1]
    n_fox = cache_fox_k.shape[3]
    fox_w = n_fox * cache_fox_k.shape[4]
    n_diff = cache_diff_k.shape[3]
    diff_w = n_diff * 2 * cache_diff_k.shape[5]
    assert fox_w == diff_w and fox_w % LANES == 0
    n_experts = w_router.shape[-1]
    assert TOP_K <= n_experts <= LANES
    row2 = lambda a: a.reshape(1, -1).astype(F32)

    i = 0
    lambda_init = 0.8 - 0.6 * math.exp(-0.3 * i)
    off_fk, off_fv, off_ff = fox_w, 2 * fox_w, 3 * fox_w
    off_dq = off_ff + n_fox
    off_dk, off_dv = off_dq + diff_w, off_dq + 2 * diff_w
    w = w_in[i]
    cols = lambda o, wd: w[:, o:o + wd]
    w_main = jnp.concatenate([cols(0, fox_w), cols(off_dq, diff_w), cols(off_fk, fox_w),
                              cols(off_fv, fox_w), cols(off_dk, diff_w), cols(off_dv, diff_w)],
                             axis=1).astype(BF16)
    w_vt = jnp.concatenate([cols(off_fv, fox_w), cols(off_dv, diff_w)], axis=1).T.astype(BF16)
    w_f = jnp.pad(cols(off_ff, n_fox), ((0, 0), (0, LANES - n_fox))).astype(BF16)
    b_f = jnp.pad(row2(b_forget[i]), ((0, 0), (0, LANES - n_fox)))
    w_r = jnp.pad(w_router[i].astype(F32), ((0, 0), (0, LANES - n_experts)))
    w_r_hi = w_r.astype(BF16)
    w_r = jnp.concatenate([w_r_hi, (w_r - w_r_hi.astype(F32)).astype(BF16)], axis=1)
    b_r = jnp.pad(row2(b_router[i]), ((0, 0), (0, LANES - n_experts)), constant_values=NEG_INF)
    wo = w_o[i].astype(BF16)
    bias3 = lambda a: a.reshape(n_experts, 1, -1).astype(F32)
    wts = (row2(g_mix[i]), w_main, w_vt, w_f, b_f, diff_lambda[i].astype(F32), row2(diff_subln_g[i]),
           rel_bias_table.astype(F32), wo[:fox_w], wo[fox_w:], row2(g_ffn[i]), w_r, b_r,
           w_gate[i].astype(BF16), bias3(b_gate[i]), w_up[i].astype(BF16), bias3(b_up[i]),
           w_down[i].astype(BF16), bias3(b_down[i]), row2(g_ple[i]),
           w_ple_gate[i].astype(BF16), w_ple_proj[i].astype(BF16))
    gf = row2(g_final)

    yp, new_p = _layer_group(x_prompt, p_prompt[i], None, wts, lambda_init, gf)
    past_s = (cache_fox_k[i], cache_fox_v[i], cache_fox_logf[i], cache_diff_k[i], cache_diff_v[i])
    ys, new_s = _layer_group(x_sample, p_sample[i], past_s, wts, lambda_init, gf)

    def rows(new, b, t):
        kf, vf, logf, kd, vd = new
        return (kf.reshape(1, b, t, n_fox, -1), vf.reshape(1, b, t, n_fox, -1),
                logf.reshape(1, b, t, n_fox), kd.reshape(1, b, t, n_diff, 2, -1),
                vd.reshape(1, b, t, n_diff, -1))

    bp, tp = x_prompt.shape[:2]
    bs, ts = x_sample.shape[:2]
    return (yp, ys) + rows(new_p, bp, tp) + rows(new_s, bs, ts)
```

```python
import functools
import math

import jax
import jax.numpy as jnp
import numpy as np
from jax import lax
from jax.experimental import pallas as pl
from jax.experimental.pallas import tpu as pltpu

HEAD_DIM = 64
CHUNK = 64
NUM_BUCKETS = 32
MAX_DISTANCE = 128
TOP_K = 4
SWIGLU_LIMIT = 7.0
SWIGLU_ALPHA = 1.702
NORM_EPS = 1e-6
SUBLN_EPS = 1e-5
NEG_INF = -1e30
LOG2E = math.log2(math.e)

LANES = 128
VMEM_LIMIT_BYTES = 56 * 1024 * 1024

F32 = jnp.float32
BF16 = jnp.bfloat16
I32 = jnp.int32
U32 = jnp.uint32


def _cparams(n_axes):
    return pltpu.CompilerParams(
        dimension_semantics=("arbitrary",) * n_axes, vmem_limit_bytes=VMEM_LIMIT_BYTES)


def _rms_scale(x, eps):
    return lax.rsqrt(jnp.mean(x * x, axis=-1, keepdims=True) + eps)


def _pack_bf16_pairs(lo, hi):
    lo_bits = pltpu.bitcast(lo.astype(BF16).astype(F32), U32)
    hi_bits = pltpu.bitcast(hi.astype(BF16).astype(F32), U32)
    return lo_bits | (hi_bits >> 16)


def _unpack_bf16_pairs(packed):
    lo = pltpu.bitcast(packed & jnp.uint32(0xFFFF0000), F32)
    hi = pltpu.bitcast(packed << 16, F32)
    return lo, hi


def _store_token_slabs(ref, rows):
    n, width = rows.shape
    parts = width // LANES
    for j in range(parts):
        ref[pl.ds(j, n, stride=parts), :] = rows[:, j * LANES:(j + 1) * LANES]


def _load_token_slabs(ref, n, parts):
    return jnp.concatenate([ref[pl.ds(j, n, stride=parts), :] for j in range(parts)], axis=1)


def _inproj_kernel(x_ref, g_ref, w_ref, wvt_ref, wf_ref, bf_ref,
                   qf_ref, qd_ref, kf_ref, vf_ref, kd_ref, vd_ref,
                   kf16_ref, kd16_ref, vf16_ref, vd16_ref, logf_ref, *, width, values_transposed):
    x = x_ref[...]
    h = (x * _rms_scale(x, NORM_EPS) * g_ref[...]).astype(BF16)
    q_scale = LOG2E / math.sqrt(HEAD_DIM)

    def proj(c):
        return jnp.dot(h, w_ref[:, c * width:(c + 1) * width], preferred_element_type=F32)

    qf_ref[...] = (proj(0) * q_scale).astype(BF16)
    qd_ref[...] = (proj(1) * q_scale).astype(BF16)
    for c, (o32, o16) in enumerate(((kf_ref, kf16_ref), (vf_ref, vf16_ref),
                                    (kd_ref, kd16_ref), (vd_ref, vd16_ref))):
        u = proj(2 + c)
        if len(o32.shape) == 3:
            o32[...] = pltpu.einshape("m(hd)->mhd", u, h=o32.shape[1])
        else:
            o32[...] = pltpu.einshape("m(hed)->mhed", u, h=o32.shape[1], e=o32.shape[2])
        if not (values_transposed and c in (1, 3)):
            o16[...] = u.astype(BF16)
    if values_transposed:
        vf16_ref[...] = _nt_dot(wvt_ref[:width, :], h).astype(BF16)
        vd16_ref[...] = _nt_dot(wvt_ref[width:, :], h).astype(BF16)
    f = jnp.dot(h, wf_ref[...], preferred_element_type=F32) + bf_ref[...]
    logf_ref[...] = jax.nn.log_sigmoid(f)


def _inproj(x2d, g_mix, w_main, w_vt, w_f, b_f, *, tm, values_transposed):
    n, d = x2d.shape
    width = w_main.shape[1] // 6
    row = lambda i: (i, 0)
    const = lambda i: (0, 0)
    if values_transposed:
        v16_shape, v16_spec = (width, n), pl.BlockSpec((width, tm), lambda i: (0, i))
    else:
        v16_shape, v16_spec = (n, width), pl.BlockSpec((tm, width), row)
    head_shapes = [(width // HEAD_DIM, HEAD_DIM)] * 2 + [
        (width // (2 * HEAD_DIM), 2, HEAD_DIM), (width // (2 * HEAD_DIM), 2 * HEAD_DIM)]
    out_shape = ([jax.ShapeDtypeStruct((n, width), BF16)] * 2
                 + [jax.ShapeDtypeStruct((n,) + s, F32) for s in head_shapes]
                 + [jax.ShapeDtypeStruct((n, width), BF16)] * 2
                 + [jax.ShapeDtypeStruct(v16_shape, BF16)] * 2
                 + [jax.ShapeDtypeStruct((n, LANES), F32)])
    out_specs = ([pl.BlockSpec((tm, width), row)] * 2
                 + [pl.BlockSpec((tm,) + s, lambda i, s=s: (i,) + (0,) * len(s))
                    for s in head_shapes]
                 + [pl.BlockSpec((tm, width), row)] * 2 + [v16_spec] * 2
                 + [pl.BlockSpec((tm, LANES), row)])
    return pl.pallas_call(
        functools.partial(_inproj_kernel, width=width, values_transposed=values_transposed),
        out_shape=out_shape,
        grid=(n // tm,),
        in_specs=[pl.BlockSpec((tm, d), row), pl.BlockSpec((1, d), const),
                  pl.BlockSpec(w_main.shape, const), pl.BlockSpec(w_vt.shape, const),
                  pl.BlockSpec(w_f.shape, const), pl.BlockSpec(b_f.shape, const)],
        out_specs=out_specs,
        compiler_params=_cparams(1),
        name="inproj",
    )(x2d, g_mix, w_main, w_vt, w_f, b_f)


def _cumsum_kernel(x_ref, o_ref):
    x = x_ref[...]
    length = x.shape[1]
    lane = lax.broadcasted_iota(I32, x.shape, 1)
    shift = 1
    while shift < length:
        x = x + jnp.where(lane >= shift, pltpu.roll(x, shift, axis=1), 0.0)
        shift *= 2
    o_ref[...] = x


def _cumsum_lanes(x):
    b, h, length = x.shape
    spec = pl.BlockSpec((None, h, length), lambda i: (i, 0, 0))
    return pl.pallas_call(
        _cumsum_kernel, out_shape=jax.ShapeDtypeStruct(x.shape, F32), grid=(b,),
        in_specs=[spec], out_specs=spec, compiler_params=_cparams(1), name="cumsum",
    )(x)


N_DECAY_TERMS = 3


def _key_decay_kernel(x_ref, sel_ref, o_ref):
    x = x_ref[...]
    length = x.shape[0]
    row = lax.broadcasted_iota(I32, x.shape, 0)
    shift = 1
    while shift < length:
        x = x + jnp.where(row >= shift, pltpu.roll(x, shift, axis=0), 0.0)
        shift *= 2
    rest = x * (-LOG2E)
    out = jnp.zeros(o_ref.shape, F32)
    for j in range(N_DECAY_TERMS):
        term = rest.astype(BF16)
        rest = rest - term.astype(F32)
        out = out + jnp.dot(term, sel_ref[j], preferred_element_type=F32)
    o_ref[...] = out.astype(BF16)


def _decay_selectors(n_heads, width):
    sel = np.zeros((N_DECAY_TERMS, LANES, width), np.float32)
    for h in range(n_heads):
        base = (h // 2) * LANES + (HEAD_DIM if h % 2 == 0 else 0)
        for j in range(N_DECAY_TERMS):
            sel[j, h, base + j] = 1.0
    return jnp.asarray(sel, BF16)


def _key_decay(logf, n_heads, width):
    b, length, lanes = logf.shape
    sel = _decay_selectors(n_heads, width)
    return pl.pallas_call(
        _key_decay_kernel, out_shape=jax.ShapeDtypeStruct((b, length, width), BF16), grid=(b,),
        in_specs=[pl.BlockSpec((None, length, lanes), lambda i: (i, 0, 0)),
                  pl.BlockSpec(sel.shape, lambda i: (0, 0, 0))],
        out_specs=pl.BlockSpec((None, length, width), lambda i: (i, 0, 0)),
        compiler_params=_cparams(1), name="key_decay",
    )(logf, sel)


def _nt_dot(a, b):
    return lax.dot_general(a, b, (((1,), (1,)), ((), ())), preferred_element_type=F32)


SCORE_LOOKAHEAD = 4


def _pipelined(units, scores, consume):
    ahead = [scores(unit) for unit in units[:SCORE_LOOKAHEAD]]
    for n, unit in enumerate(units):
        s = ahead.pop(0)
        if n + SCORE_LOOKAHEAD < len(units):
            ahead.append(scores(units[n + SCORE_LOOKAHEAD]))
        consume(unit, s)


def _memo(fn):
    cache = {}

    def get(key):
        if key not in cache:
            cache[key] = fn(key)
        return cache[key]
    return get


CACHED_CHUNK_GROUP = 2


def _fox_cached_kernel(q_ref, kn_ref, vn_ref, cn_ref, cp_ref, kp_ref, vp_ref,
                       o_ref, acc_a, acc_b, m_a, m_b, *, tkp, n_past):
    t = q_ref.shape[0]
    lo = lax.broadcasted_iota(I32, (1, LANES), 1) < HEAD_DIM
    q = q_ref[...]
    zero = jnp.zeros_like(q)
    q_heads = (jnp.where(lo, q, zero), jnp.where(lo, zero, q))
    accs = (acc_a, acc_b)
    ms = (m_a, m_b)
    for acc, m in zip(accs, ms):
        acc[...] = jnp.zeros_like(acc)
        m[...] = jnp.full_like(m, NEG_INF)

    def run(chunks):
        @_memo
        def operands(n):
            k, v, c_rows, _ = chunks[n]
            v = v.astype(BF16)
            one = jnp.ones_like(v)
            return k.astype(BF16), (jnp.where(lo, v, one), jnp.where(lo, one, v)), c_rows * LOG2E

        def scores(unit):
            n, hd = unit
            return _nt_dot(q_heads[hd], operands(n)[0])

        def consume(unit, s):
            n, hd = unit
            _, v_heads, c2 = operands(n)
            s = s - c2[hd:hd + 1, :]
            if chunks[n][3] is not None:
                s = jnp.where(chunks[n][3], s, NEG_INF)
            m_old = ms[hd][...]
            m_new = jnp.maximum(m_old, jnp.max(s, axis=1, keepdims=True))
            p = jnp.exp2(s - m_new).astype(BF16)
            accs[hd][...] = jnp.exp2(m_old - m_new) * accs[hd][...] + jnp.dot(
                p, v_heads[hd], preferred_element_type=F32)
            ms[hd][...] = m_new

        _pipelined([(n, hd) for n in range(len(chunks)) for hd in range(2)], scores, consume)

    def cached(c):
        off = pl.multiple_of(c * tkp, tkp)
        return (kp_ref[pl.ds(off, tkp), :], vp_ref[pl.ds(off, tkp), :],
                cp_ref[:, pl.ds(off, tkp)], None)

    def past_body(j, carry):
        run([cached(j * CACHED_CHUNK_GROUP + g) for g in range(CACHED_CHUNK_GROUP)])
        return carry
    lax.fori_loop(0, n_past // CACHED_CHUNK_GROUP, past_body, 0)
    causal = lax.broadcasted_iota(I32, (t, t), 1) <= lax.broadcasted_iota(I32, (t, t), 0)
    run([cached(c) for c in range(n_past - n_past % CACHED_CHUNK_GROUP, n_past)]
        + [(kn_ref[...], vn_ref[...], cn_ref[...], causal)])

    a = acc_a[...]
    b = acc_b[...]
    out = jnp.where(lo, a / pltpu.roll(a, HEAD_DIM, axis=1), b / pltpu.roll(b, HEAD_DIM, axis=1))
    o_ref[...] = out.astype(o_ref.dtype)


def _fox_cached_attention(q, k_new, v_new, c_new, k_past, v_past, c_past, *, tkp):
    b, t, w = q.shape
    n_pairs = w // LANES
    plen = k_past.shape[1]
    n_past = plen // tkp
    assert n_past * tkp == plen
    pair_rows = lambda c: c.reshape(b, n_pairs, 2, c.shape[-1])
    seqspec = lambda n: pl.BlockSpec((None, n, LANES), lambda bi, p: (bi, 0, p))
    cspec = lambda n: pl.BlockSpec((None, None, 2, n), lambda bi, p: (bi, p, 0, 0))
    return pl.pallas_call(
        functools.partial(_fox_cached_kernel, tkp=tkp, n_past=n_past),
        out_shape=jax.ShapeDtypeStruct((b, t, w), BF16),
        grid=(b, n_pairs),
        in_specs=[seqspec(t), seqspec(t), seqspec(t), cspec(t), cspec(plen),
                  seqspec(plen), seqspec(plen)],
        out_specs=seqspec(t),
        scratch_shapes=[pltpu.VMEM((t, LANES), F32)] * 2 + [pltpu.VMEM((t, 1), F32)] * 2,
        compiler_params=_cparams(2),
        name="fox_cached_attn",
    )(q, k_new, v_new, pair_rows(c_new), pair_rows(c_past), k_past, v_past)


def _diff_lambda(dl, lambda_init):
    return (jnp.exp(jnp.sum(dl[0:1] * dl[1:2], axis=1, keepdims=True))
            - jnp.exp(jnp.sum(dl[2:3] * dl[3:4], axis=1, keepdims=True)) + lambda_init)


def _diff_cached_kernel(q_ref, kn_ref, vn_ref, bprev_ref, bdiag_ref, lam_ref, g_ref,
                        kp_ref, vp_ref, o_ref, acc0, acc1, m0, m1, l0, l1,
                        *, tkp, n_past, lambda_init):
    lo = lax.broadcasted_iota(I32, (1, LANES), 1) < HEAD_DIM
    q = q_ref[...]
    zero = jnp.zeros_like(q)
    q_maps = (jnp.where(lo, q, zero), jnp.where(lo, zero, q))
    accs, ms, ls = (acc0, acc1), (m0, m1), (l0, l1)
    for acc, m, l in zip(accs, ms, ls):
        acc[...] = jnp.zeros_like(acc)
        l[...] = jnp.zeros_like(l)
        m[...] = jnp.full_like(m, NEG_INF)

    def run(chunks):
        @_memo
        def operands(n):
            return chunks[n][0].astype(BF16), chunks[n][1].astype(BF16)

        def scores(unit):
            n, mp = unit
            return _nt_dot(q_maps[mp], operands(n)[0])

        def consume(unit, s):
            n, mp = unit
            if chunks[n][2] is not None:
                s = jnp.maximum(s, NEG_INF) + chunks[n][2]
            m_old = ms[mp][...]
            m_new = jnp.maximum(m_old, jnp.max(s, axis=1, keepdims=True))
            p = jnp.exp2(s - m_new)
            alpha = jnp.exp2(m_old - m_new)
            ls[mp][...] = alpha * ls[mp][...] + jnp.sum(p, axis=1, keepdims=True)
            accs[mp][...] = alpha * accs[mp][...] + jnp.dot(
                p.astype(BF16), operands(n)[1], preferred_element_type=F32)
            ms[mp][...] = m_new

        _pipelined([(n, mp) for n in range(len(chunks)) for mp in range(2)], scores, consume)

    def cached(c, bias=None):
        off = pl.multiple_of(c * tkp, tkp)
        return kp_ref[pl.ds(off, tkp), :], vp_ref[pl.ds(off, tkp), :], bias

    def past_body(j, carry):
        run([cached(j * CACHED_CHUNK_GROUP + g) for g in range(CACHED_CHUNK_GROUP)])
        return carry
    n_far = n_past - 1
    lax.fori_loop(0, n_far // CACHED_CHUNK_GROUP, past_body, 0)
    run([cached(c) for c in range(n_far - n_far % CACHED_CHUNK_GROUP, n_far)]
        + [cached(n_far, bprev_ref[...]), (kn_ref[...], vn_ref[...], bdiag_ref[...])])

    lam = _diff_lambda(lam_ref[...], lambda_init)
    o = acc0[...] / l0[...] - lam * (acc1[...] / l1[...])
    o = o * _rms_scale(o, SUBLN_EPS) * g_ref[...] * (1.0 - lambda_init)
    o_ref[...] = o.astype(o_ref.dtype)


def _diff_cached_attention(q, k_new, v_new, bias_prev, bias_diag, diff_lambda, subln_g,
                           k_past, v_past, *, tkp, lambda_init):
    b, t, w = q.shape
    plen = k_past.shape[1]
    n_past = plen // tkp
    assert n_past * tkp == plen and tkp >= MAX_DISTANCE and t <= CHUNK and plen % CHUNK == 0
    seqspec = lambda n: pl.BlockSpec((None, n, LANES), lambda bi, h: (bi, 0, h))
    head_tile = lambda arr: pl.BlockSpec((None,) + arr.shape[1:], lambda bi, h: (h, 0, 0))
    const2 = lambda arr: pl.BlockSpec(arr.shape, lambda bi, h: (0, 0))
    return pl.pallas_call(
        functools.partial(_diff_cached_kernel, tkp=tkp, n_past=n_past, lambda_init=lambda_init),
        out_shape=jax.ShapeDtypeStruct((b, t, w), BF16),
        grid=(b, w // LANES),
        in_specs=[seqspec(t), seqspec(t), seqspec(t), head_tile(bias_prev), head_tile(bias_diag),
                  const2(diff_lambda), const2(subln_g), seqspec(plen), seqspec(plen)],
        out_specs=seqspec(t),
        scratch_shapes=[pltpu.VMEM((t, LANES), F32)] * 2 + [pltpu.VMEM((t, 1), F32)] * 4,
        compiler_params=_cparams(2),
        name="diff_cached_attn",
    )(q, k_new, v_new, bias_prev, bias_diag, diff_lambda, subln_g, k_past, v_past)


def _fox_prompt_kernel(q_ref, k_ref, kx_ref, vt_ref, o_ref, acc_a, acc_b, m_a, m_b, *, tq, rs):
    i = pl.program_id(2)
    n_qs = tq // rs
    lane = lax.broadcasted_iota(I32, (1, LANES), 1)
    lo = lane < HEAD_DIM
    top = lax.broadcasted_iota(I32, (LANES, rs), 0) < HEAD_DIM

    q = q_ref[...].astype(F32)
    ones_a = jnp.where(lane < HEAD_DIM + N_DECAY_TERMS, 1.0, 0.0)
    ones_b = jnp.where(lane < N_DECAY_TERMS, 1.0, 0.0)
    q_heads = (jnp.where(lo, q, ones_a).astype(BF16), jnp.where(lo, ones_b, q).astype(BF16))
    accs = (acc_a, acc_b)
    ms = (m_a, m_b)
    for acc, m in zip(accs, ms):
        acc[...] = jnp.zeros_like(acc)
        m[...] = jnp.full_like(m, NEG_INF)

    def chunk_group(first, units):
        @_memo
        def operands(kc):
            off = pl.multiple_of((first + kc) * rs, rs)
            k = k_ref[pl.ds(off, rs), :]
            kx = kx_ref[pl.ds(off, rs), :]
            vt = vt_ref[:, pl.ds(off, rs)]
            ones_v = jnp.ones_like(vt)
            return ((jnp.where(lo, k, kx), jnp.where(lo, kx, k)),
                    (jnp.where(top, vt, ones_v), jnp.where(top, ones_v, vt)))

        def scores(unit):
            kc, hd, qs, _ = unit
            return _nt_dot(operands(kc)[0][hd], q_heads[hd][qs * rs:(qs + 1) * rs])

        def consume(unit, s):
            kc, hd, qs, mask = unit
            cols = pl.ds(qs * rs, rs)
            if mask is not None:
                s = jnp.where(mask, s, NEG_INF)
            m_old = ms[hd][:, cols]
            m_new = jnp.maximum(m_old, jnp.max(s, axis=0, keepdims=True))
            p = jnp.exp2(s - m_new).astype(BF16)
            accs[hd][:, cols] = jnp.exp2(m_old - m_new) * accs[hd][:, cols] + jnp.dot(
                operands(kc)[1][hd], p, preferred_element_type=F32)
            ms[hd][:, cols] = m_new

        _pipelined(units, scores, consume)

    def below(j, carry):
        chunk_group(n_qs * j, [(kc, hd, qs, None) for kc in range(n_qs) for qs in range(n_qs)
                               for hd in range(2)])
        return carry
    lax.fori_loop(0, i, below, 0)
    causal = (lax.broadcasted_iota(I32, (rs, rs), 0) <= lax.broadcasted_iota(I32, (rs, rs), 1))
    chunk_group(n_qs * i, [(kc, hd, qs, causal if qs == kc else None) for kc in range(n_qs)
                           for qs in range(kc, n_qs) for hd in range(2)])

    a = acc_a[...]
    b = acc_b[...]
    out_t = jnp.concatenate([a[:HEAD_DIM] / a[HEAD_DIM:HEAD_DIM + 1], b[HEAD_DIM:] / b[0:1]],
                            axis=0)
    o_ref[...] = out_t.T.astype(o_ref.dtype)


def _fox_prompt_attention(q, k, kx, vt, *, tq, rs):
    b, t, w = q.shape
    qspec = pl.BlockSpec((None, tq, LANES), lambda bi, p, i: (bi, i, p))
    kspec = pl.BlockSpec((None, t, LANES), lambda bi, p, i: (bi, 0, p))
    return pl.pallas_call(
        functools.partial(_fox_prompt_kernel, tq=tq, rs=rs),
        out_shape=jax.ShapeDtypeStruct((b, t, w), BF16),
        grid=(b, w // LANES, t // tq),
        in_specs=[qspec, kspec, kspec, pl.BlockSpec((LANES, t), lambda bi, p, i: (p, bi))],
        out_specs=qspec,
        scratch_shapes=[pltpu.VMEM((LANES, tq), F32), pltpu.VMEM((LANES, tq), F32),
                        pltpu.VMEM((1, tq), F32), pltpu.VMEM((1, tq), F32)],
        compiler_params=_cparams(3),
        name="fox_prompt_attn",
    )(q, k, kx, vt)


ONES_ROWS = 16


def _diff_prompt_kernel(q_ref, k_ref, vt_ref, bprev_ref, bdiag_ref, lam_ref, g_ref, o_ref,
                        acc0, acc1, m0, m1, *, tq, rs, lambda_init):
    i = pl.program_id(2)
    n_qs = tq // rs
    lo = lax.broadcasted_iota(I32, (1, LANES), 1) < HEAD_DIM

    q = q_ref[...]
    zero = jnp.zeros_like(q)
    q_maps = (jnp.where(lo, q, zero), jnp.where(lo, zero, q))
    accs, ms = (acc0, acc1), (m0, m1)
    for acc, m in zip(accs, ms):
        acc[...] = jnp.zeros_like(acc)
        m[...] = jnp.full_like(m, NEG_INF)

    def chunk_group(first, blocks_ahead):
        @_memo
        def operands(kc):
            off = pl.multiple_of((first + kc) * rs, rs)
            return (k_ref[pl.ds(off, rs), :],
                    jnp.concatenate([vt_ref[:, pl.ds(off, rs)], jnp.ones((ONES_ROWS, rs), BF16)],
                                    axis=0))

        def scores(unit):
            kc, mp, qs = unit
            return _nt_dot(operands(kc)[0], q_maps[mp][qs * rs:(qs + 1) * rs])

        def consume(unit, s):
            kc, mp, qs = unit
            cols = pl.ds(qs * rs, rs)
            distance = blocks_ahead + qs - kc
            if distance == 1:
                s = jnp.maximum(s, NEG_INF) + bprev_ref[...]
            elif distance == 0:
                s = jnp.maximum(s, NEG_INF) + bdiag_ref[...]
            m_old = ms[mp][:, cols]
            m_new = jnp.maximum(m_old, jnp.max(s, axis=0, keepdims=True))
            p = jnp.exp2(s - m_new).astype(BF16)
            accs[mp][:, cols] = jnp.exp2(m_old - m_new) * accs[mp][:, cols] + jnp.dot(
                operands(kc)[1], p, preferred_element_type=F32)
            ms[mp][:, cols] = m_new

        _pipelined([(kc, mp, qs) for kc in range(n_qs) for qs in range(n_qs) for mp in range(2)
                    if blocks_ahead + qs - kc >= 0], scores, consume)

    def far_below(j, carry):
        chunk_group(n_qs * j, 2 * n_qs)
        return carry
    lax.fori_loop(0, jnp.maximum(i - 1, 0), far_below, 0)

    @pl.when(i >= 1)
    def _():
        chunk_group(n_qs * (i - 1), n_qs)

    chunk_group(n_qs * i, 0)

    lam = _diff_lambda(lam_ref[...], lambda_init)
    a0 = acc0[...]
    a1 = acc1[...]
    o = a0[:LANES] / a0[LANES:LANES + 1] - lam * (a1[:LANES] / a1[LANES:LANES + 1])
    scale = lax.rsqrt(jnp.mean(o * o, axis=0, keepdims=True) + SUBLN_EPS)
    o = o * scale * g_ref[...] * (1.0 - lambda_init)
    o_ref[...] = o.T.astype(o_ref.dtype)


def _diff_prompt_attention(q, k, vt, bias_prev, bias_diag, diff_lambda, subln_g, *,
                           tq, rs, lambda_init):
    b, t, w = q.shape
    assert rs >= MAX_DISTANCE
    qspec = pl.BlockSpec((None, tq, LANES), lambda bi, h, i: (bi, i, h))
    head_tile = lambda arr: pl.BlockSpec((None,) + arr.shape[1:], lambda bi, h, i: (h, 0, 0))
    const2 = lambda arr: pl.BlockSpec(arr.shape, lambda bi, h, i: (0, 0))
    return pl.pallas_call(
        functools.partial(_diff_prompt_kernel, tq=tq, rs=rs, lambda_init=lambda_init),
        out_shape=jax.ShapeDtypeStruct((b, t, w), BF16),
        grid=(b, w // LANES, t // tq),
        in_specs=[qspec, pl.BlockSpec((None, t, LANES), lambda bi, h, i: (bi, 0, h)),
                  pl.BlockSpec((LANES, t), lambda bi, h, i: (h, bi)),
                  head_tile(bias_prev), head_tile(bias_diag), const2(diff_lambda),
                  const2(subln_g)],
        out_specs=qspec,
        scratch_shapes=[pltpu.VMEM((LANES + ONES_ROWS, tq), F32)] * 2
        + [pltpu.VMEM((1, tq), F32)] * 2,
        compiler_params=_cparams(3),
        name="diff_prompt_attn",
    )(q, k, vt, bias_prev, bias_diag, diff_lambda, subln_g)


def _t5_bucket(rel):
    nb = NUM_BUCKETS // 2
    max_exact = nb // 2
    ret = jnp.where(rel > 0, nb, 0)
    n = jnp.abs(rel)
    nf = jnp.maximum(n, max_exact).astype(F32)
    large = max_exact + (jnp.log(nf / max_exact) / math.log(MAX_DISTANCE / max_exact)
                         * (nb - max_exact)).astype(I32)
    large = jnp.minimum(large, nb - 1)
    return ret + jnp.where(n < max_exact, n, large)


def _bias_tiles(rel_table, tq, tk_prev):
    buckets = jnp.arange(NUM_BUCKETS, dtype=I32)

    def lookup(rel):
        hit = _t5_bucket(rel)[None, :] == buckets[:, None]
        return jnp.sum(jnp.where(hit[:, None, :], rel_table[:, :, None], 0.0), axis=0)

    far = lookup(jnp.full((1,), -4 * MAX_DISTANCE, I32))

    def tile(rel0, tk):
        period = tq + tk
        j = jnp.arange(period, dtype=I32)
        strip = lookup(rel0 + jnp.where(j < tk, j, j - period)) - far
        flat = jnp.tile(strip, (1, tq))[:, :tq * (period - 1)]
        return (flat.reshape(-1, tq, period - 1)[:, :, :tk] * LOG2E).astype(F32)

    return tile(-tk_prev, tk_prev), tile(0, tq)


def _router_kernel(x_ref, of_ref, od_ref, wof_ref, wod_ref, g_ref, wr_ref, br_ref,
                   x1_ref, hn_ref, te_ref, tw_ref, rank_ref, cnt_ref, *, tm):
    step = pl.program_id(0)

    @pl.when(step == 0)
    def _():
        cnt_ref[...] = jnp.zeros_like(cnt_ref)

    x1 = (x_ref[...] + jnp.dot(of_ref[...], wof_ref[...], preferred_element_type=F32)
          + jnp.dot(od_ref[...], wod_ref[...], preferred_element_type=F32))
    x1_ref[...] = x1
    hn = x1 * _rms_scale(x1, NORM_EPS) * g_ref[...]
    half = hn.shape[1] // 2
    _store_token_slabs(hn_ref, _pack_bf16_pairs(hn[:, :half], hn[:, half:]))

    hn_hi = hn.astype(BF16)
    hn_lo = (hn - hn_hi.astype(F32)).astype(BF16)
    both = jnp.dot(hn_hi, wr_ref[...], preferred_element_type=F32)
    logits = (both[:, :LANES] + both[:, LANES:] + br_ref[...]
              + jnp.dot(hn_lo, wr_ref[:, :LANES], preferred_element_type=F32))
    lane = lax.broadcasted_iota(I32, logits.shape, 1)
    lane_f = lane.astype(F32)
    sel = jnp.zeros(logits.shape, F32)
    vals, idxs = [], []
    for _ in range(TOP_K):
        mx = jnp.max(logits, axis=1, keepdims=True)
        idx = jnp.min(jnp.where(logits == mx, lane_f, float(LANES)), axis=1, keepdims=True)
        hit = lane_f == idx
        sel = jnp.where(hit, 1.0, sel)
        logits = jnp.where(hit, -jnp.inf, logits)
        vals.append(mx)
        idxs.append(idx)
    exps = [jnp.exp(v - vals[0]) for v in vals]
    denom = exps[0] + exps[1] + exps[2] + exps[3]

    earlier = (lax.broadcasted_iota(I32, (tm, tm), 1)
               < lax.broadcasted_iota(I32, (tm, tm), 0)).astype(BF16)
    before = cnt_ref[...] + jnp.dot(earlier, sel.astype(BF16), preferred_element_type=F32)
    for k in range(TOP_K):
        te_ref[:, k:k + 1] = idxs[k].astype(I32)
        tw_ref[:, k:k + 1] = exps[k] / denom
        rank_ref[:, k:k + 1] = jnp.sum(
            jnp.where(lane_f == idxs[k], before, 0.0), axis=1, keepdims=True).astype(I32)
    cnt_ref[...] += jnp.sum(sel, axis=0, keepdims=True)


def _router(x2d, o_fox, o_diff, wo_f, wo_d, g_ffn, w_r, b_r, *, tm):
    n, d = x2d.shape
    row = lambda i: (i, 0)
    const = lambda i: (0, 0)
    full = lambda arr: pl.BlockSpec(arr.shape, const)
    parts = d // 2 // LANES
    return pl.pallas_call(
        functools.partial(_router_kernel, tm=tm),
        out_shape=[jax.ShapeDtypeStruct((n, d), F32), jax.ShapeDtypeStruct((n * parts, LANES), U32),
                   jax.ShapeDtypeStruct((n, TOP_K), I32), jax.ShapeDtypeStruct((n, TOP_K), F32),
                   jax.ShapeDtypeStruct((n, TOP_K), I32), jax.ShapeDtypeStruct((1, LANES), F32)],
        grid=(n // tm,),
        in_specs=[pl.BlockSpec((tm, d), row), pl.BlockSpec((tm, o_fox.shape[1]), row),
                  pl.BlockSpec((tm, o_diff.shape[1]), row), full(wo_f), full(wo_d),
                  full(g_ffn), full(w_r), full(b_r)],
        out_specs=[pl.BlockSpec((tm, d), row), pl.BlockSpec((tm * parts, LANES), row),
                   pl.BlockSpec((tm, TOP_K), row), pl.BlockSpec((tm, TOP_K), row),
                   pl.BlockSpec((tm, TOP_K), row), pl.BlockSpec((1, LANES), const)],
        compiler_params=_cparams(1),
        name="router",
    )(x2d, o_fox, o_diff, wo_f, wo_d, g_ffn, w_r, b_r)


def _row_copy(src, src_row, dst, dst_row, sem, parts):
    return pltpu.make_async_copy(src.at[pl.ds(src_row * parts, parts), :],
                                 dst.at[pl.ds(dst_row * parts, parts), :], sem)


def _dispatch_kernel(tail_start_ref, tail_valid_ref, pos_ref, hn_ref, xs_hbm, zeros_ref, sem,
                     *, tm, tme, n_experts):
    parts = hn_ref.shape[0] // tm
    slab = tme * parts

    def tail_copy(e):
        return pltpu.make_async_copy(
            zeros_ref, xs_hbm.at[pl.ds(pl.multiple_of(tail_start_ref[e] * parts, slab), slab), :],
            sem.at[0])

    @pl.when(pl.program_id(0) == 0)
    def _():
        zeros_ref[...] = jnp.zeros_like(zeros_ref)
        for e in range(n_experts):
            @pl.when(tail_valid_ref[e] > 0)
            def _():
                tail_copy(e).start()
        for e in range(n_experts):
            @pl.when(tail_valid_ref[e] > 0)
            def _():
                tail_copy(e).wait()

    def start_row(r, carry):
        for k in range(TOP_K):
            _row_copy(hn_ref, r, xs_hbm, pos_ref[r, k], sem.at[1], parts).start(priority=k % 2)
        return carry

    def wait_row(r, carry):
        for k in range(TOP_K):
            _row_copy(hn_ref, r, xs_hbm, pos_ref[r, k], sem.at[1], parts).wait()
        return carry

    lax.fori_loop(0, tm, start_row, 0, unroll=True)
    lax.fori_loop(0, tm, wait_row, 0, unroll=8)


def _dispatch(hn_slabs, pos, tail_start, tail_valid, *, m_pad, tm, tme):
    n = pos.shape[0]
    parts = hn_slabs.shape[0] // n
    n_experts = tail_start.shape[0]
    grid_spec = pltpu.PrefetchScalarGridSpec(
        num_scalar_prefetch=2,
        grid=(n // tm,),
        in_specs=[pl.BlockSpec((tm, TOP_K), lambda i, *_: (i, 0), memory_space=pltpu.SMEM),
                  pl.BlockSpec((tm * parts, LANES), lambda i, *_: (i, 0))],
        out_specs=pl.BlockSpec(memory_space=pl.ANY),
        scratch_shapes=[pltpu.VMEM((tme * parts, LANES), U32), pltpu.SemaphoreType.DMA((2,))],
    )
    return pl.pallas_call(
        functools.partial(_dispatch_kernel, tm=tm, tme=tme, n_experts=n_experts),
        out_shape=jax.ShapeDtypeStruct((m_pad * parts, LANES), U32),
        grid_spec=grid_spec,
        compiler_params=pltpu.CompilerParams(
            dimension_semantics=("arbitrary",), vmem_limit_bytes=VMEM_LIMIT_BYTES,
            has_side_effects=True),
        name="dispatch",
    )(tail_start, tail_valid, pos, hn_slabs)


def _experts_kernel(te_ref, nact_ref, xs_ref, wg_ref, bg_ref, wu_ref, bu_ref, wd_ref, bd_ref,
                    ys_ref, *, tme):
    @pl.when(pl.program_id(0) < nact_ref[0])
    def _():
        parts = xs_ref.shape[0] // tme
        half = parts * LANES
        x_lo, x_hi = _unpack_bf16_pairs(_load_token_slabs(xs_ref, tme, parts))
        x_lo = x_lo.astype(BF16)
        x_hi = x_hi.astype(BF16)

        def proj(w_ref, b_ref):
            return (jnp.dot(x_lo, w_ref[:half, :], preferred_element_type=F32)
                    + jnp.dot(x_hi, w_ref[half:, :], preferred_element_type=F32) + b_ref[...])

        g = jnp.minimum(proj(wg_ref, bg_ref), SWIGLU_LIMIT)
        u = jnp.clip(proj(wu_ref, bu_ref), -SWIGLU_LIMIT, SWIGLU_LIMIT)
        a = ((u + 1.0) * (g * jax.nn.sigmoid(SWIGLU_ALPHA * g))).astype(BF16)
        y = jnp.dot(a, wd_ref[...], preferred_element_type=F32) + bd_ref[...]
        _store_token_slabs(ys_ref, _pack_bf16_pairs(y[:, :half], y[:, half:]))


def _experts(xs, tile_expert, n_active, w_gate, b_gate, w_up, b_up, w_down, b_down, *, tme):
    d, d_ff = w_gate.shape[1:]
    parts = d // 2 // LANES
    n_tiles = xs.shape[0] // (tme * parts)
    row = lambda t, te, na: (jnp.minimum(t, na[0] - 1), 0)
    wspec = lambda shape: pl.BlockSpec((None,) + shape, lambda t, te, na: (te[t], 0, 0))
    grid_spec = pltpu.PrefetchScalarGridSpec(
        num_scalar_prefetch=2,
        grid=(n_tiles,),
        in_specs=[pl.BlockSpec((tme * parts, LANES), row),
                  wspec((d, d_ff)), wspec((1, d_ff)), wspec((d, d_ff)), wspec((1, d_ff)),
                  wspec((d_ff, d)), wspec((1, d))],
        out_specs=pl.BlockSpec((tme * parts, LANES), row),
    )
    return pl.pallas_call(
        functools.partial(_experts_kernel, tme=tme),
        out_shape=jax.ShapeDtypeStruct(xs.shape, U32),
        grid_spec=grid_spec,
        compiler_params=_cparams(1),
        name="experts",
    )(tile_expert, n_active, xs, w_gate, b_gate, w_up, b_up, w_down, b_down)


def _combine_kernel(pos_ref, pos_next_ref, x1_ref, tw_ref, p_ref, wg_ref, wp_ref, gple_ref,
                    gfin_ref, ys_hbm, y_ref, gbuf, sem, *, tm):
    step = pl.program_id(0)
    slot = step % 2
    parts = gbuf.shape[2] // tm

    def gather(tile_pos_ref, tile_slot, wait):
        def row(r, carry):
            for k in range(TOP_K):
                copy = _row_copy(ys_hbm, tile_pos_ref[r, k], gbuf.at[tile_slot, k], r,
                                 sem.at[tile_slot], parts)
                if wait:
                    copy.wait()
                else:
                    copy.start(priority=k % 2)
            return carry
        lax.fori_loop(0, tm, row, 0, unroll=8)

    @pl.when(step == 0)
    def _():
        gather(pos_ref, 0, wait=False)

    @pl.when(step + 1 < pl.num_programs(0))
    def _():
        gather(pos_next_ref, 1 - slot, wait=False)

    gather(pos_ref, slot, wait=True)

    x1 = x1_ref[...]
    half = x1.shape[1] // 2
    tw = tw_ref[...]
    moe_lo = jnp.zeros((tm, half), F32)
    moe_hi = jnp.zeros((tm, half), F32)
    for k in range(TOP_K):
        lo, hi = _unpack_bf16_pairs(_load_token_slabs(gbuf.at[slot, k], tm, parts))
        moe_lo = moe_lo + tw[:, k:k + 1] * lo
        moe_hi = moe_hi + tw[:, k:k + 1] * hi
    x2 = x1 + jnp.concatenate([moe_lo, moe_hi], axis=1)
    hp = (x2 * _rms_scale(x2, NORM_EPS) * gple_ref[...]).astype(BF16)
    gate = jax.nn.sigmoid(jnp.dot(hp, wg_ref[...], preferred_element_type=F32))
    proj = jnp.dot(p_ref[...].astype(BF16), wp_ref[...], preferred_element_type=F32)
    x3 = x2 + proj * gate
    y_ref[...] = x3 * _rms_scale(x3, NORM_EPS) * gfin_ref[...]


def _combine(x1, top_w, pos, ys, p2d, w_ple_gate, w_ple_proj, g_ple, g_final, *, tm):
    n, d = x1.shape
    parts = d // 2 // LANES
    row = lambda i: (i, 0)
    const = lambda i: (0, 0)
    full = lambda arr: pl.BlockSpec(arr.shape, const)
    return pl.pallas_call(
        functools.partial(_combine_kernel, tm=tm),
        out_shape=jax.ShapeDtypeStruct((n, d), F32),
        grid=(n // tm,),
        in_specs=[pl.BlockSpec((tm, TOP_K), row, memory_space=pltpu.SMEM),
                  pl.BlockSpec((tm, TOP_K), lambda i: (jnp.minimum(i + 1, n // tm - 1), 0),
                               memory_space=pltpu.SMEM),
                  pl.BlockSpec((tm, d), row), pl.BlockSpec((tm, TOP_K), row),
                  pl.BlockSpec((tm, p2d.shape[1]), row), full(w_ple_gate), full(w_ple_proj),
                  full(g_ple), full(g_final), pl.BlockSpec(memory_space=pl.ANY)],
        out_specs=pl.BlockSpec((tm, d), row),
        scratch_shapes=[pltpu.VMEM((2, TOP_K, tm * parts, LANES), U32),
                        pltpu.SemaphoreType.DMA((2,))],
        compiler_params=_cparams(1),
        name="combine",
    )(pos, pos, x1, top_w, p2d, w_ple_gate, w_ple_proj, g_ple, g_final, ys)


def _pick_tile(n, want):
    t = min(n, want)
    assert n % t == 0, (n, t)
    return t


def _layer_group(x, p, past, wts, lambda_init, g_final):
    (g_mix, w_main, w_vt, w_f, b_f, diff_lambda, subln_g, rel_table, wo_f, wo_d, g_ffn, w_r, b_r,
     w_gate, b_gate, w_up, b_up, w_down, b_down, g_ple, w_ple_gate, w_ple_proj) = wts
    b, t, d = x.shape
    n = b * t
    n_fox = (w_main.shape[1] // 6) // HEAD_DIM
    n_experts = w_gate.shape[0]
    x2d = x.reshape(n, d)

    tm = _pick_tile(n, 512)
    (qf, qd, kf, vf, kd, vd, kf16, kd16, vf16, vd16, logf_lanes) = _inproj(
        x2d, g_mix, w_main, w_vt, w_f, b_f, tm=tm, values_transposed=past is None)
    logf = logf_lanes[:, :n_fox]
    width = qf.shape[1]
    as_seq = lambda a: a.reshape(b, t, a.shape[-1])

    tq = _pick_tile(t, 2048)
    rs = _pick_tile(tq, 256)
    if past is None:
        kx = _key_decay(as_seq(logf_lanes), n_fox, width)
        o_fox = _fox_prompt_attention(as_seq(qf), as_seq(kf16), kx, vf16, tq=tq, rs=rs)
        bias_prev, bias_diag = _bias_tiles(rel_table, rs, rs)
        block = jnp.arange(rs, dtype=I32) // CHUNK
        bias_diag = jnp.where(block[:, None] <= block[None, :], jnp.swapaxes(bias_diag, 1, 2),
                              NEG_INF)
        o_diff = _diff_prompt_attention(
            as_seq(qd), as_seq(kd16), vd16, jnp.swapaxes(bias_prev, 1, 2), bias_diag,
            diff_lambda, subln_g.reshape(-1, 1), tq=tq, rs=rs, lambda_init=lambda_init)
    else:
        pk_f, pv_f, plogf, pk_d, pv_d = past
        plen = pk_f.shape[1]
        tkp = 512
        total = plen + t
        padded = -(-total // LANES) * LANES
        seq = jnp.concatenate([jnp.transpose(plogf.astype(F32), (0, 2, 1)),
                               jnp.transpose(as_seq(logf), (0, 2, 1))], axis=2)
        c_all = _cumsum_lanes(jnp.pad(seq, ((0, 0), (0, 0), (0, padded - total))))
        flat = lambda a: a.reshape(b, plen, width)
        o_fox = _fox_cached_attention(as_seq(qf), as_seq(kf16), as_seq(vf16),
                                      c_all[:, :, plen:total], flat(pk_f), flat(pv_f),
                                      c_all[:, :, :plen], tkp=tkp)
        bias_prev, bias_diag = _bias_tiles(rel_table, t, tkp)
        o_diff = _diff_cached_attention(as_seq(qd), as_seq(kd16), as_seq(vd16), bias_prev,
                                        bias_diag, diff_lambda, subln_g, flat(pk_d), flat(pv_d),
                                        tkp=tkp, lambda_init=lambda_init)

    tmr = _pick_tile(n, 512)
    x1, hn_slabs, top_e, top_w, rank, counts = _router(
        x2d, o_fox.reshape(n, width), o_diff.reshape(n, width), wo_f, wo_d, g_ffn, w_r, b_r, tm=tmr)

    tme = 512 if n * TOP_K >= 64 * 512 else 256
    n_tiles = -(-(n * TOP_K) // tme) + n_experts
    m_pad = n_tiles * tme
    cnt = counts[0, :n_experts].astype(I32)
    group = -(-cnt // tme) * tme
    ends = jnp.cumsum(group)
    starts = ends - group
    experts = jnp.arange(n_experts, dtype=I32)
    pos = rank + jnp.sum(jnp.where(top_e[:, :, None] == experts, starts, 0), axis=2)
    n_active = (ends[-1] // tme).astype(I32)
    tile_ids = jnp.arange(n_tiles, dtype=I32)
    tile_expert = jnp.sum(jnp.minimum(tile_ids, n_active - 1)[:, None] >= (ends // tme)[None, :],
                          axis=1).astype(I32)
    tile_expert = jnp.minimum(tile_expert, n_experts - 1)
    tail_start = jnp.maximum(ends - tme, 0).astype(I32)
    tail_valid = (group > 0).astype(I32)

    xs = _dispatch(hn_slabs, pos, tail_start, tail_valid, m_pad=m_pad, tm=_pick_tile(n, 512),
                   tme=tme)
    ys = _experts(xs, tile_expert, n_active.reshape(1), w_gate, b_gate, w_up, b_up, w_down, b_down,
                  tme=tme)
    y = _combine(x1, top_w, pos, ys, p.reshape(n, p.shape[-1]), w_ple_gate, w_ple_proj, g_ple,
                 g_final, tm=_pick_tile(n, 256))
    return y.reshape(b, t, d), (kf, vf, logf, kd, vd)


def kernel(x_prompt, x_sample, p_prompt, p_sample, cache_fox_k, cache_fox_v, cache_fox_logf, cache_diff_k, cache_diff_v, g_mix, w_in, b_forget, diff_lambda, diff_subln_g, rel_bias_table, w_o, g_ffn, w_router, b_router, w_gate, b_gate, w_up, b_up, w_down, b_down, g_ple, w_ple_gate, w_ple_proj, g_final):
    depth = g_mix.shape[0]
    assert depth == 1, "the final norm is fused into the last layer; one layer supported"
    d = x_prompt.shape[-1]
    n_fox = cache_fox_k.shape[3]
    fox_w = n_fox * cache_fox_k.shape[4]
    n_diff = cache_diff_k.shape[3]
    diff_w = n_diff * 2 * cache_diff_k.shape[5]
    assert fox_w == diff_w and fox_w % LANES == 0
    n_experts = w_router.shape[-1]
    assert TOP_K <= n_experts <= LANES
    row2 = lambda a: a.reshape(1, -1).astype(F32)

    i = 0
    lambda_init = 0.8 - 0.6 * math.exp(-0.3 * i)
    off_fk, off_fv, off_ff = fox_w, 2 * fox_w, 3 * fox_w
    off_dq = off_ff + n_fox
    off_dk, off_dv = off_dq + diff_w, off_dq + 2 * diff_w
    w = w_in[i]
    cols = lambda o, wd: w[:, o:o + wd]
    w_main = jnp.concatenate([cols(0, fox_w), cols(off_dq, diff_w), cols(off_fk, fox_w),
                              cols(off_fv, fox_w), cols(off_dk, diff_w), cols(off_dv, diff_w)],
                             axis=1).astype(BF16)
    w_vt = jnp.concatenate([cols(off_fv, fox_w), cols(off_dv, diff_w)], axis=1).T.astype(BF16)
    w_f = jnp.pad(cols(off_ff, n_fox), ((0, 0), (0, LANES - n_fox))).astype(BF16)
    b_f = jnp.pad(row2(b_forget[i]), ((0, 0), (0, LANES - n_fox)))
    w_r = jnp.pad(w_router[i].astype(F32), ((0, 0), (0, LANES - n_experts)))
    w_r_hi = w_r.astype(BF16)
    w_r = jnp.concatenate([w_r_hi, (w_r - w_r_hi.astype(F32)).astype(BF16)], axis=1)
    b_r = jnp.pad(row2(b_router[i]), ((0, 0), (0, LANES - n_experts)), constant_values=NEG_INF)
    wo = w_o[i].astype(BF16)
    bias3 = lambda a: a.reshape(n_experts, 1, -1).astype(F32)
    wts = (row2(g_mix[i]), w_main, w_vt, w_f, b_f, diff_lambda[i].astype(F32), row2(diff_subln_g[i]),
           rel_bias_table.astype(F32), wo[:fox_w], wo[fox_w:], row2(g_ffn[i]), w_r, b_r,
           w_gate[i].astype(BF16), bias3(b_gate[i]), w_up[i].astype(BF16), bias3(b_up[i]),
           w_down[i].astype(BF16), bias3(b_down[i]), row2(g_ple[i]),
           w_ple_gate[i].astype(BF16), w_ple_proj[i].astype(BF16))
    gf = row2(g_final)

    yp, new_p = _layer_group(x_prompt, p_prompt[i], None, wts, lambda_init, gf)
    past_s = (cache_fox_k[i], cache_fox_v[i], cache_fox_logf[i], cache_diff_k[i], cache_diff_v[i])
    ys, new_s = _layer_group(x_sample, p_sample[i], past_s, wts, lambda_init, gf)

    def rows(new, b, t):
        kf, vf, logf, kd, vd = new
        return (kf.reshape(1, b, t, n_fox, -1), vf.reshape(1, b, t, n_fox, -1),
                logf.reshape(1, b, t, n_fox), kd.reshape(1, b, t, n_diff, 2, -1),
                vd.reshape(1, b, t, n_diff, -1))

    bp, tp = x_prompt.shape[:2]
    bs, ts = x_sample.shape[:2]
    return (yp, ys) + rows(new_p, bp, tp) + rows(new_s, bs, ts)
```

```python
import functools
import math

import jax
import jax.numpy as jnp
import numpy as np
from jax import lax
from jax.experimental import pallas as pl
from jax.experimental.pallas import tpu as pltpu

HEAD_DIM = 64
CHUNK = 64
NUM_BUCKETS = 32
MAX_DISTANCE = 128
TOP_K = 4
SWIGLU_LIMIT = 7.0
SWIGLU_ALPHA = 1.702
NORM_EPS = 1e-6
SUBLN_EPS = 1e-5
NEG_INF = -1e30
LOG2E = math.log2(math.e)

LANES = 128
VMEM_LIMIT_BYTES = 56 * 1024 * 1024

F32 = jnp.float32
BF16 = jnp.bfloat16
I32 = jnp.int32
U32 = jnp.uint32


def _cparams(n_axes):
    return pltpu.CompilerParams(
        dimension_semantics=("arbitrary",) * n_axes, vmem_limit_bytes=VMEM_LIMIT_BYTES)


def _rms_scale(x, eps):
    return lax.rsqrt(jnp.mean(x * x, axis=-1, keepdims=True) + eps)


def _pack_bf16_pairs(lo, hi):
    lo_bits = pltpu.bitcast(lo.astype(BF16).astype(F32), U32)
    hi_bits = pltpu.bitcast(hi.astype(BF16).astype(F32), U32)
    return lo_bits | (hi_bits >> 16)


def _unpack_bf16_pairs(packed):
    lo = pltpu.bitcast(packed & jnp.uint32(0xFFFF0000), F32)
    hi = pltpu.bitcast(packed << 16, F32)
    return lo, hi


def _store_token_slabs(ref, rows):
    n, width = rows.shape
    parts = width // LANES
    for j in range(parts):
        ref[pl.ds(j, n, stride=parts), :] = rows[:, j * LANES:(j + 1) * LANES]


def _load_token_slabs(ref, n, parts):
    return jnp.concatenate([ref[pl.ds(j, n, stride=parts), :] for j in range(parts)], axis=1)


def _inproj_kernel(x_ref, g_ref, w_ref, wvt_ref, wf_ref, bf_ref,
                   qf_ref, qd_ref, kf_ref, vf_ref, kd_ref, vd_ref,
                   kf16_ref, kd16_ref, vf16_ref, vd16_ref, logf_ref, *, width, values_transposed):
    x = x_ref[...]
    h = (x * _rms_scale(x, NORM_EPS) * g_ref[...]).astype(BF16)
    q_scale = LOG2E / math.sqrt(HEAD_DIM)

    def proj(c):
        return jnp.dot(h, w_ref[:, c * width:(c + 1) * width], preferred_element_type=F32)

    qf_ref[...] = (proj(0) * q_scale).astype(BF16)
    qd_ref[...] = (proj(1) * q_scale).astype(BF16)
    for c, (o32, o16) in enumerate(((kf_ref, kf16_ref), (vf_ref, vf16_ref),
                                    (kd_ref, kd16_ref), (vd_ref, vd16_ref))):
        u = proj(2 + c)
        if len(o32.shape) == 3:
            o32[...] = pltpu.einshape("m(hd)->mhd", u, h=o32.shape[1])
        else:
            o32[...] = pltpu.einshape("m(hed)->mhed", u, h=o32.shape[1], e=o32.shape[2])
        if not (values_transposed and c in (1, 3)):
            o16[...] = u.astype(BF16)
    if values_transposed:
        vf16_ref[...] = _nt_dot(wvt_ref[:width, :], h).astype(BF16)
        vd16_ref[...] = _nt_dot(wvt_ref[width:, :], h).astype(BF16)
    f = jnp.dot(h, wf_ref[...], preferred_element_type=F32) + bf_ref[...]
    logf_ref[...] = jax.nn.log_sigmoid(f)


def _inproj(x2d, g_mix, w_main, w_vt, w_f, b_f, *, tm, values_transposed):
    n, d = x2d.shape
    width = w_main.shape[1] // 6
    row = lambda i: (i, 0)
    const = lambda i: (0, 0)
    if values_transposed:
        v16_shape, v16_spec = (width, n), pl.BlockSpec((width, tm), lambda i: (0, i))
    else:
        v16_shape, v16_spec = (n, width), pl.BlockSpec((tm, width), row)
    head_shapes = [(width // HEAD_DIM, HEAD_DIM)] * 2 + [
        (width // (2 * HEAD_DIM), 2, HEAD_DIM), (width // (2 * HEAD_DIM), 2 * HEAD_DIM)]
    out_shape = ([jax.ShapeDtypeStruct((n, width), BF16)] * 2
                 + [jax.ShapeDtypeStruct((n,) + s, F32) for s in head_shapes]
                 + [jax.ShapeDtypeStruct((n, width), BF16)] * 2
                 + [jax.ShapeDtypeStruct(v16_shape, BF16)] * 2
                 + [jax.ShapeDtypeStruct((n, LANES), F32)])
    out_specs = ([pl.BlockSpec((tm, width), row)] * 2
                 + [pl.BlockSpec((tm,) + s, lambda i, s=s: (i,) + (0,) * len(s))
                    for s in head_shapes]
                 + [pl.BlockSpec((tm, width), row)] * 2 + [v16_spec] * 2
                 + [pl.BlockSpec((tm, LANES), row)])
    return pl.pallas_call(
        functools.partial(_inproj_kernel, width=width, values_transposed=values_transposed),
        out_shape=out_shape,
        grid=(n // tm,),
        in_specs=[pl.BlockSpec((tm, d), row), pl.BlockSpec((1, d), const),
                  pl.BlockSpec(w_main.shape, const), pl.BlockSpec(w_vt.shape, const),
                  pl.BlockSpec(w_f.shape, const), pl.BlockSpec(b_f.shape, const)],
        out_specs=out_specs,
        compiler_params=_cparams(1),
        name="inproj",
    )(x2d, g_mix, w_main, w_vt, w_f, b_f)


def _cumsum_kernel(x_ref, o_ref):
    x = x_ref[...]
    length = x.shape[1]
    lane = lax.broadcasted_iota(I32, x.shape, 1)
    shift = 1
    while shift < length:
        x = x + jnp.where(lane >= shift, pltpu.roll(x, shift, axis=1), 0.0)
        shift *= 2
    o_ref[...] = x


def _cumsum_lanes(x):
    b, h, length = x.shape
    spec = pl.BlockSpec((None, h, length), lambda i: (i, 0, 0))
    return pl.pallas_call(
        _cumsum_kernel, out_shape=jax.ShapeDtypeStruct(x.shape, F32), grid=(b,),
        in_specs=[spec], out_specs=spec, compiler_params=_cparams(1), name="cumsum",
    )(x)


N_DECAY_TERMS = 3


def _key_decay_kernel(x_ref, sel_ref, o_ref):
    x = x_ref[...]
    length = x.shape[0]
    row = lax.broadcasted_iota(I32, x.shape, 0)
    shift = 1
    while shift < length:
        x = x + jnp.where(row >= shift, pltpu.roll(x, shift, axis=0), 0.0)
        shift *= 2
    rest = x * (-LOG2E)
    out = jnp.zeros(o_ref.shape, F32)
    for j in range(N_DECAY_TERMS):
        term = rest.astype(BF16)
        rest = rest - term.astype(F32)
        out = out + jnp.dot(term, sel_ref[j], preferred_element_type=F32)
    o_ref[...] = out.astype(BF16)


def _decay_selectors(n_heads, width):
    sel = np.zeros((N_DECAY_TERMS, LANES, width), np.float32)
    for h in range(n_heads):
        base = (h // 2) * LANES + (HEAD_DIM if h % 2 == 0 else 0)
        for j in range(N_DECAY_TERMS):
            sel[j, h, base + j] = 1.0
    return jnp.asarray(sel, BF16)


def _key_decay(logf, n_heads, width):
    b, length, lanes = logf.shape
    sel = _decay_selectors(n_heads, width)
    return pl.pallas_call(
        _key_decay_kernel, out_shape=jax.ShapeDtypeStruct((b, length, width), BF16), grid=(b,),
        in_specs=[pl.BlockSpec((None, length, lanes), lambda i: (i, 0, 0)),
                  pl.BlockSpec(sel.shape, lambda i: (0, 0, 0))],
        out_specs=pl.BlockSpec((None, length, width), lambda i: (i, 0, 0)),
        compiler_params=_cparams(1), name="key_decay",
    )(logf, sel)


def _nt_dot(a, b):
    return lax.dot_general(a, b, (((1,), (1,)), ((), ())), preferred_element_type=F32)


SCORE_LOOKAHEAD = 4


def _pipelined(units, scores, consume):
    ahead = [scores(unit) for unit in units[:SCORE_LOOKAHEAD]]
    for n, unit in enumerate(units):
        s = ahead.pop(0)
        if n + SCORE_LOOKAHEAD < len(units):
            ahead.append(scores(units[n + SCORE_LOOKAHEAD]))
        consume(unit, s)


def _memo(fn):
    cache = {}

    def get(key):
        if key not in cache:
            cache[key] = fn(key)
        return cache[key]
    return get


CACHED_CHUNK_GROUP = 4


def _fox_cached_kernel(q_ref, kn_ref, vn_ref, cn_ref, cp_ref, kp_ref, vp_ref,
                       o_ref, acc_a, acc_b, m_a, m_b, *, tkp, n_past):
    t = q_ref.shape[0]
    lo = lax.broadcasted_iota(I32, (1, LANES), 1) < HEAD_DIM
    q = q_ref[...]
    zero = jnp.zeros_like(q)
    q_heads = (jnp.where(lo, q, zero), jnp.where(lo, zero, q))
    accs = (acc_a, acc_b)
    ms = (m_a, m_b)
    for acc, m in zip(accs, ms):
        acc[...] = jnp.zeros_like(acc)
        m[...] = jnp.full_like(m, NEG_INF)

    def run(chunks):
        @_memo
        def operands(n):
            k, v, c_rows, _ = chunks[n]
            v = v.astype(BF16)
            one = jnp.ones_like(v)
            return k.astype(BF16), (jnp.where(lo, v, one), jnp.where(lo, one, v)), c_rows * LOG2E

        def scores(unit):
            n, hd = unit
            return _nt_dot(q_heads[hd], operands(n)[0])

        def consume(unit, s):
            n, hd = unit
            _, v_heads, c2 = operands(n)
            s = s - c2[hd:hd + 1, :]
            if chunks[n][3] is not None:
                s = jnp.where(chunks[n][3], s, NEG_INF)
            m_old = ms[hd][...]
            m_new = jnp.maximum(m_old, jnp.max(s, axis=1, keepdims=True))
            p = jnp.exp2(s - m_new).astype(BF16)
            accs[hd][...] = jnp.exp2(m_old - m_new) * accs[hd][...] + jnp.dot(
                p, v_heads[hd], preferred_element_type=F32)
            ms[hd][...] = m_new

        _pipelined([(n, hd) for n in range(len(chunks)) for hd in range(2)], scores, consume)

    def cached(c):
        off = pl.multiple_of(c * tkp, tkp)
        return (kp_ref[pl.ds(off, tkp), :], vp_ref[pl.ds(off, tkp), :],
                cp_ref[:, pl.ds(off, tkp)], None)

    def past_body(j, carry):
        run([cached(j * CACHED_CHUNK_GROUP + g) for g in range(CACHED_CHUNK_GROUP)])
        return carry
    lax.fori_loop(0, n_past // CACHED_CHUNK_GROUP, past_body, 0)
    causal = lax.broadcasted_iota(I32, (t, t), 1) <= lax.broadcasted_iota(I32, (t, t), 0)
    run([cached(c) for c in range(n_past - n_past % CACHED_CHUNK_GROUP, n_past)]
        + [(kn_ref[...], vn_ref[...], cn_ref[...], causal)])

    a = acc_a[...]
    b = acc_b[...]
    out = jnp.where(lo, a / pltpu.roll(a, HEAD_DIM, axis=1), b / pltpu.roll(b, HEAD_DIM, axis=1))
    o_ref[...] = out.astype(o_ref.dtype)


def _fox_cached_attention(q, k_new, v_new, c_new, k_past, v_past, c_past, *, tkp):
    b, t, w = q.shape
    n_pairs = w // LANES
    plen = k_past.shape[1]
    n_past = plen // tkp
    assert n_past * tkp == plen
    pair_rows = lambda c: c.reshape(b, n_pairs, 2, c.shape[-1])
    seqspec = lambda n: pl.BlockSpec((None, n, LANES), lambda bi, p: (bi, 0, p))
    cspec = lambda n: pl.BlockSpec((None, None, 2, n), lambda bi, p: (bi, p, 0, 0))
    return pl.pallas_call(
        functools.partial(_fox_cached_kernel, tkp=tkp, n_past=n_past),
        out_shape=jax.ShapeDtypeStruct((b, t, w), BF16),
        grid=(b, n_pairs),
        in_specs=[seqspec(t), seqspec(t), seqspec(t), cspec(t), cspec(plen),
                  seqspec(plen), seqspec(plen)],
        out_specs=seqspec(t),
        scratch_shapes=[pltpu.VMEM((t, LANES), F32)] * 2 + [pltpu.VMEM((t, 1), F32)] * 2,
        compiler_params=_cparams(2),
        name="fox_cached_attn",
    )(q, k_new, v_new, pair_rows(c_new), pair_rows(c_past), k_past, v_past)


def _diff_lambda(dl, lambda_init):
    return (jnp.exp(jnp.sum(dl[0:1] * dl[1:2], axis=1, keepdims=True))
            - jnp.exp(jnp.sum(dl[2:3] * dl[3:4], axis=1, keepdims=True)) + lambda_init)


def _diff_cached_kernel(q_ref, kn_ref, vn_ref, bprev_ref, bdiag_ref, lam_ref, g_ref,
                        kp_ref, vp_ref, o_ref, acc0, acc1, m0, m1, l0, l1,
                        *, tkp, n_past, lambda_init):
    lo = lax.broadcasted_iota(I32, (1, LANES), 1) < HEAD_DIM
    q = q_ref[...]
    zero = jnp.zeros_like(q)
    q_maps = (jnp.where(lo, q, zero), jnp.where(lo, zero, q))
    accs, ms, ls = (acc0, acc1), (m0, m1), (l0, l1)
    for acc, m, l in zip(accs, ms, ls):
        acc[...] = jnp.zeros_like(acc)
        l[...] = jnp.zeros_like(l)
        m[...] = jnp.full_like(m, NEG_INF)

    def run(chunks):
        @_memo
        def operands(n):
            return chunks[n][0].astype(BF16), chunks[n][1].astype(BF16)

        def scores(unit):
            n, mp = unit
            return _nt_dot(q_maps[mp], operands(n)[0])

        def consume(unit, s):
            n, mp = unit
            if chunks[n][2] is not None:
                s = jnp.maximum(s, NEG_INF) + chunks[n][2]
            m_old = ms[mp][...]
            m_new = jnp.maximum(m_old, jnp.max(s, axis=1, keepdims=True))
            p = jnp.exp2(s - m_new)
            alpha = jnp.exp2(m_old - m_new)
            ls[mp][...] = alpha * ls[mp][...] + jnp.sum(p, axis=1, keepdims=True)
            accs[mp][...] = alpha * accs[mp][...] + jnp.dot(
                p.astype(BF16), operands(n)[1], preferred_element_type=F32)
            ms[mp][...] = m_new

        _pipelined([(n, mp) for n in range(len(chunks)) for mp in range(2)], scores, consume)

    def cached(c, bias=None):
        off = pl.multiple_of(c * tkp, tkp)
        return kp_ref[pl.ds(off, tkp), :], vp_ref[pl.ds(off, tkp), :], bias

    def past_body(j, carry):
        run([cached(j * CACHED_CHUNK_GROUP + g) for g in range(CACHED_CHUNK_GROUP)])
        return carry
    n_far = n_past - 1
    lax.fori_loop(0, n_far // CACHED_CHUNK_GROUP, past_body, 0)
    run([cached(c) for c in range(n_far - n_far % CACHED_CHUNK_GROUP, n_far)]
        + [cached(n_far, bprev_ref[...]), (kn_ref[...], vn_ref[...], bdiag_ref[...])])

    lam = _diff_lambda(lam_ref[...], lambda_init)
    o = acc0[...] / l0[...] - lam * (acc1[...] / l1[...])
    o = o * _rms_scale(o, SUBLN_EPS) * g_ref[...] * (1.0 - lambda_init)
    o_ref[...] = o.astype(o_ref.dtype)


def _diff_cached_attention(q, k_new, v_new, bias_prev, bias_diag, diff_lambda, subln_g,
                           k_past, v_past, *, tkp, lambda_init):
    b, t, w = q.shape
    plen = k_past.shape[1]
    n_past = plen // tkp
    assert n_past * tkp == plen and tkp >= MAX_DISTANCE and t <= CHUNK and plen % CHUNK == 0
    seqspec = lambda n: pl.BlockSpec((None, n, LANES), lambda bi, h: (bi, 0, h))
    head_tile = lambda arr: pl.BlockSpec((None,) + arr.shape[1:], lambda bi, h: (h, 0, 0))
    const2 = lambda arr: pl.BlockSpec(arr.shape, lambda bi, h: (0, 0))
    return pl.pallas_call(
        functools.partial(_diff_cached_kernel, tkp=tkp, n_past=n_past, lambda_init=lambda_init),
        out_shape=jax.ShapeDtypeStruct((b, t, w), BF16),
        grid=(b, w // LANES),
        in_specs=[seqspec(t), seqspec(t), seqspec(t), head_tile(bias_prev), head_tile(bias_diag),
                  const2(diff_lambda), const2(subln_g), seqspec(plen), seqspec(plen)],
        out_specs=seqspec(t),
        scratch_shapes=[pltpu.VMEM((t, LANES), F32)] * 2 + [pltpu.VMEM((t, 1), F32)] * 4,
        compiler_params=_cparams(2),
        name="diff_cached_attn",
    )(q, k_new, v_new, bias_prev, bias_diag, diff_lambda, subln_g, k_past, v_past)


def _fox_prompt_kernel(q_ref, k_ref, kx_ref, vt_ref, o_ref, acc_a, acc_b, m_a, m_b, *, tq, rs):
    i = pl.program_id(2)
    n_qs = tq // rs
    lane = lax.broadcasted_iota(I32, (1, LANES), 1)
    lo = lane < HEAD_DIM
    top = lax.broadcasted_iota(I32, (LANES, rs), 0) < HEAD_DIM

    q = q_ref[...].astype(F32)
    ones_a = jnp.where(lane < HEAD_DIM + N_DECAY_TERMS, 1.0, 0.0)
    ones_b = jnp.where(lane < N_DECAY_TERMS, 1.0, 0.0)
    q_heads = (jnp.where(lo, q, ones_a).astype(BF16), jnp.where(lo, ones_b, q).astype(BF16))
    accs = (acc_a, acc_b)
    ms = (m_a, m_b)
    for acc, m in zip(accs, ms):
        acc[...] = jnp.zeros_like(acc)
        m[...] = jnp.full_like(m, NEG_INF)

    def chunk_group(first, units):
        @_memo
        def operands(kc):
            off = pl.multiple_of((first + kc) * rs, rs)
            k = k_ref[pl.ds(off, rs), :]
            kx = kx_ref[pl.ds(off, rs), :]
            vt = vt_ref[:, pl.ds(off, rs)]
            ones_v = jnp.ones_like(vt)
            return ((jnp.where(lo, k, kx), jnp.where(lo, kx, k)),
                    (jnp.where(top, vt, ones_v), jnp.where(top, ones_v, vt)))

        def scores(unit):
            kc, hd, qs, _ = unit
            return _nt_dot(operands(kc)[0][hd], q_heads[hd][qs * rs:(qs + 1) * rs])

        def consume(unit, s):
            kc, hd, qs, mask = unit
            cols = pl.ds(qs * rs, rs)
            if mask is not None:
                s = jnp.where(mask, s, NEG_INF)
            m_old = ms[hd][:, cols]
            m_new = jnp.maximum(m_old, jnp.max(s, axis=0, keepdims=True))
            p = jnp.exp2(s - m_new).astype(BF16)
            accs[hd][:, cols] = jnp.exp2(m_old - m_new) * accs[hd][:, cols] + jnp.dot(
                operands(kc)[1][hd], p, preferred_element_type=F32)
            ms[hd][:, cols] = m_new

        _pipelined(units, scores, consume)

    def below(j, carry):
        chunk_group(n_qs * j, [(kc, hd, qs, None) for kc in range(n_qs) for qs in range(n_qs)
                               for hd in range(2)])
        return carry
    lax.fori_loop(0, i, below, 0)
    causal = (lax.broadcasted_iota(I32, (rs, rs), 0) <= lax.broadcasted_iota(I32, (rs, rs), 1))
    chunk_group(n_qs * i, [(kc, hd, qs, causal if qs == kc else None) for kc in range(n_qs)
                           for qs in range(kc, n_qs) for hd in range(2)])

    a = acc_a[...]
    b = acc_b[...]
    out_t = jnp.concatenate([a[:HEAD_DIM] / a[HEAD_DIM:HEAD_DIM + 1], b[HEAD_DIM:] / b[0:1]],
                            axis=0)
    o_ref[...] = out_t.T.astype(o_ref.dtype)


def _fox_prompt_attention(q, k, kx, vt, *, tq, rs):
    b, t, w = q.shape
    qspec = pl.BlockSpec((None, tq, LANES), lambda bi, p, i: (bi, i, p))
    kspec = pl.BlockSpec((None, t, LANES), lambda bi, p, i: (bi, 0, p))
    return pl.pallas_call(
        functools.partial(_fox_prompt_kernel, tq=tq, rs=rs),
        out_shape=jax.ShapeDtypeStruct((b, t, w), BF16),
        grid=(b, w // LANES, t // tq),
        in_specs=[qspec, kspec, kspec, pl.BlockSpec((LANES, t), lambda bi, p, i: (p, bi))],
        out_specs=qspec,
        scratch_shapes=[pltpu.VMEM((LANES, tq), F32), pltpu.VMEM((LANES, tq), F32),
                        pltpu.VMEM((1, tq), F32), pltpu.VMEM((1, tq), F32)],
        compiler_params=_cparams(3),
        name="fox_prompt_attn",
    )(q, k, kx, vt)


ONES_ROWS = 16


def _diff_prompt_kernel(q_ref, k_ref, vt_ref, bprev_ref, bdiag_ref, lam_ref, g_ref, o_ref,
                        acc0, acc1, m0, m1, *, tq, rs, lambda_init):
    i = pl.program_id(2)
    n_qs = tq // rs
    lo = lax.broadcasted_iota(I32, (1, LANES), 1) < HEAD_DIM

    q = q_ref[...]
    zero = jnp.zeros_like(q)
    q_maps = (jnp.where(lo, q, zero), jnp.where(lo, zero, q))
    accs, ms = (acc0, acc1), (m0, m1)
    for acc, m in zip(accs, ms):
        acc[...] = jnp.zeros_like(acc)
        m[...] = jnp.full_like(m, NEG_INF)

    def chunk_group(first, blocks_ahead):
        @_memo
        def operands(kc):
            off = pl.multiple_of((first + kc) * rs, rs)
            return (k_ref[pl.ds(off, rs), :],
                    jnp.concatenate([vt_ref[:, pl.ds(off, rs)], jnp.ones((ONES_ROWS, rs), BF16)],
                                    axis=0))

        def scores(unit):
            kc, mp, qs = unit
            return _nt_dot(operands(kc)[0], q_maps[mp][qs * rs:(qs + 1) * rs])

        def consume(unit, s):
            kc, mp, qs = unit
            cols = pl.ds(qs * rs, rs)
            distance = blocks_ahead + qs - kc
            if distance == 1:
                s = jnp.maximum(s, NEG_INF) + bprev_ref[...]
            elif distance == 0:
                s = jnp.maximum(s, NEG_INF) + bdiag_ref[...]
            m_old = ms[mp][:, cols]
            m_new = jnp.maximum(m_old, jnp.max(s, axis=0, keepdims=True))
            p = jnp.exp2(s - m_new).astype(BF16)
            accs[mp][:, cols] = jnp.exp2(m_old - m_new) * accs[mp][:, cols] + jnp.dot(
                operands(kc)[1], p, preferred_element_type=F32)
            ms[mp][:, cols] = m_new

        _pipelined([(kc, mp, qs) for kc in range(n_qs) for qs in range(n_qs) for mp in range(2)
                    if blocks_ahead + qs - kc >= 0], scores, consume)

    def far_below(j, carry):
        chunk_group(n_qs * j, 2 * n_qs)
        return carry
    lax.fori_loop(0, jnp.maximum(i - 1, 0), far_below, 0)

    @pl.when(i >= 1)
    def _():
        chunk_group(n_qs * (i - 1), n_qs)

    chunk_group(n_qs * i, 0)

    lam = _diff_lambda(lam_ref[...], lambda_init)
    a0 = acc0[...]
    a1 = acc1[...]
    o = a0[:LANES] / a0[LANES:LANES + 1] - lam * (a1[:LANES] / a1[LANES:LANES + 1])
    scale = lax.rsqrt(jnp.mean(o * o, axis=0, keepdims=True) + SUBLN_EPS)
    o = o * scale * g_ref[...] * (1.0 - lambda_init)
    o_ref[...] = o.T.astype(o_ref.dtype)


def _diff_prompt_attention(q, k, vt, bias_prev, bias_diag, diff_lambda, subln_g, *,
                           tq, rs, lambda_init):
    b, t, w = q.shape
    assert rs >= MAX_DISTANCE
    qspec = pl.BlockSpec((None, tq, LANES), lambda bi, h, i: (bi, i, h))
    head_tile = lambda arr: pl.BlockSpec((None,) + arr.shape[1:], lambda bi, h, i: (h, 0, 0))
    const2 = lambda arr: pl.BlockSpec(arr.shape, lambda bi, h, i: (0, 0))
    return pl.pallas_call(
        functools.partial(_diff_prompt_kernel, tq=tq, rs=rs, lambda_init=lambda_init),
        out_shape=jax.ShapeDtypeStruct((b, t, w), BF16),
        grid=(b, w // LANES, t // tq),
        in_specs=[qspec, pl.BlockSpec((None, t, LANES), lambda bi, h, i: (bi, 0, h)),
                  pl.BlockSpec((LANES, t), lambda bi, h, i: (h, bi)),
                  head_tile(bias_prev), head_tile(bias_diag), const2(diff_lambda),
                  const2(subln_g)],
        out_specs=qspec,
        scratch_shapes=[pltpu.VMEM((LANES + ONES_ROWS, tq), F32)] * 2
        + [pltpu.VMEM((1, tq), F32)] * 2,
        compiler_params=_cparams(3),
        name="diff_prompt_attn",
    )(q, k, vt, bias_prev, bias_diag, diff_lambda, subln_g)


def _t5_bucket(rel):
    nb = NUM_BUCKETS // 2
    max_exact = nb // 2
    ret = jnp.where(rel > 0, nb, 0)
    n = jnp.abs(rel)
    nf = jnp.maximum(n, max_exact).astype(F32)
    large = max_exact + (jnp.log(nf / max_exact) / math.log(MAX_DISTANCE / max_exact)
                         * (nb - max_exact)).astype(I32)
    large = jnp.minimum(large, nb - 1)
    return ret + jnp.where(n < max_exact, n, large)


def _bias_tiles(rel_table, tq, tk_prev):
    buckets = jnp.arange(NUM_BUCKETS, dtype=I32)

    def lookup(rel):
        hit = _t5_bucket(rel)[None, :] == buckets[:, None]
        return jnp.sum(jnp.where(hit[:, None, :], rel_table[:, :, None], 0.0), axis=0)

    far = lookup(jnp.full((1,), -4 * MAX_DISTANCE, I32))

    def tile(rel0, tk):
        period = tq + tk
        j = jnp.arange(period, dtype=I32)
        strip = lookup(rel0 + jnp.where(j < tk, j, j - period)) - far
        flat = jnp.tile(strip, (1, tq))[:, :tq * (period - 1)]
        return (flat.reshape(-1, tq, period - 1)[:, :, :tk] * LOG2E).astype(F32)

    return tile(-tk_prev, tk_prev), tile(0, tq)


def _router_kernel(x_ref, of_ref, od_ref, wof_ref, wod_ref, g_ref, wr_ref, br_ref,
                   x1_ref, hn_ref, te_ref, tw_ref, rank_ref, cnt_ref, *, tm):
    step = pl.program_id(0)

    @pl.when(step == 0)
    def _():
        cnt_ref[...] = jnp.zeros_like(cnt_ref)

    x1 = (x_ref[...] + jnp.dot(of_ref[...], wof_ref[...], preferred_element_type=F32)
          + jnp.dot(od_ref[...], wod_ref[...], preferred_element_type=F32))
    x1_ref[...] = x1
    hn = x1 * _rms_scale(x1, NORM_EPS) * g_ref[...]
    half = hn.shape[1] // 2
    _store_token_slabs(hn_ref, _pack_bf16_pairs(hn[:, :half], hn[:, half:]))

    hn_hi = hn.astype(BF16)
    hn_lo = (hn - hn_hi.astype(F32)).astype(BF16)
    both = jnp.dot(hn_hi, wr_ref[...], preferred_element_type=F32)
    logits = (both[:, :LANES] + both[:, LANES:] + br_ref[...]
              + jnp.dot(hn_lo, wr_ref[:, :LANES], preferred_element_type=F32))
    lane = lax.broadcasted_iota(I32, logits.shape, 1)
    lane_f = lane.astype(F32)
    sel = jnp.zeros(logits.shape, F32)
    vals, idxs = [], []
    for _ in range(TOP_K):
        mx = jnp.max(logits, axis=1, keepdims=True)
        idx = jnp.min(jnp.where(logits == mx, lane_f, float(LANES)), axis=1, keepdims=True)
        hit = lane_f == idx
        sel = jnp.where(hit, 1.0, sel)
        logits = jnp.where(hit, -jnp.inf, logits)
        vals.append(mx)
        idxs.append(idx)
    exps = [jnp.exp(v - vals[0]) for v in vals]
    denom = exps[0] + exps[1] + exps[2] + exps[3]

    earlier = (lax.broadcasted_iota(I32, (tm, tm), 1)
               < lax.broadcasted_iota(I32, (tm, tm), 0)).astype(BF16)
    before = cnt_ref[...] + jnp.dot(earlier, sel.astype(BF16), preferred_element_type=F32)
    for k in range(TOP_K):
        te_ref[:, k:k + 1] = idxs[k].astype(I32)
        tw_ref[:, k:k + 1] = exps[k] / denom
        rank_ref[:, k:k + 1] = jnp.sum(
            jnp.where(lane_f == idxs[k], before, 0.0), axis=1, keepdims=True).astype(I32)
    cnt_ref[...] += jnp.sum(sel, axis=0, keepdims=True)


def _router(x2d, o_fox, o_diff, wo_f, wo_d, g_ffn, w_r, b_r, *, tm):
    n, d = x2d.shape
    row = lambda i: (i, 0)
    const = lambda i: (0, 0)
    full = lambda arr: pl.BlockSpec(arr.shape, const)
    parts = d // 2 // LANES
    return pl.pallas_call(
        functools.partial(_router_kernel, tm=tm),
        out_shape=[jax.ShapeDtypeStruct((n, d), F32), jax.ShapeDtypeStruct((n * parts, LANES), U32),
                   jax.ShapeDtypeStruct((n, TOP_K), I32), jax.ShapeDtypeStruct((n, TOP_K), F32),
                   jax.ShapeDtypeStruct((n, TOP_K), I32), jax.ShapeDtypeStruct((1, LANES), F32)],
        grid=(n // tm,),
        in_specs=[pl.BlockSpec((tm, d), row), pl.BlockSpec((tm, o_fox.shape[1]), row),
                  pl.BlockSpec((tm, o_diff.shape[1]), row), full(wo_f), full(wo_d),
                  full(g_ffn), full(w_r), full(b_r)],
        out_specs=[pl.BlockSpec((tm, d), row), pl.BlockSpec((tm * parts, LANES), row),
                   pl.BlockSpec((tm, TOP_K), row), pl.BlockSpec((tm, TOP_K), row),
                   pl.BlockSpec((tm, TOP_K), row), pl.BlockSpec((1, LANES), const)],
        compiler_params=_cparams(1),
        name="router",
    )(x2d, o_fox, o_diff, wo_f, wo_d, g_ffn, w_r, b_r)


def _row_copy(src, src_row, dst, dst_row, sem, parts):
    return pltpu.make_async_copy(src.at[pl.ds(src_row * parts, parts), :],
                                 dst.at[pl.ds(dst_row * parts, parts), :], sem)


def _dispatch_kernel(tail_start_ref, tail_valid_ref, pos_ref, hn_ref, xs_hbm, zeros_ref, sem,
                     *, tm, tme, n_experts):
    parts = hn_ref.shape[0] // tm
    slab = tme * parts

    def tail_copy(e):
        return pltpu.make_async_copy(
            zeros_ref, xs_hbm.at[pl.ds(pl.multiple_of(tail_start_ref[e] * parts, slab), slab), :],
            sem.at[0])

    @pl.when(pl.program_id(0) == 0)
    def _():
        zeros_ref[...] = jnp.zeros_like(zeros_ref)
        for e in range(n_experts):
            @pl.when(tail_valid_ref[e] > 0)
            def _():
                tail_copy(e).start()
        for e in range(n_experts):
            @pl.when(tail_valid_ref[e] > 0)
            def _():
                tail_copy(e).wait()

    def start_row(r, carry):
        for k in range(TOP_K):
            _row_copy(hn_ref, r, xs_hbm, pos_ref[r, k], sem.at[1], parts).start(priority=k % 2)
        return carry

    def wait_row(r, carry):
        for k in range(TOP_K):
            _row_copy(hn_ref, r, xs_hbm, pos_ref[r, k], sem.at[1], parts).wait()
        return carry

    lax.fori_loop(0, tm, start_row, 0, unroll=True)
    lax.fori_loop(0, tm, wait_row, 0, unroll=8)


def _dispatch(hn_slabs, pos, tail_start, tail_valid, *, m_pad, tm, tme):
    n = pos.shape[0]
    parts = hn_slabs.shape[0] // n
    n_experts = tail_start.shape[0]
    grid_spec = pltpu.PrefetchScalarGridSpec(
        num_scalar_prefetch=2,
        grid=(n // tm,),
        in_specs=[pl.BlockSpec((tm, TOP_K), lambda i, *_: (i, 0), memory_space=pltpu.SMEM),
                  pl.BlockSpec((tm * parts, LANES), lambda i, *_: (i, 0))],
        out_specs=pl.BlockSpec(memory_space=pl.ANY),
        scratch_shapes=[pltpu.VMEM((tme * parts, LANES), U32), pltpu.SemaphoreType.DMA((2,))],
    )
    return pl.pallas_call(
        functools.partial(_dispatch_kernel, tm=tm, tme=tme, n_experts=n_experts),
        out_shape=jax.ShapeDtypeStruct((m_pad * parts, LANES), U32),
        grid_spec=grid_spec,
        compiler_params=pltpu.CompilerParams(
            dimension_semantics=("arbitrary",), vmem_limit_bytes=VMEM_LIMIT_BYTES,
            has_side_effects=True),
        name="dispatch",
    )(tail_start, tail_valid, pos, hn_slabs)


def _experts_kernel(te_ref, nact_ref, xs_ref, wg_ref, bg_ref, wu_ref, bu_ref, wd_ref, bd_ref,
                    ys_ref, *, tme):
    @pl.when(pl.program_id(0) < nact_ref[0])
    def _():
        parts = xs_ref.shape[0] // tme
        half = parts * LANES
        x_lo, x_hi = _unpack_bf16_pairs(_load_token_slabs(xs_ref, tme, parts))
        x_lo = x_lo.astype(BF16)
        x_hi = x_hi.astype(BF16)

        def proj(w_ref, b_ref):
            return (jnp.dot(x_lo, w_ref[:half, :], preferred_element_type=F32)
                    + jnp.dot(x_hi, w_ref[half:, :], preferred_element_type=F32) + b_ref[...])

        g = jnp.minimum(proj(wg_ref, bg_ref), SWIGLU_LIMIT)
        u = jnp.clip(proj(wu_ref, bu_ref), -SWIGLU_LIMIT, SWIGLU_LIMIT)
        a = ((u + 1.0) * (g * jax.nn.sigmoid(SWIGLU_ALPHA * g))).astype(BF16)
        y = jnp.dot(a, wd_ref[...], preferred_element_type=F32) + bd_ref[...]
        _store_token_slabs(ys_ref, _pack_bf16_pairs(y[:, :half], y[:, half:]))


def _experts(xs, tile_expert, n_active, w_gate, b_gate, w_up, b_up, w_down, b_down, *, tme):
    d, d_ff = w_gate.shape[1:]
    parts = d // 2 // LANES
    n_tiles = xs.shape[0] // (tme * parts)
    row = lambda t, te, na: (jnp.minimum(t, na[0] - 1), 0)
    wspec = lambda shape: pl.BlockSpec((None,) + shape, lambda t, te, na: (te[t], 0, 0))
    grid_spec = pltpu.PrefetchScalarGridSpec(
        num_scalar_prefetch=2,
        grid=(n_tiles,),
        in_specs=[pl.BlockSpec((tme * parts, LANES), row),
                  wspec((d, d_ff)), wspec((1, d_ff)), wspec((d, d_ff)), wspec((1, d_ff)),
                  wspec((d_ff, d)), wspec((1, d))],
        out_specs=pl.BlockSpec((tme * parts, LANES), row),
    )
    return pl.pallas_call(
        functools.partial(_experts_kernel, tme=tme),
        out_shape=jax.ShapeDtypeStruct(xs.shape, U32),
        grid_spec=grid_spec,
        compiler_params=_cparams(1),
        name="experts",
    )(tile_expert, n_active, xs, w_gate, b_gate, w_up, b_up, w_down, b_down)


def _combine_kernel(pos_ref, pos_next_ref, x1_ref, tw_ref, p_ref, wg_ref, wp_ref, gple_ref,
                    gfin_ref, ys_hbm, y_ref, gbuf, sem, *, tm):
    step = pl.program_id(0)
    slot = step % 2
    parts = gbuf.shape[2] // tm

    def gather(tile_pos_ref, tile_slot, wait):
        def row(r, carry):
            for k in range(TOP_K):
                copy = _row_copy(ys_hbm, tile_pos_ref[r, k], gbuf.at[tile_slot, k], r,
                                 sem.at[tile_slot], parts)
                if wait:
                    copy.wait()
                else:
                    copy.start(priority=k % 2)
            return carry
        lax.fori_loop(0, tm, row, 0, unroll=8)

    @pl.when(step == 0)
    def _():
        gather(pos_ref, 0, wait=False)

    @pl.when(step + 1 < pl.num_programs(0))
    def _():
        gather(pos_next_ref, 1 - slot, wait=False)

    gather(pos_ref, slot, wait=True)

    x1 = x1_ref[...]
    half = x1.shape[1] // 2
    tw = tw_ref[...]
    moe_lo = jnp.zeros((tm, half), F32)
    moe_hi = jnp.zeros((tm, half), F32)
    for k in range(TOP_K):
        lo, hi = _unpack_bf16_pairs(_load_token_slabs(gbuf.at[slot, k], tm, parts))
        moe_lo = moe_lo + tw[:, k:k + 1] * lo
        moe_hi = moe_hi + tw[:, k:k + 1] * hi
    x2 = x1 + jnp.concatenate([moe_lo, moe_hi], axis=1)
    hp = (x2 * _rms_scale(x2, NORM_EPS) * gple_ref[...]).astype(BF16)
    gate = jax.nn.sigmoid(jnp.dot(hp, wg_ref[...], preferred_element_type=F32))
    proj = jnp.dot(p_ref[...].astype(BF16), wp_ref[...], preferred_element_type=F32)
    x3 = x2 + proj * gate
    y_ref[...] = x3 * _rms_scale(x3, NORM_EPS) * gfin_ref[...]


def _combine(x1, top_w, pos, ys, p2d, w_ple_gate, w_ple_proj, g_ple, g_final, *, tm):
    n, d = x1.shape
    parts = d // 2 // LANES
    row = lambda i: (i, 0)
    const = lambda i: (0, 0)
    full = lambda arr: pl.BlockSpec(arr.shape, const)
    return pl.pallas_call(
        functools.partial(_combine_kernel, tm=tm),
        out_shape=jax.ShapeDtypeStruct((n, d), F32),
        grid=(n // tm,),
        in_specs=[pl.BlockSpec((tm, TOP_K), row, memory_space=pltpu.SMEM),
                  pl.BlockSpec((tm, TOP_K), lambda i: (jnp.minimum(i + 1, n // tm - 1), 0),
                               memory_space=pltpu.SMEM),
                  pl.BlockSpec((tm, d), row), pl.BlockSpec((tm, TOP_K), row),
                  pl.BlockSpec((tm, p2d.shape[1]), row), full(w_ple_gate), full(w_ple_proj),
                  full(g_ple), full(g_final), pl.BlockSpec(memory_space=pl.ANY)],
        out_specs=pl.BlockSpec((tm, d), row),
        scratch_shapes=[pltpu.VMEM((2, TOP_K, tm * parts, LANES), U32),
                        pltpu.SemaphoreType.DMA((2,))],
        compiler_params=_cparams(1),
        name="combine",
    )(pos, pos, x1, top_w, p2d, w_ple_gate, w_ple_proj, g_ple, g_final, ys)


def _pick_tile(n, want):
    t = min(n, want)
    assert n % t == 0, (n, t)
    return t


def _layer_group(x, p, past, wts, lambda_init, g_final):
    (g_mix, w_main, w_vt, w_f, b_f, diff_lambda, subln_g, rel_table, wo_f, wo_d, g_ffn, w_r, b_r,
     w_gate, b_gate, w_up, b_up, w_down, b_down, g_ple, w_ple_gate, w_ple_proj) = wts
    b, t, d = x.shape
    n = b * t
    n_fox = (w_main.shape[1] // 6) // HEAD_DIM
    n_experts = w_gate.shape[0]
    x2d = x.reshape(n, d)

    tm = _pick_tile(n, 512)
    (qf, qd, kf, vf, kd, vd, kf16, kd16, vf16, vd16, logf_lanes) = _inproj(
        x2d, g_mix, w_main, w_vt, w_f, b_f, tm=tm, values_transposed=past is None)
    logf = logf_lanes[:, :n_fox]
    width = qf.shape[1]
    as_seq = lambda a: a.reshape(b, t, a.shape[-1])

    tq = _pick_tile(t, 2048)
    rs = _pick_tile(tq, 256)
    if past is None:
        kx = _key_decay(as_seq(logf_lanes), n_fox, width)
        o_fox = _fox_prompt_attention(as_seq(qf), as_seq(kf16), kx, vf16, tq=tq, rs=rs)
        bias_prev, bias_diag = _bias_tiles(rel_table, rs, rs)
        block = jnp.arange(rs, dtype=I32) // CHUNK
        bias_diag = jnp.where(block[:, None] <= block[None, :], jnp.swapaxes(bias_diag, 1, 2),
                              NEG_INF)
        o_diff = _diff_prompt_attention(
            as_seq(qd), as_seq(kd16), vd16, jnp.swapaxes(bias_prev, 1, 2), bias_diag,
            diff_lambda, subln_g.reshape(-1, 1), tq=tq, rs=rs, lambda_init=lambda_init)
    else:
        pk_f, pv_f, plogf, pk_d, pv_d = past
        plen = pk_f.shape[1]
        tkp = 512
        total = plen + t
        padded = -(-total // LANES) * LANES
        seq = jnp.concatenate([jnp.transpose(plogf.astype(F32), (0, 2, 1)),
                               jnp.transpose(as_seq(logf), (0, 2, 1))], axis=2)
        c_all = _cumsum_lanes(jnp.pad(seq, ((0, 0), (0, 0), (0, padded - total))))
        flat = lambda a: a.reshape(b, plen, width)
        o_fox = _fox_cached_attention(as_seq(qf), as_seq(kf16), as_seq(vf16),
                                      c_all[:, :, plen:total], flat(pk_f), flat(pv_f),
                                      c_all[:, :, :plen], tkp=tkp)
        bias_prev, bias_diag = _bias_tiles(rel_table, t, tkp)
        o_diff = _diff_cached_attention(as_seq(qd), as_seq(kd16), as_seq(vd16), bias_prev,
                                        bias_diag, diff_lambda, subln_g, flat(pk_d), flat(pv_d),
                                        tkp=tkp, lambda_init=lambda_init)

    tmr = _pick_tile(n, 512)
    x1, hn_slabs, top_e, top_w, rank, counts = _router(
        x2d, o_fox.reshape(n, width), o_diff.reshape(n, width), wo_f, wo_d, g_ffn, w_r, b_r, tm=tmr)

    tme = 512 if n * TOP_K >= 64 * 512 else 256
    n_tiles = -(-(n * TOP_K) // tme) + n_experts
    m_pad = n_tiles * tme
    cnt = counts[0, :n_experts].astype(I32)
    group = -(-cnt // tme) * tme
    ends = jnp.cumsum(group)
    starts = ends - group
    experts = jnp.arange(n_experts, dtype=I32)
    pos = rank + jnp.sum(jnp.where(top_e[:, :, None] == experts, starts, 0), axis=2)
    n_active = (ends[-1] // tme).astype(I32)
    tile_ids = jnp.arange(n_tiles, dtype=I32)
    tile_expert = jnp.sum(jnp.minimum(tile_ids, n_active - 1)[:, None] >= (ends // tme)[None, :],
                          axis=1).astype(I32)
    tile_expert = jnp.minimum(tile_expert, n_experts - 1)
    tail_start = jnp.maximum(ends - tme, 0).astype(I32)
    tail_valid = (group > 0).astype(I32)

    xs = _dispatch(hn_slabs, pos, tail_start, tail_valid, m_pad=m_pad, tm=_pick_tile(n, 512),
                   tme=tme)
    ys = _experts(xs, tile_expert, n_active.reshape(1), w_gate, b_gate, w_up, b_up, w_down, b_down,
                  tme=tme)
    y = _combine(x1, top_w, pos, ys, p.reshape(n, p.shape[-1]), w_ple_gate, w_ple_proj, g_ple,
                 g_final, tm=_pick_tile(n, 256))
    return y.reshape(b, t, d), (kf, vf, logf, kd, vd)


def kernel(x_prompt, x_sample, p_prompt, p_sample, cache_fox_k, cache_fox_v, cache_fox_logf, cache_diff_k, cache_diff_v, g_mix, w_in, b_forget, diff_lambda, diff_subln_g, rel_bias_table, w_o, g_ffn, w_router, b_router, w_gate, b_gate, w_up, b_up, w_down, b_down, g_ple, w_ple_gate, w_ple_proj, g_final):
    depth = g_mix.shape[0]
    assert depth == 1, "the final norm is fused into the last layer; one layer supported"
    d = x_prompt.shape[-1]
    n_fox = cache_fox_k.shape[3]
    fox_w = n_fox * cache_fox_k.shape[4]
    n_diff = cache_diff_k.shape[3]
    diff_w = n_diff * 2 * cache_diff_k.shape[5]
    assert fox_w == diff_w and fox_w % LANES == 0
    n_experts = w_router.shape[-1]
    assert TOP_K <= n_experts <= LANES
    row2 = lambda a: a.reshape(1, -1).astype(F32)

    i = 0
    lambda_init = 0.8 - 0.6 * math.exp(-0.3 * i)
    off_fk, off_fv, off_ff = fox_w, 2 * fox_w, 3 * fox_w
    off_dq = off_ff + n_fox
    off_dk, off_dv = off_dq + diff_w, off_dq + 2 * diff_w
    w = w_in[i]
    cols = lambda o, wd: w[:, o:o + wd]
    w_main = jnp.concatenate([cols(0, fox_w), cols(off_dq, diff_w), cols(off_fk, fox_w),
                              cols(off_fv, fox_w), cols(off_dk, diff_w), cols(off_dv, diff_w)],
                             axis=1).astype(BF16)
    w_vt = jnp.concatenate([cols(off_fv, fox_w), cols(off_dv, diff_w)], axis=1).T.astype(BF16)
    w_f = jnp.pad(cols(off_ff, n_fox), ((0, 0), (0, LANES - n_fox))).astype(BF16)
    b_f = jnp.pad(row2(b_forget[i]), ((0, 0), (0, LANES - n_fox)))
    w_r = jnp.pad(w_router[i].astype(F32), ((0, 0), (0, LANES - n_experts)))
    w_r_hi = w_r.astype(BF16)
    w_r = jnp.concatenate([w_r_hi, (w_r - w_r_hi.astype(F32)).astype(BF16)], axis=1)
    b_r = jnp.pad(row2(b_router[i]), ((0, 0), (0, LANES - n_experts)), constant_values=NEG_INF)
    wo = w_o[i].astype(BF16)
    bias3 = lambda a: a.reshape(n_experts, 1, -1).astype(F32)
    wts = (row2(g_mix[i]), w_main, w_vt, w_f, b_f, diff_lambda[i].astype(F32), row2(diff_subln_g[i]),
           rel_bias_table.astype(F32), wo[:fox_w], wo[fox_w:], row2(g_ffn[i]), w_r, b_r,
           w_gate[i].astype(BF16), bias3(b_gate[i]), w_up[i].astype(BF16), bias3(b_up[i]),
           w_down[i].astype(BF16), bias3(b_down[i]), row2(g_ple[i]),
           w_ple_gate[i].astype(BF16), w_ple_proj[i].astype(BF16))
    gf = row2(g_final)

    yp, new_p = _layer_group(x_prompt, p_prompt[i], None, wts, lambda_init, gf)
    past_s = (cache_fox_k[i], cache_fox_v[i], cache_fox_logf[i], cache_diff_k[i], cache_diff_v[i])
    ys, new_s = _layer_group(x_sample, p_sample[i], past_s, wts, lambda_init, gf)

    def rows(new, b, t):
        kf, vf, logf, kd, vd = new
        return (kf.reshape(1, b, t, n_fox, -1), vf.reshape(1, b, t, n_fox, -1),
                logf.reshape(1, b, t, n_fox), kd.reshape(1, b, t, n_diff, 2, -1),
                vd.reshape(1, b, t, n_diff, -1))

    bp, tp = x_prompt.shape[:2]
    bs, ts = x_sample.shape[:2]
    return (yp, ys) + rows(new_p, bp, tp) + rows(new_s, bs, ts)
```

```python
import functools
import math

import jax
import jax.numpy as jnp
import numpy as np
from jax import lax
from jax.experimental import pallas as pl
from jax.experimental.pallas import tpu as pltpu

HEAD_DIM = 64
CHUNK = 64
NUM_BUCKETS = 32
MAX_DISTANCE = 128
TOP_K = 4
SWIGLU_LIMIT = 7.0
SWIGLU_ALPHA = 1.702
NORM_EPS = 1e-6
SUBLN_EPS = 1e-5
NEG_INF = -1e30
LOG2E = math.log2(math.e)

LANES = 128
VMEM_LIMIT_BYTES = 56 * 1024 * 1024

F32 = jnp.float32
BF16 = jnp.bfloat16
I32 = jnp.int32
U32 = jnp.uint32


def _cparams(n_axes):
    return pltpu.CompilerParams(
        dimension_semantics=("arbitrary",) * n_axes, vmem_limit_bytes=VMEM_LIMIT_BYTES)


def _rms_scale(x, eps):
    return lax.rsqrt(jnp.mean(x * x, axis=-1, keepdims=True) + eps)


def _pack_bf16_pairs(lo, hi):
    lo_bits = pltpu.bitcast(lo.astype(BF16).astype(F32), U32)
    hi_bits = pltpu.bitcast(hi.astype(BF16).astype(F32), U32)
    return lo_bits | (hi_bits >> 16)


def _unpack_bf16_pairs(packed):
    lo = pltpu.bitcast(packed & jnp.uint32(0xFFFF0000), F32)
    hi = pltpu.bitcast(packed << 16, F32)
    return lo, hi


def _store_token_slabs(ref, rows):
    n, width = rows.shape
    parts = width // LANES
    for j in range(parts):
        ref[pl.ds(j, n, stride=parts), :] = rows[:, j * LANES:(j + 1) * LANES]


def _load_token_slabs(ref, n, parts):
    return jnp.concatenate([ref[pl.ds(j, n, stride=parts), :] for j in range(parts)], axis=1)


def _inproj_kernel(x_ref, g_ref, w_ref, wvt_ref, wf_ref, bf_ref,
                   qf_ref, qd_ref, kf_ref, vf_ref, kd_ref, vd_ref,
                   kf16_ref, kd16_ref, vf16_ref, vd16_ref, logf_ref, *, width, values_transposed):
    x = x_ref[...]
    h = (x * _rms_scale(x, NORM_EPS) * g_ref[...]).astype(BF16)
    q_scale = LOG2E / math.sqrt(HEAD_DIM)

    def proj(c):
        return jnp.dot(h, w_ref[:, c * width:(c + 1) * width], preferred_element_type=F32)

    qf_ref[...] = (proj(0) * q_scale).astype(BF16)
    qd_ref[...] = (proj(1) * q_scale).astype(BF16)
    for c, (o32, o16) in enumerate(((kf_ref, kf16_ref), (vf_ref, vf16_ref),
                                    (kd_ref, kd16_ref), (vd_ref, vd16_ref))):
        u = proj(2 + c)
        if len(o32.shape) == 3:
            o32[...] = pltpu.einshape("m(hd)->mhd", u, h=o32.shape[1])
        else:
            o32[...] = pltpu.einshape("m(hed)->mhed", u, h=o32.shape[1], e=o32.shape[2])
        if not (values_transposed and c in (1, 3)):
            o16[...] = u.astype(BF16)
    if values_transposed:
        vf16_ref[...] = _nt_dot(wvt_ref[:width, :], h).astype(BF16)
        vd16_ref[...] = _nt_dot(wvt_ref[width:, :], h).astype(BF16)
    f = jnp.dot(h, wf_ref[...], preferred_element_type=F32) + bf_ref[...]
    logf_ref[...] = jax.nn.log_sigmoid(f)


def _inproj(x2d, g_mix, w_main, w_vt, w_f, b_f, *, tm, values_transposed):
    n, d = x2d.shape
    width = w_main.shape[1] // 6
    row = lambda i: (i, 0)
    const = lambda i: (0, 0)
    if values_transposed:
        v16_shape, v16_spec = (width, n), pl.BlockSpec((width, tm), lambda i: (0, i))
    else:
        v16_shape, v16_spec = (n, width), pl.BlockSpec((tm, width), row)
    head_shapes = [(width // HEAD_DIM, HEAD_DIM)] * 2 + [
        (width // (2 * HEAD_DIM), 2, HEAD_DIM), (width // (2 * HEAD_DIM), 2 * HEAD_DIM)]
    out_shape = ([jax.ShapeDtypeStruct((n, width), BF16)] * 2
                 + [jax.ShapeDtypeStruct((n,) + s, F32) for s in head_shapes]
                 + [jax.ShapeDtypeStruct((n, width), BF16)] * 2
                 + [jax.ShapeDtypeStruct(v16_shape, BF16)] * 2
                 + [jax.ShapeDtypeStruct((n, LANES), F32)])
    out_specs = ([pl.BlockSpec((tm, width), row)] * 2
                 + [pl.BlockSpec((tm,) + s, lambda i, s=s: (i,) + (0,) * len(s))
                    for s in head_shapes]
                 + [pl.BlockSpec((tm, width), row)] * 2 + [v16_spec] * 2
                 + [pl.BlockSpec((tm, LANES), row)])
    return pl.pallas_call(
        functools.partial(_inproj_kernel, width=width, values_transposed=values_transposed),
        out_shape=out_shape,
        grid=(n // tm,),
        in_specs=[pl.BlockSpec((tm, d), row), pl.BlockSpec((1, d), const),
                  pl.BlockSpec(w_main.shape, const), pl.BlockSpec(w_vt.shape, const),
                  pl.BlockSpec(w_f.shape, const), pl.BlockSpec(b_f.shape, const)],
        out_specs=out_specs,
        compiler_params=_cparams(1),
        name="inproj",
    )(x2d, g_mix, w_main, w_vt, w_f, b_f)


def _cumsum_kernel(x_ref, o_ref):
    x = x_ref[...]
    length = x.shape[1]
    lane = lax.broadcasted_iota(I32, x.shape, 1)
    shift = 1
    while shift < length:
        x = x + jnp.where(lane >= shift, pltpu.roll(x, shift, axis=1), 0.0)
        shift *= 2
    o_ref[...] = x


def _cumsum_lanes(x):
    b, h, length = x.shape
    spec = pl.BlockSpec((None, h, length), lambda i: (i, 0, 0))
    return pl.pallas_call(
        _cumsum_kernel, out_shape=jax.ShapeDtypeStruct(x.shape, F32), grid=(b,),
        in_specs=[spec], out_specs=spec, compiler_params=_cparams(1), name="cumsum",
    )(x)


N_DECAY_TERMS = 3


def _key_decay_kernel(x_ref, sel_ref, o_ref):
    x = x_ref[...]
    length = x.shape[0]
    row = lax.broadcasted_iota(I32, x.shape, 0)
    shift = 1
    while shift < length:
        x = x + jnp.where(row >= shift, pltpu.roll(x, shift, axis=0), 0.0)
        shift *= 2
    rest = x * (-LOG2E)
    out = jnp.zeros(o_ref.shape, F32)
    for j in range(N_DECAY_TERMS):
        term = rest.astype(BF16)
        rest = rest - term.astype(F32)
        out = out + jnp.dot(term, sel_ref[j], preferred_element_type=F32)
    o_ref[...] = out.astype(BF16)


def _decay_selectors(n_heads, width):
    sel = np.zeros((N_DECAY_TERMS, LANES, width), np.float32)
    for h in range(n_heads):
        base = (h // 2) * LANES + (HEAD_DIM if h % 2 == 0 else 0)
        for j in range(N_DECAY_TERMS):
            sel[j, h, base + j] = 1.0
    return jnp.asarray(sel, BF16)


def _key_decay(logf, n_heads, width):
    b, length, lanes = logf.shape
    sel = _decay_selectors(n_heads, width)
    return pl.pallas_call(
        _key_decay_kernel, out_shape=jax.ShapeDtypeStruct((b, length, width), BF16), grid=(b,),
        in_specs=[pl.BlockSpec((None, length, lanes), lambda i: (i, 0, 0)),
                  pl.BlockSpec(sel.shape, lambda i: (0, 0, 0))],
        out_specs=pl.BlockSpec((None, length, width), lambda i: (i, 0, 0)),
        compiler_params=_cparams(1), name="key_decay",
    )(logf, sel)


def _nt_dot(a, b):
    return lax.dot_general(a, b, (((1,), (1,)), ((), ())), preferred_element_type=F32)


SCORE_LOOKAHEAD = 4


def _pipelined(units, scores, consume):
    ahead = [scores(unit) for unit in units[:SCORE_LOOKAHEAD]]
    for n, unit in enumerate(units):
        s = ahead.pop(0)
        if n + SCORE_LOOKAHEAD < len(units):
            ahead.append(scores(units[n + SCORE_LOOKAHEAD]))
        consume(unit, s)


def _memo(fn):
    cache = {}

    def get(key):
        if key not in cache:
            cache[key] = fn(key)
        return cache[key]
    return get


CACHED_CHUNK_GROUP = 4


def _fox_cached_kernel(q_ref, kn_ref, vn_ref, cn_ref, cp_ref, kp_ref, vp_ref,
                       o_ref, acc_a, acc_b, m_a, m_b, *, tkp, n_past):
    t = q_ref.shape[0]
    lo = lax.broadcasted_iota(I32, (1, LANES), 1) < HEAD_DIM
    q = q_ref[...]
    zero = jnp.zeros_like(q)
    q_heads = (jnp.where(lo, q, zero), jnp.where(lo, zero, q))
    accs = (acc_a, acc_b)
    ms = (m_a, m_b)
    for acc, m in zip(accs, ms):
        acc[...] = jnp.zeros_like(acc)
        m[...] = jnp.full_like(m, NEG_INF)

    def run(chunks):
        @_memo
        def operands(n):
            k, v, c_rows, _ = chunks[n]
            v = v.astype(BF16)
            one = jnp.ones_like(v)
            return k.astype(BF16), (jnp.where(lo, v, one), jnp.where(lo, one, v)), c_rows * LOG2E

        def scores(unit):
            n, hd = unit
            return _nt_dot(q_heads[hd], operands(n)[0])

        def consume(unit, s):
            n, hd = unit
            _, v_heads, c2 = operands(n)
            s = s - c2[hd:hd + 1, :]
            if chunks[n][3] is not None:
                s = jnp.where(chunks[n][3], s, NEG_INF)
            m_old = ms[hd][...]
            m_new = jnp.maximum(m_old, jnp.max(s, axis=1, keepdims=True))
            p = jnp.exp2(s - m_new).astype(BF16)
            accs[hd][...] = jnp.exp2(m_old - m_new) * accs[hd][...] + jnp.dot(
                p, v_heads[hd], preferred_element_type=F32)
            ms[hd][...] = m_new

        _pipelined([(n, hd) for n in range(len(chunks)) for hd in range(2)], scores, consume)

    def cached(c):
        off = pl.multiple_of(c * tkp, tkp)
        return (kp_ref[pl.ds(off, tkp), :], vp_ref[pl.ds(off, tkp), :],
                cp_ref[:, pl.ds(off, tkp)], None)

    def past_body(j, carry):
        run([cached(j * CACHED_CHUNK_GROUP + g) for g in range(CACHED_CHUNK_GROUP)])
        return carry
    lax.fori_loop(0, n_past // CACHED_CHUNK_GROUP, past_body, 0)
    causal = lax.broadcasted_iota(I32, (t, t), 1) <= lax.broadcasted_iota(I32, (t, t), 0)
    run([cached(c) for c in range(n_past - n_past % CACHED_CHUNK_GROUP, n_past)]
        + [(kn_ref[...], vn_ref[...], cn_ref[...], causal)])

    a = acc_a[...]
    b = acc_b[...]
    out = jnp.where(lo, a / pltpu.roll(a, HEAD_DIM, axis=1), b / pltpu.roll(b, HEAD_DIM, axis=1))
    o_ref[...] = out.astype(o_ref.dtype)


def _fox_cached_attention(q, k_new, v_new, c_new, k_past, v_past, c_past, *, tkp):
    b, t, w = q.shape
    n_pairs = w // LANES
    plen = k_past.shape[1]
    n_past = plen // tkp
    assert n_past * tkp == plen
    pair_rows = lambda c: c.reshape(b, n_pairs, 2, c.shape[-1])
    seqspec = lambda n: pl.BlockSpec((None, n, LANES), lambda bi, p: (bi, 0, p))
    cspec = lambda n: pl.BlockSpec((None, None, 2, n), lambda bi, p: (bi, p, 0, 0))
    return pl.pallas_call(
        functools.partial(_fox_cached_kernel, tkp=tkp, n_past=n_past),
        out_shape=jax.ShapeDtypeStruct((b, t, w), BF16),
        grid=(b, n_pairs),
        in_specs=[seqspec(t), seqspec(t), seqspec(t), cspec(t), cspec(plen),
                  seqspec(plen), seqspec(plen)],
        out_specs=seqspec(t),
        scratch_shapes=[pltpu.VMEM((t, LANES), F32)] * 2 + [pltpu.VMEM((t, 1), F32)] * 2,
        compiler_params=_cparams(2),
        name="fox_cached_attn",
    )(q, k_new, v_new, pair_rows(c_new), pair_rows(c_past), k_past, v_past)


def _diff_lambda(dl, lambda_init):
    return (jnp.exp(jnp.sum(dl[0:1] * dl[1:2], axis=1, keepdims=True))
            - jnp.exp(jnp.sum(dl[2:3] * dl[3:4], axis=1, keepdims=True)) + lambda_init)


def _diff_cached_kernel(q_ref, kn_ref, vn_ref, bprev_ref, bdiag_ref, lam_ref, g_ref,
                        kp_ref, vp_ref, o_ref, acc0, acc1, m0, m1, l0, l1,
                        *, tkp, n_past, lambda_init):
    lo = lax.broadcasted_iota(I32, (1, LANES), 1) < HEAD_DIM
    q = q_ref[...]
    zero = jnp.zeros_like(q)
    q_maps = (jnp.where(lo, q, zero), jnp.where(lo, zero, q))
    accs, ms, ls = (acc0, acc1), (m0, m1), (l0, l1)
    for acc, m, l in zip(accs, ms, ls):
        acc[...] = jnp.zeros_like(acc)
        l[...] = jnp.zeros_like(l)
        m[...] = jnp.full_like(m, NEG_INF)

    def run(chunks):
        @_memo
        def operands(n):
            return chunks[n][0].astype(BF16), chunks[n][1].astype(BF16)

        def scores(unit):
            n, mp = unit
            return _nt_dot(q_maps[mp], operands(n)[0])

        def consume(unit, s):
            n, mp = unit
            if chunks[n][2] is not None:
                s = jnp.maximum(s, NEG_INF) + chunks[n][2]
            m_old = ms[mp][...]
            m_new = jnp.maximum(m_old, jnp.max(s, axis=1, keepdims=True))
            p = jnp.exp2(s - m_new)
            alpha = jnp.exp2(m_old - m_new)
            ls[mp][...] = alpha * ls[mp][...] + jnp.sum(p, axis=1, keepdims=True)
            accs[mp][...] = alpha * accs[mp][...] + jnp.dot(
                p.astype(BF16), operands(n)[1], preferred_element_type=F32)
            ms[mp][...] = m_new

        _pipelined([(n, mp) for n in range(len(chunks)) for mp in range(2)], scores, consume)

    def cached(c, bias=None):
        off = pl.multiple_of(c * tkp, tkp)
        return kp_ref[pl.ds(off, tkp), :], vp_ref[pl.ds(off, tkp), :], bias

    def past_body(j, carry):
        run([cached(j * CACHED_CHUNK_GROUP + g) for g in range(CACHED_CHUNK_GROUP)])
        return carry
    n_far = n_past - 1
    lax.fori_loop(0, n_far // CACHED_CHUNK_GROUP, past_body, 0)
    run([cached(c) for c in range(n_far - n_far % CACHED_CHUNK_GROUP, n_far)]
        + [cached(n_far, bprev_ref[...]), (kn_ref[...], vn_ref[...], bdiag_ref[...])])

    lam = _diff_lambda(lam_ref[...], lambda_init)
    o = acc0[...] / l0[...] - lam * (acc1[...] / l1[...])
    o = o * _rms_scale(o, SUBLN_EPS) * g_ref[...] * (1.0 - lambda_init)
    o_ref[...] = o.astype(o_ref.dtype)


def _diff_cached_attention(q, k_new, v_new, bias_prev, bias_diag, diff_lambda, subln_g,
                           k_past, v_past, *, tkp, lambda_init):
    b, t, w = q.shape
    plen = k_past.shape[1]
    n_past = plen // tkp
    assert n_past * tkp == plen and tkp >= MAX_DISTANCE and t <= CHUNK and plen % CHUNK == 0
    seqspec = lambda n: pl.BlockSpec((None, n, LANES), lambda bi, h: (bi, 0, h))
    head_tile = lambda arr: pl.BlockSpec((None,) + arr.shape[1:], lambda bi, h: (h, 0, 0))
    const2 = lambda arr: pl.BlockSpec(arr.shape, lambda bi, h: (0, 0))
    return pl.pallas_call(
        functools.partial(_diff_cached_kernel, tkp=tkp, n_past=n_past, lambda_init=lambda_init),
        out_shape=jax.ShapeDtypeStruct((b, t, w), BF16),
        grid=(b, w // LANES),
        in_specs=[seqspec(t), seqspec(t), seqspec(t), head_tile(bias_prev), head_tile(bias_diag),
                  const2(diff_lambda), const2(subln_g), seqspec(plen), seqspec(plen)],
        out_specs=seqspec(t),
        scratch_shapes=[pltpu.VMEM((t, LANES), F32)] * 2 + [pltpu.VMEM((t, 1), F32)] * 4,
        compiler_params=_cparams(2),
        name="diff_cached_attn",
    )(q, k_new, v_new, bias_prev, bias_diag, diff_lambda, subln_g, k_past, v_past)


def _fox_prompt_kernel(q_ref, k_ref, kx_ref, vt_ref, o_ref, acc_a, acc_b, m_a, m_b, *, tq, rs):
    i = pl.program_id(2)
    n_qs = tq // rs
    lane = lax.broadcasted_iota(I32, (1, LANES), 1)
    lo = lane < HEAD_DIM
    top = lax.broadcasted_iota(I32, (LANES, rs), 0) < HEAD_DIM

    q = q_ref[...].astype(F32)
    ones_a = jnp.where(lane < HEAD_DIM + N_DECAY_TERMS, 1.0, 0.0)
    ones_b = jnp.where(lane < N_DECAY_TERMS, 1.0, 0.0)
    q_heads = (jnp.where(lo, q, ones_a).astype(BF16), jnp.where(lo, ones_b, q).astype(BF16))
    accs = (acc_a, acc_b)
    ms = (m_a, m_b)
    for acc, m in zip(accs, ms):
        acc[...] = jnp.zeros_like(acc)
        m[...] = jnp.full_like(m, NEG_INF)

    def chunk_group(first, units):
        @_memo
        def operands(kc):
            off = pl.multiple_of((first + kc) * rs, rs)
            k = k_ref[pl.ds(off, rs), :]
            kx = kx_ref[pl.ds(off, rs), :]
            vt = vt_ref[:, pl.ds(off, rs)]
            ones_v = jnp.ones_like(vt)
            return ((jnp.where(lo, k, kx), jnp.where(lo, kx, k)),
                    (jnp.where(top, vt, ones_v), jnp.where(top, ones_v, vt)))

        def scores(unit):
            kc, hd, qs, _ = unit
            return _nt_dot(operands(kc)[0][hd], q_heads[hd][qs * rs:(qs + 1) * rs])

        def consume(unit, s):
            kc, hd, qs, mask = unit
            cols = pl.ds(qs * rs, rs)
            if mask is not None:
                s = jnp.where(mask, s, NEG_INF)
            m_old = ms[hd][:, cols]
            m_new = jnp.maximum(m_old, jnp.max(s, axis=0, keepdims=True))
            p = jnp.exp2(s - m_new).astype(BF16)
            accs[hd][:, cols] = jnp.exp2(m_old - m_new) * accs[hd][:, cols] + jnp.dot(
                operands(kc)[1][hd], p, preferred_element_type=F32)
            ms[hd][:, cols] = m_new

        _pipelined(units, scores, consume)

    def below(j, carry):
        chunk_group(n_qs * j, [(kc, hd, qs, None) for kc in range(n_qs) for qs in range(n_qs)
                               for hd in range(2)])
        return carry
    lax.fori_loop(0, i, below, 0)
    causal = (lax.broadcasted_iota(I32, (rs, rs), 0) <= lax.broadcasted_iota(I32, (rs, rs), 1))
    chunk_group(n_qs * i, [(kc, hd, qs, causal if qs == kc else None) for kc in range(n_qs)
                           for qs in range(kc, n_qs) for hd in range(2)])

    a = acc_a[...]
    b = acc_b[...]
    out_t = jnp.concatenate([a[:HEAD_DIM] / a[HEAD_DIM:HEAD_DIM + 1], b[HEAD_DIM:] / b[0:1]],
                            axis=0)
    o_ref[...] = out_t.T.astype(o_ref.dtype)


def _fox_prompt_attention(q, k, kx, vt, *, tq, rs):
    b, t, w = q.shape
    qspec = pl.BlockSpec((None, tq, LANES), lambda bi, p, i: (bi, i, p))
    kspec = pl.BlockSpec((None, t, LANES), lambda bi, p, i: (bi, 0, p))
    return pl.pallas_call(
        functools.partial(_fox_prompt_kernel, tq=tq, rs=rs),
        out_shape=jax.ShapeDtypeStruct((b, t, w), BF16),
        grid=(b, w // LANES, t // tq),
        in_specs=[qspec, kspec, kspec, pl.BlockSpec((LANES, t), lambda bi, p, i: (p, bi))],
        out_specs=qspec,
        scratch_shapes=[pltpu.VMEM((LANES, tq), F32), pltpu.VMEM((LANES, tq), F32),
                        pltpu.VMEM((1, tq), F32), pltpu.VMEM((1, tq), F32)],
        compiler_params=_cparams(3),
        name="fox_prompt_attn",
    )(q, k, kx, vt)


ONES_ROWS = 16


def _diff_prompt_kernel(q_ref, k_ref, vt_ref, bprev_ref, bdiag_ref, lam_ref, g_ref, o_ref,
                        acc0, acc1, m0, m1, *, tq, rs, lambda_init):
    i = pl.program_id(2)
    n_qs = tq // rs
    lo = lax.broadcasted_iota(I32, (1, LANES), 1) < HEAD_DIM

    q = q_ref[...]
    zero = jnp.zeros_like(q)
    q_maps = (jnp.where(lo, q, zero), jnp.where(lo, zero, q))
    accs, ms = (acc0, acc1), (m0, m1)
    for acc, m in zip(accs, ms):
        acc[...] = jnp.zeros_like(acc)
        m[...] = jnp.full_like(m, NEG_INF)

    def chunk_group(first, blocks_ahead):
        @_memo
        def operands(kc):
            off = pl.multiple_of((first + kc) * rs, rs)
            return (k_ref[pl.ds(off, rs), :],
                    jnp.concatenate([vt_ref[:, pl.ds(off, rs)], jnp.ones((ONES_ROWS, rs), BF16)],
                                    axis=0))

        def scores(unit):
            kc, mp, qs = unit
            return _nt_dot(operands(kc)[0], q_maps[mp][qs * rs:(qs + 1) * rs])

        def consume(unit, s):
            kc, mp, qs = unit
            cols = pl.ds(qs * rs, rs)
            distance = blocks_ahead + qs - kc
            if distance == 1:
                s = jnp.maximum(s, NEG_INF) + bprev_ref[...]
            elif distance == 0:
                s = jnp.maximum(s, NEG_INF) + bdiag_ref[...]
            m_old = ms[mp][:, cols]
            m_new = jnp.maximum(m_old, jnp.max(s, axis=0, keepdims=True))
            p = jnp.exp2(s - m_new).astype(BF16)
            accs[mp][:, cols] = jnp.exp2(m_old - m_new) * accs[mp][:, cols] + jnp.dot(
                operands(kc)[1], p, preferred_element_type=F32)
            ms[mp][:, cols] = m_new

        _pipelined([(kc, mp, qs) for kc in range(n_qs) for qs in range(n_qs) for mp in range(2)
                    if blocks_ahead + qs - kc >= 0], scores, consume)

    def far_below(j, carry):
        chunk_group(n_qs * j, 2 * n_qs)
        return carry
    lax.fori_loop(0, jnp.maximum(i - 1, 0), far_below, 0)

    @pl.when(i >= 1)
    def _():
        chunk_group(n_qs * (i - 1), n_qs)

    chunk_group(n_qs * i, 0)

    lam = _diff_lambda(lam_ref[...], lambda_init)
    a0 = acc0[...]
    a1 = acc1[...]
    o = a0[:LANES] / a0[LANES:LANES + 1] - lam * (a1[:LANES] / a1[LANES:LANES + 1])
    scale = lax.rsqrt(jnp.mean(o * o, axis=0, keepdims=True) + SUBLN_EPS)
    o = o * scale * g_ref[...] * (1.0 - lambda_init)
    o_ref[...] = o.T.astype(o_ref.dtype)


def _diff_prompt_attention(q, k, vt, bias_prev, bias_diag, diff_lambda, subln_g, *,
                           tq, rs, lambda_init):
    b, t, w = q.shape
    assert rs >= MAX_DISTANCE
    qspec = pl.BlockSpec((None, tq, LANES), lambda bi, h, i: (bi, i, h))
    head_tile = lambda arr: pl.BlockSpec((None,) + arr.shape[1:], lambda bi, h, i: (h, 0, 0))
    const2 = lambda arr: pl.BlockSpec(arr.shape, lambda bi, h, i: (0, 0))
    return pl.pallas_call(
        functools.partial(_diff_prompt_kernel, tq=tq, rs=rs, lambda_init=lambda_init),
        out_shape=jax.ShapeDtypeStruct((b, t, w), BF16),
        grid=(b, w // LANES, t // tq),
        in_specs=[qspec, pl.BlockSpec((None, t, LANES), lambda bi, h, i: (bi, 0, h)),
                  pl.BlockSpec((LANES, t), lambda bi, h, i: (h, bi)),
                  head_tile(bias_prev), head_tile(bias_diag), const2(diff_lambda),
                  const2(subln_g)],
        out_specs=qspec,
        scratch_shapes=[pltpu.VMEM((LANES + ONES_ROWS, tq), F32)] * 2
        + [pltpu.VMEM((1, tq), F32)] * 2,
        compiler_params=_cparams(3),
        name="diff_prompt_attn",
    )(q, k, vt, bias_prev, bias_diag, diff_lambda, subln_g)


def _t5_bucket(rel):
    nb = NUM_BUCKETS // 2
    max_exact = nb // 2
    ret = jnp.where(rel > 0, nb, 0)
    n = jnp.abs(rel)
    nf = jnp.maximum(n, max_exact).astype(F32)
    large = max_exact + (jnp.log(nf / max_exact) / math.log(MAX_DISTANCE / max_exact)
                         * (nb - max_exact)).astype(I32)
    large = jnp.minimum(large, nb - 1)
    return ret + jnp.where(n < max_exact, n, large)


def _bias_tiles(rel_table, tq, tk_prev):
    buckets = jnp.arange(NUM_BUCKETS, dtype=I32)

    def lookup(rel):
        hit = _t5_bucket(rel)[None, :] == buckets[:, None]
        return jnp.sum(jnp.where(hit[:, None, :], rel_table[:, :, None], 0.0), axis=0)

    far = lookup(jnp.full((1,), -4 * MAX_DISTANCE, I32))

    def tile(rel0, tk):
        period = tq + tk
        j = jnp.arange(period, dtype=I32)
        strip = lookup(rel0 + jnp.where(j < tk, j, j - period)) - far
        flat = jnp.tile(strip, (1, tq))[:, :tq * (period - 1)]
        return (flat.reshape(-1, tq, period - 1)[:, :, :tk] * LOG2E).astype(F32)

    return tile(-tk_prev, tk_prev), tile(0, tq)


def _router_kernel(x_ref, of_ref, od_ref, wof_ref, wod_ref, g_ref, wr_ref, br_ref,
                   x1_ref, hn_ref, te_ref, tw_ref, rank_ref, cnt_ref, *, tm):
    step = pl.program_id(0)

    @pl.when(step == 0)
    def _():
        cnt_ref[...] = jnp.zeros_like(cnt_ref)

    x1 = (x_ref[...] + jnp.dot(of_ref[...], wof_ref[...], preferred_element_type=F32)
          + jnp.dot(od_ref[...], wod_ref[...], preferred_element_type=F32))
    x1_ref[...] = x1
    hn = x1 * _rms_scale(x1, NORM_EPS) * g_ref[...]
    half = hn.shape[1] // 2
    _store_token_slabs(hn_ref, _pack_bf16_pairs(hn[:, :half], hn[:, half:]))

    hn_hi = hn.astype(BF16)
    hn_lo = (hn - hn_hi.astype(F32)).astype(BF16)
    both = jnp.dot(hn_hi, wr_ref[...], preferred_element_type=F32)
    logits = (both[:, :LANES] + both[:, LANES:] + br_ref[...]
              + jnp.dot(hn_lo, wr_ref[:, :LANES], preferred_element_type=F32))
    lane = lax.broadcasted_iota(I32, logits.shape, 1)
    lane_f = lane.astype(F32)
    sel = jnp.zeros(logits.shape, F32)
    vals, idxs = [], []
    for _ in range(TOP_K):
        mx = jnp.max(logits, axis=1, keepdims=True)
        idx = jnp.min(jnp.where(logits == mx, lane_f, float(LANES)), axis=1, keepdims=True)
        hit = lane_f == idx
        sel = jnp.where(hit, 1.0, sel)
        logits = jnp.where(hit, -jnp.inf, logits)
        vals.append(mx)
        idxs.append(idx)
    exps = [jnp.exp(v - vals[0]) for v in vals]
    denom = exps[0] + exps[1] + exps[2] + exps[3]

    earlier = (lax.broadcasted_iota(I32, (tm, tm), 1)
               < lax.broadcasted_iota(I32, (tm, tm), 0)).astype(BF16)
    before = cnt_ref[...] + jnp.dot(earlier, sel.astype(BF16), preferred_element_type=F32)
    for k in range(TOP_K):
        te_ref[:, k:k + 1] = idxs[k].astype(I32)
        tw_ref[:, k:k + 1] = exps[k] / denom
        rank_ref[:, k:k + 1] = jnp.sum(
            jnp.where(lane_f == idxs[k], before, 0.0), axis=1, keepdims=True).astype(I32)
    cnt_ref[...] += jnp.sum(sel, axis=0, keepdims=True)


def _router(x2d, o_fox, o_diff, wo_f, wo_d, g_ffn, w_r, b_r, *, tm):
    n, d = x2d.shape
    row = lambda i: (i, 0)
    const = lambda i: (0, 0)
    full = lambda arr: pl.BlockSpec(arr.shape, const)
    parts = d // 2 // LANES
    return pl.pallas_call(
        functools.partial(_router_kernel, tm=tm),
        out_shape=[jax.ShapeDtypeStruct((n, d), F32), jax.ShapeDtypeStruct((n * parts, LANES), U32),
                   jax.ShapeDtypeStruct((n, TOP_K), I32), jax.ShapeDtypeStruct((n, TOP_K), F32),
                   jax.ShapeDtypeStruct((n, TOP_K), I32), jax.ShapeDtypeStruct((1, LANES), F32)],
        grid=(n // tm,),
        in_specs=[pl.BlockSpec((tm, d), row), pl.BlockSpec((tm, o_fox.shape[1]), row),
                  pl.BlockSpec((tm, o_diff.shape[1]), row), full(wo_f), full(wo_d),
                  full(g_ffn), full(w_r), full(b_r)],
        out_specs=[pl.BlockSpec((tm, d), row), pl.BlockSpec((tm * parts, LANES), row),
                   pl.BlockSpec((tm, TOP_K), row), pl.BlockSpec((tm, TOP_K), row),
                   pl.BlockSpec((tm, TOP_K), row), pl.BlockSpec((1, LANES), const)],
        compiler_params=_cparams(1),
        name="router",
    )(x2d, o_fox, o_diff, wo_f, wo_d, g_ffn, w_r, b_r)


def _row_copy(src, src_row, dst, dst_row, sem, parts):
    return pltpu.make_async_copy(src.at[pl.ds(src_row * parts, parts), :],
                                 dst.at[pl.ds(dst_row * parts, parts), :], sem)


def _dispatch_kernel(tail_start_ref, tail_valid_ref, pos_ref, hn_ref, xs_hbm, zeros_ref, sem,
                     *, tm, tme, n_experts):
    parts = hn_ref.shape[0] // tm
    slab = tme * parts

    def tail_copy(e):
        return pltpu.make_async_copy(
            zeros_ref, xs_hbm.at[pl.ds(pl.multiple_of(tail_start_ref[e] * parts, slab), slab), :],
            sem.at[0])

    @pl.when(pl.program_id(0) == 0)
    def _():
        zeros_ref[...] = jnp.zeros_like(zeros_ref)
        for e in range(n_experts):
            @pl.when(tail_valid_ref[e] > 0)
            def _():
                tail_copy(e).start()
        for e in range(n_experts):
            @pl.when(tail_valid_ref[e] > 0)
            def _():
                tail_copy(e).wait()

    def start_row(r, carry):
        for k in range(TOP_K):
            _row_copy(hn_ref, r, xs_hbm, pos_ref[r, k], sem.at[1], parts).start(priority=k % 2)
        return carry

    def wait_row(r, carry):
        for k in range(TOP_K):
            _row_copy(hn_ref, r, xs_hbm, pos_ref[r, k], sem.at[1], parts).wait()
        return carry

    lax.fori_loop(0, tm, start_row, 0, unroll=True)
    lax.fori_loop(0, tm, wait_row, 0, unroll=8)


def _dispatch(hn_slabs, pos, tail_start, tail_valid, *, m_pad, tm, tme):
    n = pos.shape[0]
    parts = hn_slabs.shape[0] // n
    n_experts = tail_start.shape[0]
    grid_spec = pltpu.PrefetchScalarGridSpec(
        num_scalar_prefetch=2,
        grid=(n // tm,),
        in_specs=[pl.BlockSpec((tm, TOP_K), lambda i, *_: (i, 0), memory_space=pltpu.SMEM),
                  pl.BlockSpec((tm * parts, LANES), lambda i, *_: (i, 0))],
        out_specs=pl.BlockSpec(memory_space=pl.ANY),
        scratch_shapes=[pltpu.VMEM((tme * parts, LANES), U32), pltpu.SemaphoreType.DMA((2,))],
    )
    return pl.pallas_call(
        functools.partial(_dispatch_kernel, tm=tm, tme=tme, n_experts=n_experts),
        out_shape=jax.ShapeDtypeStruct((m_pad * parts, LANES), U32),
        grid_spec=grid_spec,
        compiler_params=pltpu.CompilerParams(
            dimension_semantics=("arbitrary",), vmem_limit_bytes=VMEM_LIMIT_BYTES,
            has_side_effects=True),
        name="dispatch",
    )(tail_start, tail_valid, pos, hn_slabs)


def _experts_kernel(te_ref, nact_ref, xs_ref, wg_ref, bg_ref, wu_ref, bu_ref, wd_ref, bd_ref,
                    ys_ref, *, tme):
    @pl.when(pl.program_id(0) < nact_ref[0])
    def _():
        parts = xs_ref.shape[0] // tme
        half = parts * LANES
        x_lo, x_hi = _unpack_bf16_pairs(_load_token_slabs(xs_ref, tme, parts))
        x_lo = x_lo.astype(BF16)
        x_hi = x_hi.astype(BF16)

        def proj(w_ref, b_ref):
            return (jnp.dot(x_lo, w_ref[:half, :], preferred_element_type=F32)
                    + jnp.dot(x_hi, w_ref[half:, :], preferred_element_type=F32) + b_ref[...])

        g = jnp.minimum(proj(wg_ref, bg_ref), SWIGLU_LIMIT)
        u = jnp.clip(proj(wu_ref, bu_ref), -SWIGLU_LIMIT, SWIGLU_LIMIT)
        a = ((u + 1.0) * (g * jax.nn.sigmoid(SWIGLU_ALPHA * g))).astype(BF16)
        y = jnp.dot(a, wd_ref[...], preferred_element_type=F32) + bd_ref[...]
        _store_token_slabs(ys_ref, _pack_bf16_pairs(y[:, :half], y[:, half:]))


def _experts(xs, tile_expert, n_active, w_gate, b_gate, w_up, b_up, w_down, b_down, *, tme):
    d, d_ff = w_gate.shape[1:]
    parts = d // 2 // LANES
    n_tiles = xs.shape[0] // (tme * parts)
    row = lambda t, te, na: (jnp.minimum(t, na[0] - 1), 0)
    wspec = lambda shape: pl.BlockSpec((None,) + shape, lambda t, te, na: (te[t], 0, 0))
    grid_spec = pltpu.PrefetchScalarGridSpec(
        num_scalar_prefetch=2,
        grid=(n_tiles,),
        in_specs=[pl.BlockSpec((tme * parts, LANES), row),
                  wspec((d, d_ff)), wspec((1, d_ff)), wspec((d, d_ff)), wspec((1, d_ff)),
                  wspec((d_ff, d)), wspec((1, d))],
        out_specs=pl.BlockSpec((tme * parts, LANES), row),
    )
    return pl.pallas_call(
        functools.partial(_experts_kernel, tme=tme),
        out_shape=jax.ShapeDtypeStruct(xs.shape, U32),
        grid_spec=grid_spec,
        compiler_params=_cparams(1),
        name="experts",
    )(tile_expert, n_active, xs, w_gate, b_gate, w_up, b_up, w_down, b_down)


def _combine_kernel(pos_ref, pos_next_ref, x1_ref, tw_ref, p_ref, wg_ref, wp_ref, gple_ref,
                    gfin_ref, ys_hbm, y_ref, gbuf, sem, *, tm):
    step = pl.program_id(0)
    slot = step % 2
    parts = gbuf.shape[2] // tm

    def gather(tile_pos_ref, tile_slot, wait):
        def row(r, carry):
            for k in range(TOP_K):
                copy = _row_copy(ys_hbm, tile_pos_ref[r, k], gbuf.at[tile_slot, k], r,
                                 sem.at[tile_slot], parts)
                if wait:
                    copy.wait()
                else:
                    copy.start(priority=k % 2)
            return carry
        lax.fori_loop(0, tm, row, 0, unroll=8)

    @pl.when(step == 0)
    def _():
        gather(pos_ref, 0, wait=False)

    @pl.when(step + 1 < pl.num_programs(0))
    def _():
        gather(pos_next_ref, 1 - slot, wait=False)

    gather(pos_ref, slot, wait=True)

    x1 = x1_ref[...]
    half = x1.shape[1] // 2
    tw = tw_ref[...]
    moe_lo = jnp.zeros((tm, half), F32)
    moe_hi = jnp.zeros((tm, half), F32)
    for k in range(TOP_K):
        lo, hi = _unpack_bf16_pairs(_load_token_slabs(gbuf.at[slot, k], tm, parts))
        moe_lo = moe_lo + tw[:, k:k + 1] * lo
        moe_hi = moe_hi + tw[:, k:k + 1] * hi
    x2 = x1 + jnp.concatenate([moe_lo, moe_hi], axis=1)
    hp = (x2 * _rms_scale(x2, NORM_EPS) * gple_ref[...]).astype(BF16)
    gate = jax.nn.sigmoid(jnp.dot(hp, wg_ref[...], preferred_element_type=F32))
    proj = jnp.dot(p_ref[...].astype(BF16), wp_ref[...], preferred_element_type=F32)
    x3 = x2 + proj * gate
    y_ref[...] = x3 * _rms_scale(x3, NORM_EPS) * gfin_ref[...]


def _combine(x1, top_w, pos, ys, p2d, w_ple_gate, w_ple_proj, g_ple, g_final, *, tm):
    n, d = x1.shape
    parts = d // 2 // LANES
    row = lambda i: (i, 0)
    const = lambda i: (0, 0)
    full = lambda arr: pl.BlockSpec(arr.shape, const)
    return pl.pallas_call(
        functools.partial(_combine_kernel, tm=tm),
        out_shape=jax.ShapeDtypeStruct((n, d), F32),
        grid=(n // tm,),
        in_specs=[pl.BlockSpec((tm, TOP_K), row, memory_space=pltpu.SMEM),
                  pl.BlockSpec((tm, TOP_K), lambda i: (jnp.minimum(i + 1, n // tm - 1), 0),
                               memory_space=pltpu.SMEM),
                  pl.BlockSpec((tm, d), row), pl.BlockSpec((tm, TOP_K), row),
                  pl.BlockSpec((tm, p2d.shape[1]), row), full(w_ple_gate), full(w_ple_proj),
                  full(g_ple), full(g_final), pl.BlockSpec(memory_space=pl.ANY)],
        out_specs=pl.BlockSpec((tm, d), row),
        scratch_shapes=[pltpu.VMEM((2, TOP_K, tm * parts, LANES), U32),
                        pltpu.SemaphoreType.DMA((2,))],
        compiler_params=_cparams(1),
        name="combine",
    )(pos, pos, x1, top_w, p2d, w_ple_gate, w_ple_proj, g_ple, g_final, ys)


def _pick_tile(n, want):
    t = min(n, want)
    assert n % t == 0, (n, t)
    return t


def _layer_group(x, p, past, wts, lambda_init, g_final):
    (g_mix, w_main, w_vt, w_f, b_f, diff_lambda, subln_g, rel_table, wo_f, wo_d, g_ffn, w_r, b_r,
     w_gate, b_gate, w_up, b_up, w_down, b_down, g_ple, w_ple_gate, w_ple_proj) = wts
    b, t, d = x.shape
    n = b * t
    n_fox = (w_main.shape[1] // 6) // HEAD_DIM
    n_experts = w_gate.shape[0]
    x2d = x.reshape(n, d)

    tm = _pick_tile(n, 512)
    (qf, qd, kf, vf, kd, vd, kf16, kd16, vf16, vd16, logf_lanes) = _inproj(
        x2d, g_mix, w_main, w_vt, w_f, b_f, tm=tm, values_transposed=past is None)
    logf = logf_lanes[:, :n_fox]
    width = qf.shape[1]
    as_seq = lambda a: a.reshape(b, t, a.shape[-1])

    tq = _pick_tile(t, 2048)
    rs = _pick_tile(tq, 256)
    if past is None:
        kx = _key_decay(as_seq(logf_lanes), n_fox, width)
        o_fox = _fox_prompt_attention(as_seq(qf), as_seq(kf16), kx, vf16, tq=tq, rs=rs)
        bias_prev, bias_diag = _bias_tiles(rel_table, rs, rs)
        block = jnp.arange(rs, dtype=I32) // CHUNK
        bias_diag = jnp.where(block[:, None] <= block[None, :], jnp.swapaxes(bias_diag, 1, 2),
                              NEG_INF)
        o_diff = _diff_prompt_attention(
            as_seq(qd), as_seq(kd16), vd16, jnp.swapaxes(bias_prev, 1, 2), bias_diag,
            diff_lambda, subln_g.reshape(-1, 1), tq=tq, rs=rs, lambda_init=lambda_init)
    else:
        pk_f, pv_f, plogf, pk_d, pv_d = past
        plen = pk_f.shape[1]
        tkp = 512
        total = plen + t
        padded = -(-total // LANES) * LANES
        seq = jnp.concatenate([jnp.transpose(plogf.astype(F32), (0, 2, 1)),
                               jnp.transpose(as_seq(logf), (0, 2, 1))], axis=2)
        c_all = _cumsum_lanes(jnp.pad(seq, ((0, 0), (0, 0), (0, padded - total))))
        flat = lambda a: a.reshape(b, plen, width)
        o_fox = _fox_cached_attention(as_seq(qf), as_seq(kf16), as_seq(vf16),
                                      c_all[:, :, plen:total], flat(pk_f), flat(pv_f),
                                      c_all[:, :, :plen], tkp=tkp)
        bias_prev, bias_diag = _bias_tiles(rel_table, t, tkp)
        o_diff = _diff_cached_attention(as_seq(qd), as_seq(kd16), as_seq(vd16), bias_prev,
                                        bias_diag, diff_lambda, subln_g, flat(pk_d), flat(pv_d),
                                        tkp=tkp, lambda_init=lambda_init)

    tmr = _pick_tile(n, 1024)
    x1, hn_slabs, top_e, top_w, rank, counts = _router(
        x2d, o_fox.reshape(n, width), o_diff.reshape(n, width), wo_f, wo_d, g_ffn, w_r, b_r, tm=tmr)

    tme = 1024 if n * TOP_K >= 64 * 1024 else 256
    n_tiles = -(-(n * TOP_K) // tme) + n_experts
    m_pad = n_tiles * tme
    cnt = counts[0, :n_experts].astype(I32)
    group = -(-cnt // tme) * tme
    ends = jnp.cumsum(group)
    starts = ends - group
    experts = jnp.arange(n_experts, dtype=I32)
    pos = rank + jnp.sum(jnp.where(top_e[:, :, None] == experts, starts, 0), axis=2)
    n_active = (ends[-1] // tme).astype(I32)
    tile_ids = jnp.arange(n_tiles, dtype=I32)
    tile_expert = jnp.sum(jnp.minimum(tile_ids, n_active - 1)[:, None] >= (ends // tme)[None, :],
                          axis=1).astype(I32)
    tile_expert = jnp.minimum(tile_expert, n_experts - 1)
    tail_start = jnp.maximum(ends - tme, 0).astype(I32)
    tail_valid = (group > 0).astype(I32)

    xs = _dispatch(hn_slabs, pos, tail_start, tail_valid, m_pad=m_pad, tm=_pick_tile(n, 512),
                   tme=tme)
    ys = _experts(xs, tile_expert, n_active.reshape(1), w_gate, b_gate, w_up, b_up, w_down, b_down,
                  tme=tme)
    y = _combine(x1, top_w, pos, ys, p.reshape(n, p.shape[-1]), w_ple_gate, w_ple_proj, g_ple,
                 g_final, tm=_pick_tile(n, 256))
    return y.reshape(b, t, d), (kf, vf, logf, kd, vd)


def kernel(x_prompt, x_sample, p_prompt, p_sample, cache_fox_k, cache_fox_v, cache_fox_logf, cache_diff_k, cache_diff_v, g_mix, w_in, b_forget, diff_lambda, diff_subln_g, rel_bias_table, w_o, g_ffn, w_router, b_router, w_gate, b_gate, w_up, b_up, w_down, b_down, g_ple, w_ple_gate, w_ple_proj, g_final):
    depth = g_mix.shape[0]
    assert depth == 1, "the final norm is fused into the last layer; one layer supported"
    d = x_prompt.shape[-1]
    n_fox = cache_fox_k.shape[3]
    fox_w = n_fox * cache_fox_k.shape[4]
    n_diff = cache_diff_k.shape[3]
    diff_w = n_diff * 2 * cache_diff_k.shape[5]
    assert fox_w == diff_w and fox_w % LANES == 0
    n_experts = w_router.shape[-1]
    assert TOP_K <= n_experts <= LANES
    row2 = lambda a: a.reshape(1, -1).astype(F32)

    i = 0
    lambda_init = 0.8 - 0.6 * math.exp(-0.3 * i)
    off_fk, off_fv, off_ff = fox_w, 2 * fox_w, 3 * fox_w
    off_dq = off_ff + n_fox
    off_dk, off_dv = off_dq + diff_w, off_dq + 2 * diff_w
    w = w_in[i]
    cols = lambda o, wd: w[:, o:o + wd]
    w_main = jnp.concatenate([cols(0, fox_w), cols(off_dq, diff_w), cols(off_fk, fox_w),
                              cols(off_fv, fox_w), cols(off_dk, diff_w), cols(off_dv, diff_w)],
                             axis=1).astype(BF16)
    w_vt = jnp.concatenate([cols(off_fv, fox_w), cols(off_dv, diff_w)], axis=1).T.astype(BF16)
    w_f = jnp.pad(cols(off_ff, n_fox), ((0, 0), (0, LANES - n_fox))).astype(BF16)
    b_f = jnp.pad(row2(b_forget[i]), ((0, 0), (0, LANES - n_fox)))
    w_r = jnp.pad(w_router[i].astype(F32), ((0, 0), (0, LANES - n_experts)))
    w_r_hi = w_r.astype(BF16)
    w_r = jnp.concatenate([w_r_hi, (w_r - w_r_hi.astype(F32)).astype(BF16)], axis=1)
    b_r = jnp.pad(row2(b_router[i]), ((0, 0), (0, LANES - n_experts)), constant_values=NEG_INF)
    wo = w_o[i].astype(BF16)
    bias3 = lambda a: a.reshape(n_experts, 1, -1).astype(F32)
    wts = (row2(g_mix[i]), w_main, w_vt, w_f, b_f, diff_lambda[i].astype(F32), row2(diff_subln_g[i]),
           rel_bias_table.astype(F32), wo[:fox_w], wo[fox_w:], row2(g_ffn[i]), w_r, b_r,
           w_gate[i].astype(BF16), bias3(b_gate[i]), w_up[i].astype(BF16), bias3(b_up[i]),
           w_down[i].astype(BF16), bias3(b_down[i]), row2(g_ple[i]),
           w_ple_gate[i].astype(BF16), w_ple_proj[i].astype(BF16))
    gf = row2(g_final)

    yp, new_p = _layer_group(x_prompt, p_prompt[i], None, wts, lambda_init, gf)
    past_s = (cache_fox_k[i], cache_fox_v[i], cache_fox_logf[i], cache_diff_k[i], cache_diff_v[i])
    ys, new_s = _layer_group(x_sample, p_sample[i], past_s, wts, lambda_init, gf)

    def rows(new, b, t):
        kf, vf, logf, kd, vd = new
        return (kf.reshape(1, b, t, n_fox, -1), vf.reshape(1, b, t, n_fox, -1),
                logf.reshape(1, b, t, n_fox), kd.reshape(1, b, t, n_diff, 2, -1),
                vd.reshape(1, b, t, n_diff, -1))

    bp, tp = x_prompt.shape[:2]
    bs, ts = x_sample.shape[:2]
    return (yp, ys) + rows(new_p, bp, tp) + rows(new_s, bs, ts)
```
